```python
import math
import jax, jax.numpy as jnp
from jax import lax
import numpy as np

D_MODEL = 4096
BATCH = 2
SEQ = 4096
DEPTH = 1

HEAD_DIM = 64
N_Q_HEADS = 32
N_KV_HEADS = 4
GQ = N_Q_HEADS // N_KV_HEADS
ATTN_WIDTH = N_Q_HEADS * HEAD_DIM
KV_WIDTH = N_KV_HEADS * HEAD_DIM
WINDOW = 128
BLK = 128
ROT_DIM = HEAD_DIM // 4
ROPE_THETA = 500000.0

SSM_WIDTH = D_MODEL - ATTN_WIDTH
GROUP_CH = 16
SSM_GROUPS = SSM_WIDTH // GROUP_CH
STATE = 64

MIX_WIDTH = ATTN_WIDTH + SSM_WIDTH
IN_WIDTH = ATTN_WIDTH + 2 * KV_WIDTH + SSM_WIDTH

N_EXPERTS = 32
TOP_K = 4
D_FF = D_MODEL // 4
SWIGLU_LIMIT = 7.0
SWIGLU_ALPHA = 1.702
MOE_BLK = 128

DN_ALPHA = (2.0 * DEPTH) ** 0.25
DN_BETA = (8.0 * DEPTH) ** -0.25
EPS = 1e-5

kernel_name = "hybrid_swa_s5_moe_adaln_deepnorm"


def layer_norm(t, g, b):
    tf = t.astype(jnp.float32)
    mu = jnp.mean(tf, axis=-1, keepdims=True)
    var = jnp.mean(jnp.square(tf - mu), axis=-1, keepdims=True)
    return ((tf - mu) * lax.rsqrt(var + EPS) * g.astype(jnp.float32) + b.astype(jnp.float32)).astype(t.dtype)


def rms_norm(t, g):
    tf = t.astype(jnp.float32)
    return (tf * lax.rsqrt(jnp.mean(jnp.square(tf), axis=-1, keepdims=True) + EPS) * g.astype(jnp.float32)).astype(t.dtype)


def partial_rope(t, cos, sin):
    half = ROT_DIM // 2
    tf = t.astype(jnp.float32)
    t1 = tf[..., :half]
    t2 = tf[..., half:ROT_DIM]
    out = jnp.concatenate([t1 * cos - t2 * sin, t2 * cos + t1 * sin, tf[..., ROT_DIM:]], axis=-1)
    return out.astype(t.dtype)


def swa_attention(q, k, v, sinks):
    bsz, seqlen = q.shape[0], q.shape[1]
    nb = seqlen // BLK
    qb = q.reshape(bsz, nb, BLK, N_KV_HEADS, GQ, HEAD_DIM)
    kb = k.reshape(bsz, nb, BLK, N_KV_HEADS, HEAD_DIM)
    vb = v.reshape(bsz, nb, BLK, N_KV_HEADS, HEAD_DIM)
    pad = ((0, 0), (1, 0), (0, 0), (0, 0), (0, 0))
    kk = jnp.concatenate([jnp.pad(kb, pad)[:, :-1], kb], axis=2)
    vv = jnp.concatenate([jnp.pad(vb, pad)[:, :-1], vb], axis=2)
    s = jnp.einsum('bnqhgd,bnkhd->bnhgqk', qb, kk).astype(jnp.float32) * (HEAD_DIM ** -0.5)
    qi = jnp.arange(BLK)[:, None] + BLK
    kj = jnp.arange(2 * BLK)[None, :]
    rel = qi - kj
    band = (rel >= 0) & (rel < WINDOW)
    blk_idx = jnp.arange(nb)[:, None, None]
    valid = band[None] & ((blk_idx > 0) | (kj[None] >= BLK))
    s = jnp.where(valid[None, :, None, None], s, -1e30)
    sink = sinks.astype(jnp.float32).reshape(N_KV_HEADS, GQ)[None, None, :, :, None, None]
    m = jnp.maximum(jnp.max(s, axis=-1, keepdims=True), sink)
    p = jnp.exp(s - m)
    p = p / (jnp.sum(p, axis=-1, keepdims=True) + jnp.exp(sink - m))
    o = jnp.einsum('bnhgqk,bnkhd->bnqhgd', p.astype(v.dtype), vv)
    return o.reshape(bsz, seqlen, ATTN_WIDTH)


def s5_mixer(u, a_re, a_im, b_re, b_im, c_re, c_im, d_skip, log_dt, w_glu, b_glu):
    bsz, seqlen = u.shape[0], u.shape[1]
    f32 = jnp.float32
    ug = u.reshape(bsz, seqlen, SSM_GROUPS, GROUP_CH).astype(f32)
    A = lax.complex(a_re.astype(f32), a_im.astype(f32))
    dt = jnp.exp(log_dt.astype(f32))[:, None]
    A_bar = jnp.exp(A * dt)
    Bm = lax.complex(b_re.astype(f32), b_im.astype(f32))
    B_bar = ((A_bar - 1.0) / A)[..., None] * Bm
    Bu = jnp.einsum('gpc,blgc->blgp', B_bar, ug.astype(jnp.complex64))
    a_seq = jnp.broadcast_to(A_bar, Bu.shape)

    def combine(e1, e2):
        a1, h1 = e1
        a2, h2 = e2
        return a1 * a2, a2 * h1 + h2

    _, h = lax.associative_scan(combine, (a_seq, Bu), axis=1)
    Cm = lax.complex(c_re.astype(f32), c_im.astype(f32))
    y = jnp.real(jnp.einsum('gcp,blgp->blgc', Cm, h)) + d_skip.astype(f32).reshape(SSM_GROUPS, GROUP_CH) * ug
    y = jax.nn.gelu(y)
    z = jnp.einsum('blgc,gcf->blgf', y, w_glu.astype(f32)) + b_glu.astype(f32)
    out = z[..., :GROUP_CH] * jax.nn.sigmoid(z[..., GROUP_CH:])
    return out.reshape(bsz, seqlen, SSM_WIDTH).astype(u.dtype)


def moe_ffn(h, w_router, b_router, w_gate, b_gate, w_up, b_up, w_down, b_down):
    bsz, seqlen, d = h.shape
    T = bsz * seqlen
    xt = h.reshape(T, d)
    logits = (xt @ w_router + b_router).astype(jnp.float32)
    top_val, top_idx = lax.top_k(logits, TOP_K)
    gates = jax.nn.softmax(top_val, axis=-1)
    n_assign = T * TOP_K
    flat_e = top_idx.reshape(-1).astype(jnp.int32)
    flat_tok = jnp.repeat(jnp.arange(T, dtype=jnp.int32), TOP_K)
    flat_g = gates.reshape(-1)
    order = jnp.argsort(flat_e)
    sorted_e = flat_e[order]
    counts = jnp.bincount(flat_e, length=N_EXPERTS).astype(jnp.int32)
    padded = (counts + MOE_BLK - 1) // MOE_BLK * MOE_BLK
    start = jnp.cumsum(counts) - counts
    pend = jnp.cumsum(padded)
    pstart = pend - padded
    rank = jnp.arange(n_assign, dtype=jnp.int32) - start[sorted_e]
    dest = pstart[sorted_e] + rank
    cap = ((n_assign + MOE_BLK - 1) // MOE_BLK) * MOE_BLK + N_EXPERTS * MOE_BLK
    n_blocks = cap // MOE_BLK
    row_tok = jnp.full((cap,), T, dtype=jnp.int32).at[dest].set(flat_tok[order])
    row_g = jnp.zeros((cap,), jnp.float32).at[dest].set(flat_g[order])
    blk_e = jnp.searchsorted(pend, jnp.arange(n_blocks, dtype=jnp.int32) * MOE_BLK, side='right')
    blk_e = jnp.minimum(blk_e, N_EXPERTS - 1).astype(jnp.int32)
    x_pad = jnp.concatenate([xt, jnp.zeros((1, d), xt.dtype)], axis=0)
    xs = x_pad[row_tok].reshape(n_blocks, MOE_BLK, d)

    def expert_block(args):
        xb, e = args
        g = xb @ w_gate[e] + b_gate[e]
        up = xb @ w_up[e] + b_up[e]
        g = jnp.minimum(g, SWIGLU_LIMIT)
        up = jnp.clip(up, -SWIGLU_LIMIT, SWIGLU_LIMIT)
        act = g * jax.nn.sigmoid(SWIGLU_ALPHA * g) * (up + 1.0)
        return act @ w_down[e] + b_down[e]

    ys = lax.map(expert_block, (xs, blk_e)).reshape(cap, d)
    out = jax.ops.segment_sum(ys.astype(jnp.float32) * row_g[:, None], row_tok, num_segments=T + 1)[:T]
    return out.reshape(bsz, seqlen, d).astype(h.dtype)


def setup_inputs(seed: int = 0) -> dict:
    key = jax.random.key(seed)
    ks = jax.random.split(key, 40)
    f32 = jnp.float32

    def nrm(k, shape, scale):
        return jax.random.normal(k, shape, f32) * scale

    L = DEPTH
    x = nrm(ks[0], (BATCH, SEQ, D_MODEL), 1.0)
    c = nrm(ks[1], (BATCH, D_MODEL), 1.0)
    positions = (jnp.arange(SEQ, dtype=jnp.int32)[None, :]
                 + jax.random.randint(ks[2], (BATCH, 1), 0, 1024, dtype=jnp.int32))
    w_ada = nrm(ks[3], (L, D_MODEL, 6 * D_MODEL), 0.1 * D_MODEL ** -0.5)
    b_ada = nrm(ks[4], (L, 6 * D_MODEL), 0.01)
    w_in = nrm(ks[5], (L, D_MODEL, IN_WIDTH), D_MODEL ** -0.5)
    attn_sinks = nrm(ks[6], (L, N_Q_HEADS), 0.5)
    ssm_a_re = -0.5 * jnp.exp(nrm(ks[7], (L, SSM_GROUPS, STATE), 0.01))
    ssm_a_im = jnp.broadcast_to(math.pi * jnp.arange(STATE, dtype=f32), (L, SSM_GROUPS, STATE)) \
        + nrm(ks[8], (L, SSM_GROUPS, STATE), 0.001)
    ssm_b_re = nrm(ks[9], (L, SSM_GROUPS, STATE, GROUP_CH), (2 * GROUP_CH) ** -0.5)
    ssm_b_im = nrm(ks[10], (L, SSM_GROUPS, STATE, GROUP_CH), (2 * GROUP_CH) ** -0.5)
    ssm_c_re = nrm(ks[11], (L, SSM_GROUPS, GROUP_CH, STATE), (2 * STATE) ** -0.5)
    ssm_c_im = nrm(ks[12], (L, SSM_GROUPS, GROUP_CH, STATE), (2 * STATE) ** -0.5)
    ssm_d = nrm(ks[13], (L, SSM_WIDTH), 1.0)
    ssm_log_dt = jax.random.uniform(ks[14], (L, SSM_GROUPS), f32, math.log(0.001), math.log(0.1))
    ssm_w_glu = nrm(ks[15], (L, SSM_GROUPS, GROUP_CH, 2 * GROUP_CH), GROUP_CH ** -0.5)
    ssm_b_glu = nrm(ks[16], (L, SSM_GROUPS, 2 * GROUP_CH), 0.01)
    g_attn_out = 1.0 + nrm(ks[17], (L, ATTN_WIDTH), 0.01)
    g_ssm_out = 1.0 + nrm(ks[18], (L, SSM_WIDTH), 0.01)
    w_out = nrm(ks[19], (L, MIX_WIDTH, D_MODEL), DN_BETA * MIX_WIDTH ** -0.5)
    ln1_g = 1.0 + nrm(ks[20], (L, D_MODEL), 0.01)
    ln1_b = nrm(ks[21], (L, D_MODEL), 0.01)
    w_router = nrm(ks[22], (L, D_MODEL, N_EXPERTS), D_MODEL ** -0.5)
    b_router = nrm(ks[23], (L, N_EXPERTS), 0.01)
    w_gate = nrm(ks[24], (L, N_EXPERTS, D_MODEL, D_FF), D_MODEL ** -0.5)
    b_gate = nrm(ks[25], (L, N_EXPERTS, D_FF), 0.01)
    w_up = nrm(ks[26], (L, N_EXPERTS, D_MODEL, D_FF), D_MODEL ** -0.5)
    b_up = nrm(ks[27], (L, N_EXPERTS, D_FF), 0.01)
    w_down = nrm(ks[28], (L, N_EXPERTS, D_FF, D_MODEL), DN_BETA * D_FF ** -0.5)
    b_down = nrm(ks[29], (L, N_EXPERTS, D_MODEL), 0.01)
    ln2_g = 1.0 + nrm(ks[30], (L, D_MODEL), 0.01)
    ln2_b = nrm(ks[31], (L, D_MODEL), 0.01)
    return {"x": x, "c": c, "positions": positions, "w_ada": w_ada, "b_ada": b_ada,
            "w_in": w_in, "attn_sinks": attn_sinks,
            "ssm_a_re": ssm_a_re, "ssm_a_im": ssm_a_im, "ssm_b_re": ssm_b_re, "ssm_b_im": ssm_b_im,
            "ssm_c_re": ssm_c_re, "ssm_c_im": ssm_c_im, "ssm_d": ssm_d, "ssm_log_dt": ssm_log_dt,
            "ssm_w_glu": ssm_w_glu, "ssm_b_glu": ssm_b_glu,
            "g_attn_out": g_attn_out, "g_ssm_out": g_ssm_out, "w_out": w_out,
            "ln1_g": ln1_g, "ln1_b": ln1_b, "w_router": w_router, "b_router": b_router,
            "w_gate": w_gate, "b_gate": b_gate, "w_up": w_up, "b_up": b_up,
            "w_down": w_down, "b_down": b_down, "ln2_g": ln2_g, "ln2_b": ln2_b}


def reference(x, c, positions, w_ada, b_ada, w_in, attn_sinks,
              ssm_a_re, ssm_a_im, ssm_b_re, ssm_b_im, ssm_c_re, ssm_c_im, ssm_d, ssm_log_dt,
              ssm_w_glu, ssm_b_glu, g_attn_out, g_ssm_out, w_out, ln1_g, ln1_b,
              w_router, b_router, w_gate, b_gate, w_up, b_up, w_down, b_down, ln2_g, ln2_b):
    bsz, seqlen, _ = x.shape
    inv_freq = ROPE_THETA ** (-jnp.arange(0, ROT_DIM, 2, dtype=jnp.float32) / ROT_DIM)
    ang = positions.astype(jnp.float32)[..., None] * inv_freq
    cos = jnp.cos(ang)[:, :, None, :]
    sin = jnp.sin(ang)[:, :, None, :]
    c_act = jax.nn.silu(c)
    for l in range(DEPTH):
        mod = (c_act @ w_ada[l] + b_ada[l])[:, None, :]
        sh1, sc1, g1, sh2, sc2, g2 = jnp.split(mod, 6, axis=-1)
        h = x * (1.0 + sc1) + sh1
        proj = h @ w_in[l]
        q = proj[..., :ATTN_WIDTH].reshape(bsz, seqlen, N_Q_HEADS, HEAD_DIM)
        k = proj[..., ATTN_WIDTH:ATTN_WIDTH + KV_WIDTH].reshape(bsz, seqlen, N_KV_HEADS, HEAD_DIM)
        v = proj[..., ATTN_WIDTH + KV_WIDTH:ATTN_WIDTH + 2 * KV_WIDTH].reshape(bsz, seqlen, N_KV_HEADS, HEAD_DIM)
        u = proj[..., ATTN_WIDTH + 2 * KV_WIDTH:]
        q = partial_rope(q, cos, sin)
        k = partial_rope(k, cos, sin)
        attn = swa_attention(q, k, v, attn_sinks[l])
        ssm = s5_mixer(u, ssm_a_re[l], ssm_a_im[l], ssm_b_re[l], ssm_b_im[l], ssm_c_re[l], ssm_c_im[l],
                       ssm_d[l], ssm_log_dt[l], ssm_w_glu[l], ssm_b_glu[l])
        mix = jnp.concatenate([rms_norm(attn, g_attn_out[l]), rms_norm(ssm, g_ssm_out[l])], axis=-1) @ w_out[l]
        x = layer_norm(DN_ALPHA * x + (1.0 + g1) * mix, ln1_g[l], ln1_b[l])
        h = x * (1.0 + sc2) + sh2
        ffn = moe_ffn(h, w_router[l], b_router[l], w_gate[l], b_gate[l], w_up[l], b_up[l], w_down[l], b_down[l])
        x = layer_norm(DN_ALPHA * x + (1.0 + g2) * ffn, ln2_g[l], ln2_b[l])
    return x
```

```python
import functools
import math

import jax
import jax.numpy as jnp
from jax import lax
from jax.experimental import pallas as pl
from jax.experimental.pallas import tpu as pltpu

F32 = jnp.float32
BF16 = jnp.bfloat16

HEAD_DIM = 64
N_Q_HEADS = 32
N_KV_HEADS = 4
GQ = N_Q_HEADS // N_KV_HEADS
ATTN_WIDTH = N_Q_HEADS * HEAD_DIM
KV_WIDTH = N_KV_HEADS * HEAD_DIM
QKV_WIDTH = ATTN_WIDTH + 2 * KV_WIDTH
BLK = 128
ROT_DIM = HEAD_DIM // 4
ROPE_THETA = 500000.0
GROUP_CH = 16
STATE = 64
N_EXPERTS = 32
TOP_K = 4
SWIGLU_LIMIT = 7.0
SWIGLU_ALPHA = 1.702
MOE_BLK = 128
DEPTH = 1
DN_ALPHA = (2.0 * DEPTH) ** 0.25
EPS = 1e-5

LANES = 128
SUBLANES = 8
N_SUBSEQ = SUBLANES
GROUPS_PER_BLK = 16
SSM_BLK_IN = GROUPS_PER_BLK * GROUP_CH
SSM_BLK_ST = GROUPS_PER_BLK * STATE
VMEM_LIMIT = 56 * 1024 * 1024


def _cparams(sem, vmem=VMEM_LIMIT):
    return pltpu.CompilerParams(dimension_semantics=sem, vmem_limit_bytes=vmem)


def _resident(shape, index_map):
    return pl.BlockSpec(shape, index_map, pipeline_mode=pl.Buffered(1))


def _ada_kernel(c_ref, w_ref, b_ref, o_ref):
    c = c_ref[...]
    ca = c * jax.nn.sigmoid(c)
    o_ref[...] = jnp.dot(ca.astype(BF16), w_ref[...].astype(BF16),
                         preferred_element_type=F32) + b_ref[...]


def _ada_mod(c, w_ada, b_ada, tn=512):
    bsz, d = c.shape
    n = w_ada.shape[1]
    c8 = jnp.zeros((SUBLANES, d), F32).at[:bsz].set(c)
    out = pl.pallas_call(
        _ada_kernel,
        grid=(n // tn,),
        in_specs=[pl.BlockSpec((SUBLANES, d), lambda j: (0, 0)),
                  pl.BlockSpec((d, tn), lambda j: (0, j)),
                  pl.BlockSpec((1, tn), lambda j: (0, j))],
        out_specs=pl.BlockSpec((SUBLANES, tn), lambda j: (0, j)),
        out_shape=jax.ShapeDtypeStruct((SUBLANES, n), F32),
        compiler_params=_cparams(("arbitrary",)),
        name="ada_mod",
    )(c8, w_ada, b_ada.reshape(1, n))
    return out[:bsz]


def _inproj_kernel(x_ref, mod_ref, w_ref, qkv_ref, u_ref, *, nc):
    m = mod_ref[...]
    h = (x_ref[...] * (1.0 + m[1:2, :]) + m[0:1, :]).astype(BF16)
    n_qkv = qkv_ref.shape[-1]
    n_u = u_ref.shape[-1]
    for n0 in range(0, n_qkv, nc):
        qkv_ref[:, n0:n0 + nc] = jnp.dot(
            h, w_ref[:, n0:n0 + nc], preferred_element_type=F32).astype(BF16)
    for n0 in range(0, n_u, nc):
        u_ref[:, n0:n0 + nc] = jnp.dot(
            h, w_ref[:, n_qkv + n0:n_qkv + n0 + nc], preferred_element_type=F32)


def _in_proj(x, mod3, w_in_bf, lsub, tm=128):
    bsz, seqlen, d = x.shape
    n_in = w_in_bf.shape[1]
    n_u = n_in - QKV_WIDTH
    per = lsub // tm
    return pl.pallas_call(
        functools.partial(_inproj_kernel, nc=512),
        grid=(bsz, seqlen // tm),
        in_specs=[pl.BlockSpec((None, tm, d), lambda b, i: (b, i, 0)),
                  pl.BlockSpec((None, 6, d), lambda b, i: (b, 0, 0)),
                  _resident((d, n_in), lambda b, i: (0, 0))],
        out_specs=[pl.BlockSpec((None, tm, QKV_WIDTH), lambda b, i: (b, i, 0)),
                   pl.BlockSpec((None, tm, n_u), lambda b, i: (b, i % per, i // per))],
        out_shape=[jax.ShapeDtypeStruct((bsz, seqlen, QKV_WIDTH), BF16),
                   jax.ShapeDtypeStruct((bsz, lsub, N_SUBSEQ * n_u), F32)],
        compiler_params=_cparams(("arbitrary", "arbitrary")),
        name="in_proj",
    )(x, mod3, w_in_bf)


def _rope(t, tab):
    c, s_lo, s_hi = tab[:, :LANES], tab[:, LANES:2 * LANES], tab[:, 2 * LANES:]
    half = ROT_DIM // 2
    out = []
    for j in range(t.shape[1] // LANES):
        tj = t[:, j * LANES:(j + 1) * LANES]
        out.append(tj * c + pltpu.roll(tj, LANES - half, 1) * s_lo + pltpu.roll(tj, half, 1) * s_hi)
    return out


def _attn_kernel(sink_ref, q_ref, kc_ref, kp_ref, vc_ref, vp_ref, tc_ref, tp_ref, g_ref, o_ref):
    n = pl.program_id(1)
    low = lax.broadcasted_iota(jnp.int32, (2 * BLK, LANES), 1) < HEAD_DIM
    low_q = lax.broadcasted_iota(jnp.int32, (BLK, LANES), 1) < HEAD_DIM

    q_chunks = _rope(q_ref[...].astype(F32) * (HEAD_DIM ** -0.5), tc_ref[...])
    k_raw = jnp.concatenate([kp_ref[...], kc_ref[...]], axis=0).astype(F32)
    k_chunks = _rope(k_raw, jnp.concatenate([tp_ref[...], tc_ref[...]], axis=0))
    v_raw = jnp.concatenate([vp_ref[...], vc_ref[...]], axis=0).astype(F32)

    qi = lax.broadcasted_iota(jnp.int32, (BLK, 2 * BLK), 0)
    kj = lax.broadcasted_iota(jnp.int32, (BLK, 2 * BLK), 1)
    first_key = jnp.where(n > 0, 0, BLK)
    valid = ((kj < BLK) & (kj > qi) & (kj >= first_key)) | ((kj >= BLK) & ((kj - BLK) <= qi))

    o_chunks = []
    for hk in range(N_KV_HEADS):
        kc = k_chunks[hk // 2]
        vc = v_raw[:, (hk // 2) * LANES:(hk // 2 + 1) * LANES]
        k_sw = pltpu.roll(kc, HEAD_DIM, 1)
        v_sw = pltpu.roll(vc, HEAD_DIM, 1)
        if hk % 2 == 0:
            kk2 = jnp.where(low, kc, k_sw)
            v_lo = jnp.where(low, vc, 0.0)
            v_hi = jnp.where(low, 0.0, v_sw)
        else:
            kk2 = jnp.where(low, k_sw, kc)
            v_lo = jnp.where(low, v_sw, 0.0)
            v_hi = jnp.where(low, 0.0, vc)
        kk2 = kk2.astype(BF16)
        v_lo = v_lo.astype(BF16)
        v_hi = v_hi.astype(BF16)
        lhs = []
        for j in range(GQ // 2):
            q2 = q_chunks[hk * (GQ // 2) + j]
            lhs.append(jnp.where(low_q, q2, 0.0).astype(BF16))
            lhs.append(jnp.where(low_q, 0.0, q2).astype(BF16))
        s_all = lax.dot_general(jnp.concatenate(lhs, axis=0), kk2,
                                (((1,), (1,)), ((), ())), preferred_element_type=F32)
        for j in range(GQ // 2):
            acc = None
            for side, vv in ((0, v_lo), (1, v_hi)):
                i = 2 * j + side
                s = jnp.where(valid, s_all[i * BLK:(i + 1) * BLK], -1e30)
                sink = sink_ref[hk * GQ + i]
                m = jnp.maximum(jnp.max(s, axis=-1, keepdims=True), sink)
                p = jnp.exp(s - m)
                denom = jnp.sum(p, axis=-1, keepdims=True) + jnp.exp(sink - m)
                p = (p * (1.0 / denom)).astype(BF16)
                o = jnp.dot(p, vv, preferred_element_type=F32)
                acc = o if acc is None else acc + o
            o_chunks.append(acc)

    ssq = None
    for oc in o_chunks:
        t = jnp.sum(oc * oc, axis=-1, keepdims=True)
        ssq = t if ssq is None else ssq + t
    inv = lax.rsqrt(ssq * (1.0 / ATTN_WIDTH) + EPS)
    for j, oc in enumerate(o_chunks):
        o_ref[:, j * LANES:(j + 1) * LANES] = (oc * inv * g_ref[:, j * LANES:(j + 1) * LANES]).astype(BF16)


def _attention(qkv, rope_tab, sinks, g_attn):
    bsz, seqlen, _ = qkv.shape
    nb = seqlen // BLK
    kcol = ATTN_WIDTH // KV_WIDTH
    cur = lambda b, n: (b, n, 0)
    prev = lambda b, n: (b, jnp.maximum(n - 1, 0), 0)
    return pl.pallas_call(
        _attn_kernel,
        grid=(bsz, nb),
        in_specs=[pl.BlockSpec(memory_space=pltpu.SMEM),
                  pl.BlockSpec((None, BLK, ATTN_WIDTH), cur),
                  pl.BlockSpec((None, BLK, KV_WIDTH), lambda b, n: (b, n, kcol)),
                  pl.BlockSpec((None, BLK, KV_WIDTH), lambda b, n: (b, jnp.maximum(n - 1, 0), kcol)),
                  pl.BlockSpec((None, BLK, KV_WIDTH), lambda b, n: (b, n, kcol + 1)),
                  pl.BlockSpec((None, BLK, KV_WIDTH), lambda b, n: (b, jnp.maximum(n - 1, 0), kcol + 1)),
                  pl.BlockSpec((None, BLK, 3 * LANES), cur),
                  pl.BlockSpec((None, BLK, 3 * LANES), prev),
                  pl.BlockSpec((1, ATTN_WIDTH), lambda b, n: (0, 0))],
        out_specs=pl.BlockSpec((None, BLK, ATTN_WIDTH), cur),
        out_shape=jax.ShapeDtypeStruct((bsz, seqlen, ATTN_WIDTH), BF16),
        compiler_params=_cparams(("arbitrary", "arbitrary")),
        name="swa_attention",
    )(sinks, qkv, qkv, qkv, qkv, qkv, rope_tab, rope_tab, g_attn.reshape(1, ATTN_WIDTH))


def _rope_tables(positions):
    half = ROT_DIM // 2
    inv_freq = ROPE_THETA ** (-jnp.arange(0, ROT_DIM, 2, dtype=F32) / ROT_DIM)
    ang = positions.astype(F32)[..., None] * inv_freq
    cos, sin = jnp.cos(ang), jnp.sin(ang)
    shp = cos.shape[:-1] + (HEAD_DIM - ROT_DIM,)
    c = jnp.concatenate([cos, cos, jnp.ones(shp, F32)], axis=-1)
    z8 = jnp.zeros_like(sin)
    s_lo = jnp.concatenate([-sin, z8, jnp.zeros(shp, F32)], axis=-1)
    s_hi = jnp.concatenate([z8, sin, jnp.zeros(shp, F32)], axis=-1)
    rep = LANES // HEAD_DIM
    return jnp.concatenate([jnp.tile(c, rep), jnp.tile(s_lo, rep), jnp.tile(s_hi, rep)], axis=-1)


def _s5_scan(buf, ar, ai, hr, hi, ti, store):
    def step(i, carry):
        hr, hi = carry
        r0 = pl.multiple_of(i * SUBLANES, SUBLANES)
        row = buf[pl.ds(r0, SUBLANES), :]
        nhr = ar * hr - ai * hi + row[:, :SSM_BLK_ST]
        nhi = ar * hi + ai * hr + row[:, SSM_BLK_ST:]
        if store:
            buf[pl.ds(r0, SUBLANES), :] = jnp.concatenate([nhr, nhi], axis=1)
        return nhr, nhi
    return lax.fori_loop(0, ti, step, (hr, hi), unroll=4)


def _s5_pass1_kernel(u_ref, bdb_ref, a_ref, f_ref, buf, hst, *, ti):
    ic = pl.program_id(2)

    @pl.when(ic == 0)
    def _():
        hst[...] = jnp.zeros_like(hst)

    u = u_ref[...].reshape(ti * SUBLANES, SSM_BLK_IN)
    buf[...] = jnp.dot(u.astype(BF16), bdb_ref[...], preferred_element_type=F32)
    ar = jnp.broadcast_to(a_ref[0:1, :], (SUBLANES, SSM_BLK_ST))
    ai = jnp.broadcast_to(a_ref[1:2, :], (SUBLANES, SSM_BLK_ST))
    hr, hi = _s5_scan(buf, ar, ai, hst[:, :SSM_BLK_ST], hst[:, SSM_BLK_ST:], ti, store=False)
    hst[...] = jnp.concatenate([hr, hi], axis=1)

    @pl.when(ic == pl.num_programs(2) - 1)
    def _():
        f_ref[...] = hst[...]


def _s5_pass2_kernel(u_ref, f_ref, bdb_ref, a_ref, bdc_ref, glu_ref, vec_ref, o_ref, buf, hst, *, ti):
    ic = pl.program_id(2)

    @pl.when(ic == 0)
    def _():
        fr, fi = f_ref[:, :SSM_BLK_ST], f_ref[:, SSM_BLK_ST:]
        pr = jnp.broadcast_to(a_ref[2:3, :], (SUBLANES, SSM_BLK_ST))
        pi = jnp.broadcast_to(a_ref[3:4, :], (SUBLANES, SSM_BLK_ST))
        row = lax.broadcasted_iota(jnp.int32, (SUBLANES, SSM_BLK_ST), 0)
        hr = jnp.zeros((SUBLANES, SSM_BLK_ST), F32)
        hi = jnp.zeros((SUBLANES, SSM_BLK_ST), F32)
        for _ in range(N_SUBSEQ - 1):
            nr = pr * hr - pi * hi + fr
            ni = pr * hi + pi * hr + fi
            hr = jnp.where(row == 0, 0.0, pltpu.roll(nr, 1, 0))
            hi = jnp.where(row == 0, 0.0, pltpu.roll(ni, 1, 0))
        hst[...] = jnp.concatenate([hr, hi], axis=1)

    u = u_ref[...].reshape(ti * SUBLANES, SSM_BLK_IN)
    buf[...] = jnp.dot(u.astype(BF16), bdb_ref[...], preferred_element_type=F32)
    ar = jnp.broadcast_to(a_ref[0:1, :], (SUBLANES, SSM_BLK_ST))
    ai = jnp.broadcast_to(a_ref[1:2, :], (SUBLANES, SSM_BLK_ST))
    hr, hi = _s5_scan(buf, ar, ai, hst[:, :SSM_BLK_ST], hst[:, SSM_BLK_ST:], ti, store=True)
    hst[...] = jnp.concatenate([hr, hi], axis=1)

    y = jnp.dot(buf[...].astype(BF16), bdc_ref[...], preferred_element_type=F32)
    y = jax.nn.gelu(y + vec_ref[0:1, :SSM_BLK_IN] * u)
    z = jnp.dot(y.astype(BF16), glu_ref[...], preferred_element_type=F32) + vec_ref[1:2, :]
    out = z[:, :SSM_BLK_IN] * jax.nn.sigmoid(z[:, SSM_BLK_IN:])
    o_ref[...] = out.reshape(ti, SUBLANES, SSM_BLK_IN)


def _s5_params(a_re, a_im, b_re, b_im, c_re, c_im, d_skip, log_dt, w_glu, b_glu, lsub):
    g = a_re.shape[0]
    nf = g // GROUPS_PER_BLK
    a = lax.complex(a_re.astype(F32), a_im.astype(F32))
    dt = jnp.exp(log_dt.astype(F32))[:, None]
    a_bar = jnp.exp(a * dt)
    a_pow = jnp.exp(a * dt * lsub)
    b_bar = ((a_bar - 1.0) / a)[..., None] * lax.complex(b_re.astype(F32), b_im.astype(F32))
    eye = jnp.eye(GROUPS_PER_BLK, dtype=F32)

    def bd_in(m):
        m = m.reshape(nf, GROUPS_PER_BLK, STATE, GROUP_CH)
        return jnp.einsum('fgpc,gh->fgchp', m, eye).reshape(nf, SSM_BLK_IN, SSM_BLK_ST)

    def bd_out(m):
        m = m.reshape(nf, GROUPS_PER_BLK, GROUP_CH, STATE)
        return jnp.einsum('fgcp,gh->fgphc', m, eye).reshape(nf, SSM_BLK_ST, SSM_BLK_IN)

    def bd_glu(m):
        m = m.reshape(nf, GROUPS_PER_BLK, GROUP_CH, GROUP_CH)
        return jnp.einsum('fgcd,gh->fgchd', m, eye).reshape(nf, SSM_BLK_IN, SSM_BLK_IN)

    bdb = jnp.concatenate([bd_in(jnp.real(b_bar)), bd_in(jnp.imag(b_bar))], axis=2).astype(BF16)
    bdc = jnp.concatenate([bd_out(c_re.astype(F32)), bd_out(-c_im.astype(F32))], axis=1).astype(BF16)
    wg = w_glu.astype(F32)
    glu = jnp.concatenate([bd_glu(wg[..., :GROUP_CH]), bd_glu(wg[..., GROUP_CH:])], axis=2).astype(BF16)
    flat = lambda m: m.reshape(nf, 1, SSM_BLK_ST)
    avec = jnp.concatenate([flat(jnp.real(a_bar)), flat(jnp.imag(a_bar)),
                            flat(jnp.real(a_pow)), flat(jnp.imag(a_pow))], axis=1)
    bg = b_glu.astype(F32).reshape(nf, GROUPS_PER_BLK, 2 * GROUP_CH)
    bvec = jnp.concatenate([bg[..., :GROUP_CH].reshape(nf, 1, SSM_BLK_IN),
                            bg[..., GROUP_CH:].reshape(nf, 1, SSM_BLK_IN)], axis=2)
    dvec = jnp.concatenate([d_skip.astype(F32).reshape(nf, 1, SSM_BLK_IN),
                            jnp.zeros((nf, 1, SSM_BLK_IN), F32)], axis=2)
    vec = jnp.concatenate([dvec, bvec], axis=1)
    return bdb, bdc, glu, avec, vec


def _s5(u_t, params, ti=128):
    bdb, bdc, glu, avec, vec = params
    bsz, lsub, wide = u_t.shape
    width = wide // N_SUBSEQ
    nf = width // SSM_BLK_IN
    ti = min(ti, lsub)
    u4 = u_t.reshape(bsz, lsub, N_SUBSEQ, width)
    grid = (bsz, nf, lsub // ti)
    u_spec = pl.BlockSpec((None, ti, N_SUBSEQ, SSM_BLK_IN), lambda b, f, i: (b, i, 0, f))
    blk = lambda r, c: pl.BlockSpec((None, r, c), lambda b, f, i: (f, 0, 0))
    f_spec = pl.BlockSpec((None, None, N_SUBSEQ, 2 * SSM_BLK_ST), lambda b, f, i: (b, f, 0, 0))
    scratch = [pltpu.VMEM((ti * SUBLANES, 2 * SSM_BLK_ST), F32),
               pltpu.VMEM((SUBLANES, 2 * SSM_BLK_ST), F32)]
    sem = ("arbitrary", "arbitrary", "arbitrary")
    fin = pl.pallas_call(
        functools.partial(_s5_pass1_kernel, ti=ti),
        grid=grid,
        in_specs=[u_spec, blk(SSM_BLK_IN, 2 * SSM_BLK_ST), blk(4, SSM_BLK_ST)],
        out_specs=f_spec,
        out_shape=jax.ShapeDtypeStruct((bsz, nf, N_SUBSEQ, 2 * SSM_BLK_ST), F32),
        scratch_shapes=scratch,
        compiler_params=_cparams(sem),
        name="s5_pass1",
    )(u4, bdb, avec)
    out = pl.pallas_call(
        functools.partial(_s5_pass2_kernel, ti=ti),
        grid=grid,
        in_specs=[u_spec, f_spec, blk(SSM_BLK_IN, 2 * SSM_BLK_ST), blk(4, SSM_BLK_ST),
                  blk(2 * SSM_BLK_ST, SSM_BLK_IN), blk(SSM_BLK_IN, 2 * SSM_BLK_IN),
                  blk(2, 2 * SSM_BLK_IN)],
        out_specs=u_spec,
        out_shape=jax.ShapeDtypeStruct((bsz, lsub, N_SUBSEQ, width), F32),
        scratch_shapes=scratch,
        compiler_params=_cparams(sem),
        name="s5_pass2",
    )(u4, fin, bdb, avec, bdc, glu, vec)
    return out.reshape(bsz, lsub, wide)


def _layer_norm_rows(y, g, b):
    mu = jnp.mean(y, axis=-1, keepdims=True)
    yc = y - mu
    var = jnp.mean(yc * yc, axis=-1, keepdims=True)
    return yc * lax.rsqrt(var + EPS) * g + b


def _outproj_kernel(attn_ref, ssm_ref, gs_ref, w_ref, x_ref, mod_ref, ln_ref, wr_ref, br_ref,
                    x1_ref, h2_ref, idx_ref, gate_ref, ybuf, *, nc):
    m = mod_ref[...]
    ssm = ssm_ref[...]
    ms = jnp.mean(ssm * ssm, axis=-1, keepdims=True)
    ssm_n = (ssm * lax.rsqrt(ms + EPS) * gs_ref[...]).astype(BF16)
    attn = attn_ref[...]
    ka = attn.shape[1]
    d = x_ref.shape[1]
    for n0 in range(0, d, nc):
        mix = (jnp.dot(attn, w_ref[:ka, n0:n0 + nc], preferred_element_type=F32)
               + jnp.dot(ssm_n, w_ref[ka:, n0:n0 + nc], preferred_element_type=F32))
        ybuf[:, n0:n0 + nc] = DN_ALPHA * x_ref[:, n0:n0 + nc] + (1.0 + m[2:3, n0:n0 + nc]) * mix
    x1 = _layer_norm_rows(ybuf[...], ln_ref[0:1, :], ln_ref[1:2, :])
    x1_ref[...] = x1
    h2 = x1 * (1.0 + m[4:5, :]) + m[3:4, :]
    h2_ref[...] = h2
    hi = h2.astype(BF16)
    lo = (h2 - hi.astype(F32)).astype(BF16)
    tm = h2.shape[0]
    r = jnp.dot(jnp.concatenate([hi, lo], axis=0), wr_ref[...], preferred_element_type=F32)
    logits = r[:tm, :N_EXPERTS] + r[:tm, N_EXPERTS:] + r[tm:, :N_EXPERTS] + br_ref[...]
    lane = lax.broadcasted_iota(jnp.int32, logits.shape, 1)
    vals, idxs = [], []
    for _ in range(TOP_K):
        mx = jnp.max(logits, axis=-1, keepdims=True)
        ix = jnp.min(jnp.where(logits == mx, lane, N_EXPERTS), axis=-1, keepdims=True)
        vals.append(mx)
        idxs.append(ix)
        logits = jnp.where(lane == ix, -jnp.inf, logits)
    tv = jnp.concatenate(vals, axis=1)
    e = jnp.exp(tv - vals[0])
    gate_ref[...] = e / jnp.sum(e, axis=-1, keepdims=True)
    idx_ref[...] = jnp.concatenate(idxs, axis=1)


def _out_proj(attn_n, ssm_t, g_ssm, w_out_bf, x, mod3, ln1, wr, br, lsub, tm=128):
    bsz, seqlen, d = x.shape
    ka = attn_n.shape[-1]
    ks = w_out_bf.shape[0] - ka
    per = lsub // tm
    row = lambda b, i: (b, i, 0)
    const = lambda b, i: (0, 0)
    return pl.pallas_call(
        functools.partial(_outproj_kernel, nc=512),
        grid=(bsz, seqlen // tm),
        in_specs=[pl.BlockSpec((None, tm, ka), row),
                  pl.BlockSpec((None, tm, ks), lambda b, i: (b, i % per, i // per)),
                  pl.BlockSpec((1, ks), const),
                  _resident((ka + ks, d), const),
                  pl.BlockSpec((None, tm, d), row),
                  pl.BlockSpec((None, 6, d), lambda b, i: (b, 0, 0)),
                  pl.BlockSpec((2, d), const),
                  pl.BlockSpec((d, 2 * N_EXPERTS), const),
                  pl.BlockSpec((1, N_EXPERTS), const)],
        out_specs=[pl.BlockSpec((None, tm, d), row),
                   pl.BlockSpec((None, tm, d), row),
                   pl.BlockSpec((None, tm, TOP_K), row),
                   pl.BlockSpec((None, tm, TOP_K), row)],
        out_shape=[jax.ShapeDtypeStruct((bsz, seqlen, d), F32),
                   jax.ShapeDtypeStruct((bsz, seqlen, d), F32),
                   jax.ShapeDtypeStruct((bsz, seqlen, TOP_K), jnp.int32),
                   jax.ShapeDtypeStruct((bsz, seqlen, TOP_K), F32)],
        scratch_shapes=[pltpu.VMEM((tm, d), F32)],
        compiler_params=_cparams(("arbitrary", "arbitrary")),
        name="out_proj_ln_router",
    )(attn_n, ssm_t, g_ssm.reshape(1, ks), w_out_bf, x, mod3, ln1, wr, br)


def _expert_up_kernel(blk_e, first, nreal, xs_ref, wg_ref, wu_ref, bg_ref, bu_ref, act_ref, wgb, wub):
    rb = pl.program_id(1)

    @pl.when(first[rb] == 1)
    def _():
        wgb[...] = wg_ref[...].astype(BF16)
        wub[...] = wu_ref[...].astype(BF16)

    @pl.when(rb < nreal[0])
    def _():
        x = xs_ref[...].astype(BF16)
        g = jnp.dot(x, wgb[...], preferred_element_type=F32) + bg_ref[...]
        up = jnp.dot(x, wub[...], preferred_element_type=F32) + bu_ref[...]
        g = jnp.minimum(g, SWIGLU_LIMIT)
        up = jnp.clip(up, -SWIGLU_LIMIT, SWIGLU_LIMIT)
        act_ref[...] = (g * jax.nn.sigmoid(SWIGLU_ALPHA * g) * (up + 1.0)).astype(BF16)

    @pl.when(rb >= nreal[0])
    def _():
        act_ref[...] = jnp.zeros_like(act_ref)


def _expert_down_kernel(blk_e, first, nreal, act_ref, wd_ref, bd_ref, y_ref, wdb):
    rb = pl.program_id(1)

    @pl.when(first[rb] == 1)
    def _():
        wdb[...] = wd_ref[...].astype(BF16)

    @pl.when(rb < nreal[0])
    def _():
        y_ref[...] = jnp.dot(act_ref[...], wdb[...], preferred_element_type=F32) + bd_ref[...]

    @pl.when(rb >= nreal[0])
    def _():
        y_ref[...] = jnp.zeros_like(y_ref)


def _experts(xs, blk_e, first, nreal, w_gate, b_gate, w_up, b_up, w_down, b_down, tf=512, tn=2048):
    cap, d = xs.shape
    n_e, _, dff = w_gate.shape
    nblk = cap // MOE_BLK
    act = pl.pallas_call(
        _expert_up_kernel,
        grid_spec=pltpu.PrefetchScalarGridSpec(
            num_scalar_prefetch=3,
            grid=(dff // tf, nblk),
            in_specs=[pl.BlockSpec((MOE_BLK, d), lambda f, r, be, fi, nr: (r, 0)),
                      pl.BlockSpec((None, d, tf), lambda f, r, be, fi, nr: (be[r], 0, f)),
                      pl.BlockSpec((None, d, tf), lambda f, r, be, fi, nr: (be[r], 0, f)),
                      pl.BlockSpec((None, 1, tf), lambda f, r, be, fi, nr: (be[r], 0, f)),
                      pl.BlockSpec((None, 1, tf), lambda f, r, be, fi, nr: (be[r], 0, f))],
            out_specs=pl.BlockSpec((MOE_BLK, tf), lambda f, r, be, fi, nr: (r, f)),
            scratch_shapes=[pltpu.VMEM((d, tf), BF16), pltpu.VMEM((d, tf), BF16)]),
        out_shape=jax.ShapeDtypeStruct((cap, dff), BF16),
        compiler_params=_cparams(("arbitrary", "arbitrary")),
        name="expert_gate_up",
    )(blk_e, first, nreal, xs, w_gate, w_up, b_gate.reshape(n_e, 1, dff), b_up.reshape(n_e, 1, dff))
    ys = pl.pallas_call(
        _expert_down_kernel,
        grid_spec=pltpu.PrefetchScalarGridSpec(
            num_scalar_prefetch=3,
            grid=(d // tn, nblk),
            in_specs=[pl.BlockSpec((MOE_BLK, dff), lambda n, r, be, fi, nr: (r, 0)),
                      pl.BlockSpec((None, dff, tn), lambda n, r, be, fi, nr: (be[r], 0, n)),
                      pl.BlockSpec((None, 1, tn), lambda n, r, be, fi, nr: (be[r], 0, n))],
            out_specs=pl.BlockSpec((MOE_BLK, tn), lambda n, r, be, fi, nr: (r, n)),
            scratch_shapes=[pltpu.VMEM((dff, tn), BF16)]),
        out_shape=jax.ShapeDtypeStruct((cap, d), F32),
        compiler_params=_cparams(("arbitrary", "arbitrary")),
        name="expert_down",
    )(blk_e, first, nreal, act, w_down, b_down.reshape(n_e, 1, d))
    return ys


def _routing(top_idx, n_tok):
    n_assign = n_tok * TOP_K
    flat_e = top_idx.reshape(-1)
    onehot = (flat_e[:, None] == jnp.arange(N_EXPERTS, dtype=jnp.int32)[None, :]).astype(jnp.int32)
    csum = jnp.cumsum(onehot, axis=0)
    rank = jnp.sum(csum * onehot, axis=1) - 1
    counts = csum[-1]
    padded = (counts + MOE_BLK - 1) // MOE_BLK * MOE_BLK
    pend = jnp.cumsum(padded)
    pstart = pend - padded
    dest = jnp.sum(onehot * pstart[None, :], axis=1) + rank
    cap = ((n_assign + MOE_BLK - 1) // MOE_BLK) * MOE_BLK + N_EXPERTS * MOE_BLK
    nblk = cap // MOE_BLK
    blk_e = jnp.searchsorted(pend, jnp.arange(nblk, dtype=jnp.int32) * MOE_BLK, side='right')
    blk_e = jnp.minimum(blk_e, N_EXPERTS - 1).astype(jnp.int32)
    first = jnp.concatenate([jnp.ones((1,), jnp.int32), (blk_e[1:] != blk_e[:-1]).astype(jnp.int32)])
    nreal = (pend[-1:] // MOE_BLK).astype(jnp.int32)
    return dest.astype(jnp.int32), cap, blk_e, first, nreal


def _final_ln_kernel(x1_ref, moe_ref, mod_ref, ln_ref, o_ref):
    m = mod_ref[...]
    y = DN_ALPHA * x1_ref[...] + (1.0 + m[5:6, :]) * moe_ref[...]
    o_ref[...] = _layer_norm_rows(y, ln_ref[0:1, :], ln_ref[1:2, :])


def _final_ln(x1, moe, mod3, ln2, tm=256):
    bsz, seqlen, d = x1.shape
    row = lambda b, i: (b, i, 0)
    return pl.pallas_call(
        _final_ln_kernel,
        grid=(bsz, seqlen // tm),
        in_specs=[pl.BlockSpec((None, tm, d), row),
                  pl.BlockSpec((None, tm, d), row),
                  pl.BlockSpec((None, 6, d), lambda b, i: (b, 0, 0)),
                  pl.BlockSpec((2, d), lambda b, i: (0, 0))],
        out_specs=pl.BlockSpec((None, tm, d), row),
        out_shape=jax.ShapeDtypeStruct((bsz, seqlen, d), F32),
        compiler_params=_cparams(("arbitrary", "arbitrary")),
        name="final_ln",
    )(x1, moe, mod3, ln2)


def kernel(x, c, positions, w_ada, b_ada, w_in, attn_sinks, ssm_a_re, ssm_a_im, ssm_b_re, ssm_b_im,
           ssm_c_re, ssm_c_im, ssm_d, ssm_log_dt, ssm_w_glu, ssm_b_glu, g_attn_out, g_ssm_out, w_out,
           ln1_g, ln1_b, w_router, b_router, w_gate, b_gate, w_up, b_up, w_down, b_down, ln2_g, ln2_b):
    bsz, seqlen, d = x.shape
    lsub = seqlen // N_SUBSEQ
    n_tok = bsz * seqlen
    rope_tab = _rope_tables(positions)
    for l in range(w_ada.shape[0]):
        mod3 = _ada_mod(c, w_ada[l], b_ada[l]).reshape(bsz, 6, d)
        qkv, u_t = _in_proj(x, mod3, w_in[l].astype(BF16), lsub)
        attn_n = _attention(qkv, rope_tab, attn_sinks[l].astype(F32), g_attn_out[l].astype(F32))
        s5p = _s5_params(ssm_a_re[l], ssm_a_im[l], ssm_b_re[l], ssm_b_im[l], ssm_c_re[l], ssm_c_im[l],
                         ssm_d[l], ssm_log_dt[l], ssm_w_glu[l], ssm_b_glu[l], lsub)
        ssm_t = _s5(u_t, s5p)
        wr_hi = w_router[l].astype(BF16)
        wr_lo = (w_router[l] - wr_hi.astype(F32)).astype(BF16)
        x1, h2, top_idx, gates = _out_proj(
            attn_n, ssm_t, g_ssm_out[l].astype(F32), w_out[l].astype(BF16), x, mod3,
            jnp.stack([ln1_g[l], ln1_b[l]]).astype(F32),
            jnp.concatenate([wr_hi, wr_lo], axis=1), b_router[l].reshape(1, N_EXPERTS).astype(F32), lsub)
        dest, cap, blk_e, first, nreal = _routing(top_idx, n_tok)
        flat_tok = jnp.repeat(jnp.arange(n_tok, dtype=jnp.int32), TOP_K)
        row_tok = jnp.full((cap,), n_tok, jnp.int32).at[dest].set(flat_tok)
        h2_pad = jnp.concatenate([h2.reshape(n_tok, d), jnp.zeros((1, d), F32)], axis=0)
        xs = h2_pad[row_tok]
        ys = _experts(xs, blk_e, first, nreal, w_gate[l], b_gate[l], w_up[l], b_up[l], w_down[l], b_down[l])
        picked = jnp.take(ys, dest.reshape(n_tok, TOP_K), axis=0)
        moe = jnp.sum(picked * gates.reshape(n_tok, TOP_K, 1), axis=1).reshape(bsz, seqlen, d)
        x = _final_ln(x1, moe, mod3, jnp.stack([ln2_g[l], ln2_b[l]]).astype(F32))
    return x
```

```python
import functools
import math

import jax
import jax.numpy as jnp
from jax import lax
from jax.experimental import pallas as pl
from jax.experimental.pallas import tpu as pltpu

F32 = jnp.float32
BF16 = jnp.bfloat16

HEAD_DIM = 64
N_Q_HEADS = 32
N_KV_HEADS = 4
GQ = N_Q_HEADS // N_KV_HEADS
ATTN_WIDTH = N_Q_HEADS * HEAD_DIM
KV_WIDTH = N_KV_HEADS * HEAD_DIM
QKV_WIDTH = ATTN_WIDTH + 2 * KV_WIDTH
BLK = 128
ROT_DIM = HEAD_DIM // 4
ROPE_THETA = 500000.0
GROUP_CH = 16
STATE = 64
N_EXPERTS = 32
TOP_K = 4
SWIGLU_LIMIT = 7.0
SWIGLU_ALPHA = 1.702
EXPERT_BLK = 256
ROUTE_TILE = 128
MOVE_TILE = 256
DOWN_TILE = 2048
DEPTH = 1
DN_ALPHA = (2.0 * DEPTH) ** 0.25
EPS = 1e-5

LANES = 128
SUBLANES = 8
N_SUBSEQ = SUBLANES
GROUPS_PER_BLK = 16
SSM_BLK_IN = GROUPS_PER_BLK * GROUP_CH
SSM_BLK_ST = GROUPS_PER_BLK * STATE
VMEM_LIMIT = 56 * 1024 * 1024


def _cparams(sem, vmem=VMEM_LIMIT):
    return pltpu.CompilerParams(dimension_semantics=sem, vmem_limit_bytes=vmem)


def _resident(shape, index_map):
    return pl.BlockSpec(shape, index_map, pipeline_mode=pl.Buffered(1))


def _ada_kernel(c_ref, w_ref, b_ref, o_ref):
    c = c_ref[...]
    ca = c * jax.nn.sigmoid(c)
    o_ref[...] = jnp.dot(ca.astype(BF16), w_ref[...].astype(BF16),
                         preferred_element_type=F32) + b_ref[...]


def _ada_mod(c, w_ada, b_ada, tn=512):
    bsz, d = c.shape
    n = w_ada.shape[1]
    c8 = jnp.zeros((SUBLANES, d), F32).at[:bsz].set(c)
    out = pl.pallas_call(
        _ada_kernel,
        grid=(n // tn,),
        in_specs=[pl.BlockSpec((SUBLANES, d), lambda j: (0, 0)),
                  pl.BlockSpec((d, tn), lambda j: (0, j)),
                  pl.BlockSpec((1, tn), lambda j: (0, j))],
        out_specs=pl.BlockSpec((SUBLANES, tn), lambda j: (0, j)),
        out_shape=jax.ShapeDtypeStruct((SUBLANES, n), F32),
        compiler_params=_cparams(("arbitrary",)),
        name="ada_mod",
    )(c8, w_ada, b_ada.reshape(1, n))
    return out[:bsz]


def _inproj_kernel(x_ref, mod_ref, w_ref, o_ref, *, nc):
    m = mod_ref[...]
    h = (x_ref[...] * (1.0 + m[1:2, :]) + m[0:1, :]).astype(BF16)
    for n0 in range(0, o_ref.shape[-1], nc):
        o_ref[:, n0:n0 + nc] = jnp.dot(
            h, w_ref[:, n0:n0 + nc], preferred_element_type=F32).astype(BF16)


def _in_proj(x, mod3, w_in_bf, tm=128):
    bsz, seqlen, d = x.shape
    n_in = w_in_bf.shape[1]
    return pl.pallas_call(
        functools.partial(_inproj_kernel, nc=512),
        grid=(bsz, seqlen // tm),
        in_specs=[pl.BlockSpec((None, tm, d), lambda b, i: (b, i, 0)),
                  pl.BlockSpec((None, 6, d), lambda b, i: (b, 0, 0)),
                  _resident((d, n_in), lambda b, i: (0, 0))],
        out_specs=pl.BlockSpec((None, tm, n_in), lambda b, i: (b, i, 0)),
        out_shape=jax.ShapeDtypeStruct((bsz, seqlen, n_in), BF16),
        compiler_params=_cparams(("arbitrary", "arbitrary")),
        name="in_proj",
    )(x, mod3, w_in_bf)


def _rope(t, tab):
    c, s_lo, s_hi = tab[:, :LANES], tab[:, LANES:2 * LANES], tab[:, 2 * LANES:]
    half = ROT_DIM // 2
    out = []
    for j in range(t.shape[1] // LANES):
        tj = t[:, j * LANES:(j + 1) * LANES]
        out.append(tj * c + pltpu.roll(tj, LANES - half, 1) * s_lo + pltpu.roll(tj, half, 1) * s_hi)
    return out


def _attn_kernel(sink_ref, q_ref, kc_ref, kp_ref, vc_ref, vp_ref, tc_ref, tp_ref, g_ref, o_ref):
    n = pl.program_id(1)
    low = lax.broadcasted_iota(jnp.int32, (2 * BLK, LANES), 1) < HEAD_DIM
    low_q = lax.broadcasted_iota(jnp.int32, (BLK, LANES), 1) < HEAD_DIM

    q_chunks = _rope(q_ref[...].astype(F32) * (HEAD_DIM ** -0.5), tc_ref[...])
    k_raw = jnp.concatenate([kp_ref[...], kc_ref[...]], axis=0).astype(F32)
    k_chunks = _rope(k_raw, jnp.concatenate([tp_ref[...], tc_ref[...]], axis=0))
    v_raw = jnp.concatenate([vp_ref[...], vc_ref[...]], axis=0).astype(F32)

    qi = lax.broadcasted_iota(jnp.int32, (BLK, 2 * BLK), 0)
    kj = lax.broadcasted_iota(jnp.int32, (BLK, 2 * BLK), 1)
    first_key = jnp.where(n > 0, 0, BLK)
    valid = ((kj < BLK) & (kj > qi) & (kj >= first_key)) | ((kj >= BLK) & ((kj - BLK) <= qi))

    o_chunks = []
    for hk in range(N_KV_HEADS):
        kc = k_chunks[hk // 2]
        vc = v_raw[:, (hk // 2) * LANES:(hk // 2 + 1) * LANES]
        k_sw = pltpu.roll(kc, HEAD_DIM, 1)
        v_sw = pltpu.roll(vc, HEAD_DIM, 1)
        if hk % 2 == 0:
            kk2 = jnp.where(low, kc, k_sw)
            v_lo = jnp.where(low, vc, 0.0)
            v_hi = jnp.where(low, 0.0, v_sw)
        else:
            kk2 = jnp.where(low, k_sw, kc)
            v_lo = jnp.where(low, v_sw, 0.0)
            v_hi = jnp.where(low, 0.0, vc)
        kk2 = kk2.astype(BF16)
        v_lo = v_lo.astype(BF16)
        v_hi = v_hi.astype(BF16)
        lhs = []
        for j in range(GQ // 2):
            q2 = q_chunks[hk * (GQ // 2) + j]
            lhs.append(jnp.where(low_q, q2, 0.0).astype(BF16))
            lhs.append(jnp.where(low_q, 0.0, q2).astype(BF16))
        s_all = lax.dot_general(jnp.concatenate(lhs, axis=0), kk2,
                                (((1,), (1,)), ((), ())), preferred_element_type=F32)
        for j in range(GQ // 2):
            acc = None
            for side, vv in ((0, v_lo), (1, v_hi)):
                i = 2 * j + side
                s = jnp.where(valid, s_all[i * BLK:(i + 1) * BLK], -1e30)
                sink = sink_ref[hk * GQ + i]
                m = jnp.maximum(jnp.max(s, axis=-1, keepdims=True), sink)
                p = jnp.exp(s - m)
                denom = jnp.sum(p, axis=-1, keepdims=True) + jnp.exp(sink - m)
                p = (p * (1.0 / denom)).astype(BF16)
                o = jnp.dot(p, vv, preferred_element_type=F32)
                acc = o if acc is None else acc + o
            o_chunks.append(acc)

    ssq = None
    for oc in o_chunks:
        t = jnp.sum(oc * oc, axis=-1, keepdims=True)
        ssq = t if ssq is None else ssq + t
    inv = lax.rsqrt(ssq * (1.0 / ATTN_WIDTH) + EPS)
    for j, oc in enumerate(o_chunks):
        o_ref[:, j * LANES:(j + 1) * LANES] = (oc * inv * g_ref[:, j * LANES:(j + 1) * LANES]).astype(BF16)


def _attention(qkv, rope_tab, sinks, g_attn):
    bsz, seqlen, _ = qkv.shape
    nb = seqlen // BLK
    kcol = ATTN_WIDTH // KV_WIDTH
    cur = lambda b, n: (b, n, 0)
    prev = lambda b, n: (b, jnp.maximum(n - 1, 0), 0)
    return pl.pallas_call(
        _attn_kernel,
        grid=(bsz, nb),
        in_specs=[pl.BlockSpec(memory_space=pltpu.SMEM),
                  pl.BlockSpec((None, BLK, ATTN_WIDTH), cur),
                  pl.BlockSpec((None, BLK, KV_WIDTH), lambda b, n: (b, n, kcol)),
                  pl.BlockSpec((None, BLK, KV_WIDTH), lambda b, n: (b, jnp.maximum(n - 1, 0), kcol)),
                  pl.BlockSpec((None, BLK, KV_WIDTH), lambda b, n: (b, n, kcol + 1)),
                  pl.BlockSpec((None, BLK, KV_WIDTH), lambda b, n: (b, jnp.maximum(n - 1, 0), kcol + 1)),
                  pl.BlockSpec((None, BLK, 3 * LANES), cur),
                  pl.BlockSpec((None, BLK, 3 * LANES), prev),
                  pl.BlockSpec((1, ATTN_WIDTH), lambda b, n: (0, 0))],
        out_specs=pl.BlockSpec((None, BLK, ATTN_WIDTH), cur),
        out_shape=jax.ShapeDtypeStruct((bsz, seqlen, ATTN_WIDTH), BF16),
        compiler_params=_cparams(("arbitrary", "arbitrary")),
        name="swa_attention",
    )(sinks, qkv, qkv, qkv, qkv, qkv, rope_tab, rope_tab, g_attn.reshape(1, ATTN_WIDTH))


def _rope_tables(positions):
    half = ROT_DIM // 2
    inv_freq = ROPE_THETA ** (-jnp.arange(0, ROT_DIM, 2, dtype=F32) / ROT_DIM)
    ang = positions.astype(F32)[..., None] * inv_freq
    cos, sin = jnp.cos(ang), jnp.sin(ang)
    shp = cos.shape[:-1] + (HEAD_DIM - ROT_DIM,)
    c = jnp.concatenate([cos, cos, jnp.ones(shp, F32)], axis=-1)
    z8 = jnp.zeros_like(sin)
    s_lo = jnp.concatenate([-sin, z8, jnp.zeros(shp, F32)], axis=-1)
    s_hi = jnp.concatenate([z8, sin, jnp.zeros(shp, F32)], axis=-1)
    rep = LANES // HEAD_DIM
    return jnp.concatenate([jnp.tile(c, rep), jnp.tile(s_lo, rep), jnp.tile(s_hi, rep)], axis=-1)


def _s5_scan(buf, ar, ai, hr, hi, ti, store):
    def step(i, carry):
        hr, hi = carry
        r0 = pl.multiple_of(i * SUBLANES, SUBLANES)
        row = buf[pl.ds(r0, SUBLANES), :]
        nhr = ar * hr - ai * hi + row[:, :SSM_BLK_ST]
        nhi = ar * hi + ai * hr + row[:, SSM_BLK_ST:]
        if store:
            buf[pl.ds(r0, SUBLANES), :] = jnp.concatenate([nhr, nhi], axis=1)
        return nhr, nhi
    return lax.fori_loop(0, ti, step, (hr, hi), unroll=4)


def _time_major(u_ref, ti):
    u = pltpu.einshape("jid->ijd", u_ref[...].astype(F32))
    return u.reshape(ti * N_SUBSEQ, u.shape[-1])


def _s5_pass1_kernel(u_ref, bdb_ref, a_ref, f_ref, buf, hst, *, ti):
    ic = pl.program_id(2)

    @pl.when(ic == 0)
    def _():
        hst[...] = jnp.zeros_like(hst)

    u = _time_major(u_ref, ti)
    buf[...] = jnp.dot(u.astype(BF16), bdb_ref[...], preferred_element_type=F32)
    ar = jnp.broadcast_to(a_ref[0:1, :], (SUBLANES, SSM_BLK_ST))
    ai = jnp.broadcast_to(a_ref[1:2, :], (SUBLANES, SSM_BLK_ST))
    hr, hi = _s5_scan(buf, ar, ai, hst[:, :SSM_BLK_ST], hst[:, SSM_BLK_ST:], ti, store=False)
    hst[...] = jnp.concatenate([hr, hi], axis=1)

    @pl.when(ic == pl.num_programs(2) - 1)
    def _():
        f_ref[...] = hst[...]


def _s5_pass2_kernel(u_ref, f_ref, bdb_ref, a_ref, bdc_ref, glu_ref, vec_ref, o_ref, buf, hst, *, ti):
    ic = pl.program_id(2)

    @pl.when(ic == 0)
    def _():
        fr, fi = f_ref[:, :SSM_BLK_ST], f_ref[:, SSM_BLK_ST:]
        pr = jnp.broadcast_to(a_ref[2:3, :], (SUBLANES, SSM_BLK_ST))
        pi = jnp.broadcast_to(a_ref[3:4, :], (SUBLANES, SSM_BLK_ST))
        row = lax.broadcasted_iota(jnp.int32, (SUBLANES, SSM_BLK_ST), 0)
        hr = jnp.zeros((SUBLANES, SSM_BLK_ST), F32)
        hi = jnp.zeros((SUBLANES, SSM_BLK_ST), F32)
        for _ in range(N_SUBSEQ - 1):
            nr = pr * hr - pi * hi + fr
            ni = pr * hi + pi * hr + fi
            hr = jnp.where(row == 0, 0.0, pltpu.roll(nr, 1, 0))
            hi = jnp.where(row == 0, 0.0, pltpu.roll(ni, 1, 0))
        hst[...] = jnp.concatenate([hr, hi], axis=1)

    u = _time_major(u_ref, ti)
    buf[...] = jnp.dot(u.astype(BF16), bdb_ref[...], preferred_element_type=F32)
    ar = jnp.broadcast_to(a_ref[0:1, :], (SUBLANES, SSM_BLK_ST))
    ai = jnp.broadcast_to(a_ref[1:2, :], (SUBLANES, SSM_BLK_ST))
    hr, hi = _s5_scan(buf, ar, ai, hst[:, :SSM_BLK_ST], hst[:, SSM_BLK_ST:], ti, store=True)
    hst[...] = jnp.concatenate([hr, hi], axis=1)

    y = jnp.dot(buf[...].astype(BF16), bdc_ref[...], preferred_element_type=F32)
    y = jax.nn.gelu(y + vec_ref[0:1, :SSM_BLK_IN] * u)
    z = jnp.dot(y.astype(BF16), glu_ref[...], preferred_element_type=F32) + vec_ref[1:2, :]
    out = z[:, :SSM_BLK_IN] * jax.nn.sigmoid(z[:, SSM_BLK_IN:])
    out = pltpu.einshape("ijd->jid", out.reshape(ti, N_SUBSEQ, SSM_BLK_IN))
    o_ref[...] = out.astype(BF16)


def _s5_params(a_re, a_im, b_re, b_im, c_re, c_im, d_skip, log_dt, w_glu, b_glu, lsub):
    g = a_re.shape[0]
    nf = g // GROUPS_PER_BLK
    a = lax.complex(a_re.astype(F32), a_im.astype(F32))
    dt = jnp.exp(log_dt.astype(F32))[:, None]
    a_bar = jnp.exp(a * dt)
    a_pow = jnp.exp(a * dt * lsub)
    b_bar = ((a_bar - 1.0) / a)[..., None] * lax.complex(b_re.astype(F32), b_im.astype(F32))
    eye = jnp.eye(GROUPS_PER_BLK, dtype=F32)

    def bd_in(m):
        m = m.reshape(nf, GROUPS_PER_BLK, STATE, GROUP_CH)
        return jnp.einsum('fgpc,gh->fgchp', m, eye).reshape(nf, SSM_BLK_IN, SSM_BLK_ST)

    def bd_out(m):
        m = m.reshape(nf, GROUPS_PER_BLK, GROUP_CH, STATE)
        return jnp.einsum('fgcp,gh->fgphc', m, eye).reshape(nf, SSM_BLK_ST, SSM_BLK_IN)

    def bd_glu(m):
        m = m.reshape(nf, GROUPS_PER_BLK, GROUP_CH, GROUP_CH)
        return jnp.einsum('fgcd,gh->fgchd', m, eye).reshape(nf, SSM_BLK_IN, SSM_BLK_IN)

    bdb = jnp.concatenate([bd_in(jnp.real(b_bar)), bd_in(jnp.imag(b_bar))], axis=2).astype(BF16)
    bdc = jnp.concatenate([bd_out(c_re.astype(F32)), bd_out(-c_im.astype(F32))], axis=1).astype(BF16)
    wg = w_glu.astype(F32)
    glu = jnp.concatenate([bd_glu(wg[..., :GROUP_CH]), bd_glu(wg[..., GROUP_CH:])], axis=2).astype(BF16)
    flat = lambda m: m.reshape(nf, 1, SSM_BLK_ST)
    avec = jnp.concatenate([flat(jnp.real(a_bar)), flat(jnp.imag(a_bar)),
                            flat(jnp.real(a_pow)), flat(jnp.imag(a_pow))], axis=1)
    bg = b_glu.astype(F32).reshape(nf, GROUPS_PER_BLK, 2 * GROUP_CH)
    bvec = jnp.concatenate([bg[..., :GROUP_CH].reshape(nf, 1, SSM_BLK_IN),
                            bg[..., GROUP_CH:].reshape(nf, 1, SSM_BLK_IN)], axis=2)
    dvec = jnp.concatenate([d_skip.astype(F32).reshape(nf, 1, SSM_BLK_IN),
                            jnp.zeros((nf, 1, SSM_BLK_IN), F32)], axis=2)
    vec = jnp.concatenate([dvec, bvec], axis=1)
    return bdb, bdc, glu, avec, vec


def _s5(proj, params, ti=128):
    bdb, bdc, glu, avec, vec = params
    bsz, seqlen, n_in = proj.shape
    width = n_in - QKV_WIDTH
    nf = width // SSM_BLK_IN
    lsub = seqlen // N_SUBSEQ
    ti = min(ti, lsub)
    u_col0 = QKV_WIDTH // SSM_BLK_IN
    p4 = proj.reshape(bsz, N_SUBSEQ, lsub, n_in)
    grid = (bsz, nf, lsub // ti)
    u_spec = pl.BlockSpec((None, N_SUBSEQ, ti, SSM_BLK_IN), lambda b, f, i: (b, 0, i, u_col0 + f))
    o_spec = pl.BlockSpec((None, N_SUBSEQ, ti, SSM_BLK_IN), lambda b, f, i: (b, 0, i, f))
    blk = lambda r, c: pl.BlockSpec((None, r, c), lambda b, f, i: (f, 0, 0))
    f_spec = pl.BlockSpec((None, None, N_SUBSEQ, 2 * SSM_BLK_ST), lambda b, f, i: (b, f, 0, 0))
    scratch = [pltpu.VMEM((ti * SUBLANES, 2 * SSM_BLK_ST), F32),
               pltpu.VMEM((SUBLANES, 2 * SSM_BLK_ST), F32)]
    sem = ("arbitrary", "arbitrary", "arbitrary")
    fin = pl.pallas_call(
        functools.partial(_s5_pass1_kernel, ti=ti),
        grid=grid,
        in_specs=[u_spec, blk(SSM_BLK_IN, 2 * SSM_BLK_ST), blk(4, SSM_BLK_ST)],
        out_specs=f_spec,
        out_shape=jax.ShapeDtypeStruct((bsz, nf, N_SUBSEQ, 2 * SSM_BLK_ST), F32),
        scratch_shapes=scratch,
        compiler_params=_cparams(sem),
        name="s5_pass1",
    )(p4, bdb, avec)
    out = pl.pallas_call(
        functools.partial(_s5_pass2_kernel, ti=ti),
        grid=grid,
        in_specs=[u_spec, f_spec, blk(SSM_BLK_IN, 2 * SSM_BLK_ST), blk(4, SSM_BLK_ST),
                  blk(2 * SSM_BLK_ST, SSM_BLK_IN), blk(SSM_BLK_IN, 2 * SSM_BLK_IN),
                  blk(2, 2 * SSM_BLK_IN)],
        out_specs=o_spec,
        out_shape=jax.ShapeDtypeStruct((bsz, N_SUBSEQ, lsub, width), BF16),
        scratch_shapes=scratch,
        compiler_params=_cparams(sem),
        name="s5_pass2",
    )(p4, fin, bdb, avec, bdc, glu, vec)
    return out.reshape(bsz, seqlen, width)


def _layer_norm_rows(y, g, b):
    mu = jnp.mean(y, axis=-1, keepdims=True)
    yc = y - mu
    var = jnp.mean(yc * yc, axis=-1, keepdims=True)
    return yc * lax.rsqrt(var + EPS) * g + b


def _pack_bf16_pairs(lo, hi):
    lo_bits = lax.bitcast_convert_type(lo.astype(BF16).astype(F32), jnp.uint32)
    hi_bits = lax.bitcast_convert_type(hi.astype(BF16).astype(F32), jnp.uint32)
    return (lo_bits >> 16) | (hi_bits & jnp.uint32(0xFFFF0000))


def _unpack_bf16_pairs(w):
    lo = lax.bitcast_convert_type(w << 16, F32)
    hi = lax.bitcast_convert_type(w & jnp.uint32(0xFFFF0000), F32)
    return lo, hi


def _outproj_kernel(attn_ref, ssm_ref, gs_ref, w_ref, x_ref, mod_ref, ln_ref, wr_ref, br_ref,
                    x1_ref, hp_ref, idx_ref, gate_ref, rank_ref, cnt_ref, ybuf, *, nc):
    m = mod_ref[...]
    ssm = ssm_ref[...].astype(F32)
    ms = jnp.mean(ssm * ssm, axis=-1, keepdims=True)
    ssm_n = (ssm * lax.rsqrt(ms + EPS) * gs_ref[...]).astype(BF16)
    attn = attn_ref[...]
    ka = attn.shape[1]
    d = x_ref.shape[1]
    for n0 in range(0, d, nc):
        mix = (jnp.dot(attn, w_ref[:ka, n0:n0 + nc], preferred_element_type=F32)
               + jnp.dot(ssm_n, w_ref[ka:, n0:n0 + nc], preferred_element_type=F32))
        ybuf[:, n0:n0 + nc] = DN_ALPHA * x_ref[:, n0:n0 + nc] + (1.0 + m[2:3, n0:n0 + nc]) * mix
    x1 = _layer_norm_rows(ybuf[...], ln_ref[0:1, :], ln_ref[1:2, :])
    x1_ref[...] = x1
    h2 = x1 * (1.0 + m[4:5, :]) + m[3:4, :]
    hp_ref[...] = _pack_bf16_pairs(h2[:, :d // 2], h2[:, d // 2:])
    hi = h2.astype(BF16)
    lo = (h2 - hi.astype(F32)).astype(BF16)
    tm = h2.shape[0]
    r = jnp.dot(jnp.concatenate([hi, lo], axis=0), wr_ref[...], preferred_element_type=F32)
    logits = r[:tm, :N_EXPERTS] + r[:tm, N_EXPERTS:] + r[tm:, :N_EXPERTS] + br_ref[...]
    lane = lax.broadcasted_iota(jnp.int32, logits.shape, 1)
    vals, idxs = [], []
    for _ in range(TOP_K):
        mx = jnp.max(logits, axis=-1, keepdims=True)
        ix = jnp.min(jnp.where(logits == mx, lane, N_EXPERTS), axis=-1, keepdims=True)
        vals.append(mx)
        idxs.append(ix)
        logits = jnp.where(lane == ix, -jnp.inf, logits)
    tv = jnp.concatenate(vals, axis=1)
    e = jnp.exp(tv - vals[0])
    gate_ref[...] = e / jnp.sum(e, axis=-1, keepdims=True)
    idx_ref[...] = jnp.concatenate(idxs, axis=1)
    tri = (lax.broadcasted_iota(jnp.int32, (tm, tm), 0) > lax.broadcasted_iota(jnp.int32, (tm, tm), 1)).astype(BF16)
    run = jnp.zeros((1, N_EXPERTS), F32)
    ranks = []
    for ix in idxs:
        onehot = (lane == ix).astype(F32)
        before = jnp.dot(tri, onehot.astype(BF16), preferred_element_type=F32) + run
        ranks.append(jnp.sum(onehot * before, axis=-1, keepdims=True))
        run = run + jnp.sum(onehot, axis=0, keepdims=True)
    rank_ref[...] = jnp.concatenate(ranks, axis=1).astype(jnp.int32)
    cnt_ref[...] = jnp.zeros_like(cnt_ref)
    cnt_ref[0:1, 0:N_EXPERTS] = run


def _out_proj(attn_n, ssm, g_ssm, w_out_bf, x, mod3, ln1, wr, br, tm=ROUTE_TILE):
    bsz, seqlen, d = x.shape
    ka = attn_n.shape[-1]
    ks = w_out_bf.shape[0] - ka
    nt = seqlen // tm
    row = lambda b, i: (b, i, 0)
    const = lambda b, i: (0, 0)
    return pl.pallas_call(
        functools.partial(_outproj_kernel, nc=512),
        grid=(bsz, nt),
        in_specs=[pl.BlockSpec((None, tm, ka), row),
                  pl.BlockSpec((None, tm, ks), row),
                  pl.BlockSpec((1, ks), const),
                  _resident((ka + ks, d), const),
                  pl.BlockSpec((None, tm, d), row),
                  pl.BlockSpec((None, 6, d), lambda b, i: (b, 0, 0)),
                  pl.BlockSpec((2, d), const),
                  pl.BlockSpec((d, 2 * N_EXPERTS), const),
                  pl.BlockSpec((1, N_EXPERTS), const)],
        out_specs=[pl.BlockSpec((None, tm, d), row),
                   pl.BlockSpec((None, tm, d // 2), row),
                   pl.BlockSpec((None, tm, TOP_K), row),
                   pl.BlockSpec((None, tm, TOP_K), row),
                   pl.BlockSpec((None, tm, TOP_K), row),
                   pl.BlockSpec((None, SUBLANES, LANES), lambda b, i: (b * nt + i, 0, 0))],
        out_shape=[jax.ShapeDtypeStruct((bsz, seqlen, d), F32),
                   jax.ShapeDtypeStruct((bsz, seqlen, d // 2), jnp.uint32),
                   jax.ShapeDtypeStruct((bsz, seqlen, TOP_K), jnp.int32),
                   jax.ShapeDtypeStruct((bsz, seqlen, TOP_K), F32),
                   jax.ShapeDtypeStruct((bsz, seqlen, TOP_K), jnp.int32),
                   jax.ShapeDtypeStruct((bsz * nt, SUBLANES, LANES), F32)],
        scratch_shapes=[pltpu.VMEM((tm, d), F32)],
        compiler_params=_cparams(("arbitrary", "arbitrary")),
        name="out_proj_ln_router",
    )(attn_n, ssm, g_ssm.reshape(1, ks), w_out_bf, x, mod3, ln1, wr, br)


def _expert_up_kernel(blk_e, first, nreal, xs_ref, wg_ref, wu_ref, bg_ref, bu_ref, act_ref, wgb, wub):
    rb = pl.program_id(1)

    @pl.when(first[rb] == 1)
    def _():
        wgb[...] = wg_ref[...].astype(BF16)
        wub[...] = wu_ref[...].astype(BF16)

    @pl.when(rb < nreal[0])
    def _():
        x = jnp.concatenate(_unpack_bf16_pairs(xs_ref[...]), axis=1).astype(BF16)
        g = jnp.dot(x, wgb[...], preferred_element_type=F32) + bg_ref[...]
        up = jnp.dot(x, wub[...], preferred_element_type=F32) + bu_ref[...]
        g = jnp.minimum(g, SWIGLU_LIMIT)
        up = jnp.clip(up, -SWIGLU_LIMIT, SWIGLU_LIMIT)
        act_ref[...] = (g * jax.nn.sigmoid(SWIGLU_ALPHA * g) * (up + 1.0)).astype(BF16)

    @pl.when(rb >= nreal[0])
    def _():
        act_ref[...] = jnp.zeros_like(act_ref)


def _expert_down_kernel(blk_e, first, nreal, act_ref, wd_ref, bd_ref, y_ref, wdb):
    rb = pl.program_id(1)

    @pl.when(first[rb] == 1)
    def _():
        wdb[...] = wd_ref[...].astype(BF16)

    @pl.when(rb < nreal[0])
    def _():
        y = jnp.dot(act_ref[...], wdb[...], preferred_element_type=F32) + bd_ref[...]
        half = y.shape[1] // 2
        y_ref[...] = _pack_bf16_pairs(y[:, :half], y[:, half:])

    @pl.when(rb >= nreal[0])
    def _():
        y_ref[...] = jnp.zeros_like(y_ref)


def _experts(xs, blk_e, first, nreal, w_gate, b_gate, w_up, b_up, w_down, b_down, tf=512, tn=DOWN_TILE):
    cap = xs.shape[0]
    n_e, d, dff = w_gate.shape
    nblk = cap // EXPERT_BLK
    act = pl.pallas_call(
        _expert_up_kernel,
        grid_spec=pltpu.PrefetchScalarGridSpec(
            num_scalar_prefetch=3,
            grid=(dff // tf, nblk),
            in_specs=[pl.BlockSpec((EXPERT_BLK, d // 2), lambda f, r, be, fi, nr: (r, 0)),
                      pl.BlockSpec((None, d, tf), lambda f, r, be, fi, nr: (be[r], 0, f)),
                      pl.BlockSpec((None, d, tf), lambda f, r, be, fi, nr: (be[r], 0, f)),
                      pl.BlockSpec((None, 1, tf), lambda f, r, be, fi, nr: (be[r], 0, f)),
                      pl.BlockSpec((None, 1, tf), lambda f, r, be, fi, nr: (be[r], 0, f))],
            out_specs=pl.BlockSpec((EXPERT_BLK, tf), lambda f, r, be, fi, nr: (r, f)),
            scratch_shapes=[pltpu.VMEM((d, tf), BF16), pltpu.VMEM((d, tf), BF16)]),
        out_shape=jax.ShapeDtypeStruct((cap, dff), BF16),
        compiler_params=_cparams(("arbitrary", "arbitrary")),
        name="expert_gate_up",
    )(blk_e, first, nreal, xs, w_gate, w_up, b_gate.reshape(n_e, 1, dff), b_up.reshape(n_e, 1, dff))
    ys = pl.pallas_call(
        _expert_down_kernel,
        grid_spec=pltpu.PrefetchScalarGridSpec(
            num_scalar_prefetch=3,
            grid=(d // tn, nblk),
            in_specs=[pl.BlockSpec((EXPERT_BLK, dff), lambda n, r, be, fi, nr: (r, 0)),
                      pl.BlockSpec((None, dff, tn), lambda n, r, be, fi, nr: (be[r], 0, n)),
                      pl.BlockSpec((None, 1, tn), lambda n, r, be, fi, nr: (be[r], 0, n))],
            out_specs=pl.BlockSpec((EXPERT_BLK, tn // 2), lambda n, r, be, fi, nr: (r, n)),
            scratch_shapes=[pltpu.VMEM((dff, tn), BF16)]),
        out_shape=jax.ShapeDtypeStruct((cap, d // 2), jnp.uint32),
        compiler_params=_cparams(("arbitrary", "arbitrary")),
        name="expert_down",
    )(blk_e, first, nreal, act, w_down, b_down.reshape(n_e, 1, d))
    return ys


def _route_tables(cnt, n_tok):
    counts = cnt[:, 0, :N_EXPERTS].astype(jnp.int32)
    tot = jnp.sum(counts, axis=0)
    padded = (tot + EXPERT_BLK - 1) // EXPERT_BLK * EXPERT_BLK
    pend = jnp.cumsum(padded)
    pstart = pend - padded
    tile_off = jnp.cumsum(counts, axis=0) - counts
    base = (pstart[None, :] + tile_off).reshape(-1).astype(jnp.int32)
    n_assign = n_tok * TOP_K
    cap = ((n_assign + EXPERT_BLK - 1) // EXPERT_BLK) * EXPERT_BLK + N_EXPERTS * EXPERT_BLK
    nblk = cap // EXPERT_BLK
    blk_e = jnp.searchsorted(pend, jnp.arange(nblk, dtype=jnp.int32) * EXPERT_BLK, side='right')
    blk_e = jnp.minimum(blk_e, N_EXPERTS - 1).astype(jnp.int32)
    first = jnp.concatenate([jnp.ones((1,), jnp.int32), (blk_e[1:] != blk_e[:-1]).astype(jnp.int32)])
    nreal = (pend[-1:] // EXPERT_BLK).astype(jnp.int32)
    return base, cap, blk_e, first, nreal, (pstart + tot).astype(jnp.int32), (padded - tot).astype(jnp.int32)


def _row_copy(src, src_row, dst, dst_row, sem):
    return pltpu.make_async_copy(src.at[pl.ds(src_row, 1)], dst.at[pl.ds(dst_row, 1)], sem)


def _dispatch_kernel(base_ref, pad0_ref, padn_ref, nreal_ref, hp_ref, idx_ref, rank_ref, xs_ref, dest_ref,
                     zblk, sem, zsem, bsem):
    i = pl.program_id(0)
    tm = hp_ref.shape[0]
    nblk = xs_ref.shape[0] // EXPERT_BLK

    def for_each_pad_row(fn):
        def per_expert(e, c):
            def per_row(r, c2):
                fn(_row_copy(zblk, 0, xs_ref, pad0_ref[e] + r, zsem))
                return c2
            return lax.fori_loop(0, padn_ref[e], per_row, c)
        lax.fori_loop(0, N_EXPERTS, per_expert, 0)

    def for_each_unused_block(fn):
        def per_block(b, c):
            r0 = pl.multiple_of(b * EXPERT_BLK, EXPERT_BLK)
            fn(pltpu.make_async_copy(zblk, xs_ref.at[pl.ds(r0, EXPERT_BLK)], bsem))
            return c
        lax.fori_loop(nreal_ref[0], nblk, per_block, 0)

    @pl.when(i == 0)
    def _():
        zblk[...] = jnp.zeros_like(zblk)
        for_each_pad_row(lambda cp: cp.start())
        for_each_unused_block(lambda cp: cp.start())

    def issue(r, c):
        tile = i * (tm // ROUTE_TILE) + r // ROUTE_TILE
        for k in range(TOP_K):
            a = r * TOP_K + k
            dst = base_ref[tile * N_EXPERTS + idx_ref[a]] + rank_ref[a]
            dest_ref[a] = dst
            _row_copy(hp_ref, r, xs_ref, dst, sem).start()
        return c
    lax.fori_loop(0, tm, issue, 0)

    def drain(r, c):
        for k in range(TOP_K):
            _row_copy(hp_ref, 0, xs_ref, 0, sem).wait()
        return c
    lax.fori_loop(0, tm, drain, 0)

    @pl.when(i == 0)
    def _():
        for_each_pad_row(lambda cp: cp.wait())
        for_each_unused_block(lambda cp: cp.wait())


def _dispatch(hp, top_idx, rank, base, pad0, padn, nreal, cap, tm=MOVE_TILE):
    n_tok, half = hp.shape
    n_a = tm * TOP_K
    smem_blk = lambda: pl.BlockSpec((n_a,), lambda i, *_: (i,), memory_space=pltpu.SMEM)
    return pl.pallas_call(
        _dispatch_kernel,
        grid_spec=pltpu.PrefetchScalarGridSpec(
            num_scalar_prefetch=4,
            grid=(n_tok // tm,),
            in_specs=[pl.BlockSpec((tm, half), lambda i, *_: (i, 0)), smem_blk(), smem_blk()],
            out_specs=[pl.BlockSpec(memory_space=pl.ANY), smem_blk()],
            scratch_shapes=[pltpu.VMEM((EXPERT_BLK, half), jnp.uint32),
                            pltpu.SemaphoreType.DMA, pltpu.SemaphoreType.DMA, pltpu.SemaphoreType.DMA]),
        out_shape=[jax.ShapeDtypeStruct((cap, half), jnp.uint32),
                   jax.ShapeDtypeStruct((n_tok * TOP_K,), jnp.int32)],
        compiler_params=_cparams(("arbitrary",)),
        name="moe_dispatch",
    )(base, pad0, padn, nreal, hp, top_idx.reshape(-1), rank.reshape(-1))


def _combine_kernel(dest_ref, gate_ref, x1_ref, mod_ref, ln_ref, ys_ref, o_ref, buf, sem, *, tn):
    tm = x1_ref.shape[0]

    def issue(r, c):
        for k in range(TOP_K):
            pltpu.make_async_copy(ys_ref.at[pl.ds(dest_ref[r * TOP_K + k], 1)],
                                  buf.at[k, pl.ds(r, 1)], sem).start()
        return c
    lax.fori_loop(0, tm, issue, 0)

    def drain(r, c):
        for k in range(TOP_K):
            pltpu.make_async_copy(ys_ref.at[pl.ds(0, 1)], buf.at[0, pl.ds(0, 1)], sem).wait()
        return c
    lax.fori_loop(0, tm, drain, 0)

    g = gate_ref[...]
    m = mod_ref[...]
    d = x1_ref.shape[1]
    hw = tn // 2
    for n in range(d // tn):
        lo = hi = None
        for k in range(TOP_K):
            wl, wh = _unpack_bf16_pairs(buf[k, :, n * hw:(n + 1) * hw])
            lo = g[:, k:k + 1] * wl if lo is None else lo + g[:, k:k + 1] * wl
            hi = g[:, k:k + 1] * wh if hi is None else hi + g[:, k:k + 1] * wh
        for half, moe in ((0, lo), (1, hi)):
            c0 = n * tn + half * hw
            o_ref[:, c0:c0 + hw] = DN_ALPHA * x1_ref[:, c0:c0 + hw] + (1.0 + m[5:6, c0:c0 + hw]) * moe
    o_ref[...] = _layer_norm_rows(o_ref[...], ln_ref[0:1, :], ln_ref[1:2, :])


def _combine(ys, dest, gates, x1, mod3, ln2, tm=MOVE_TILE, tn=DOWN_TILE):
    bsz, seqlen, d = x1.shape
    n_tok = bsz * seqlen
    per_b = seqlen // tm
    out = pl.pallas_call(
        functools.partial(_combine_kernel, tn=tn),
        grid=(n_tok // tm,),
        in_specs=[pl.BlockSpec((tm * TOP_K,), lambda i: (i,), memory_space=pltpu.SMEM),
                  pl.BlockSpec((tm, TOP_K), lambda i: (i, 0)),
                  pl.BlockSpec((tm, d), lambda i: (i, 0)),
                  pl.BlockSpec((None, 6, d), lambda i: (i // per_b, 0, 0)),
                  pl.BlockSpec((2, d), lambda i: (0, 0)),
                  pl.BlockSpec(memory_space=pl.ANY)],
        out_specs=pl.BlockSpec((tm, d), lambda i: (i, 0)),
        out_shape=jax.ShapeDtypeStruct((n_tok, d), F32),
        scratch_shapes=[pltpu.VMEM((TOP_K, tm, d // 2), jnp.uint32), pltpu.SemaphoreType.DMA],
        compiler_params=_cparams(("arbitrary",)),
        name="moe_combine_ln",
    )(dest, gates.reshape(n_tok, TOP_K), x1.reshape(n_tok, d), mod3, ln2, ys)
    return out.reshape(bsz, seqlen, d)


def kernel(x, c, positions, w_ada, b_ada, w_in, attn_sinks, ssm_a_re, ssm_a_im, ssm_b_re, ssm_b_im,
           ssm_c_re, ssm_c_im, ssm_d, ssm_log_dt, ssm_w_glu, ssm_b_glu, g_attn_out, g_ssm_out, w_out,
           ln1_g, ln1_b, w_router, b_router, w_gate, b_gate, w_up, b_up, w_down, b_down, ln2_g, ln2_b):
    bsz, seqlen, d = x.shape
    lsub = seqlen // N_SUBSEQ
    n_tok = bsz * seqlen
    rope_tab = _rope_tables(positions)
    for l in range(w_ada.shape[0]):
        mod3 = _ada_mod(c, w_ada[l], b_ada[l]).reshape(bsz, 6, d)
        proj = _in_proj(x, mod3, w_in[l].astype(BF16))
        attn_n = _attention(proj, rope_tab, attn_sinks[l].astype(F32), g_attn_out[l].astype(F32))
        s5p = _s5_params(ssm_a_re[l], ssm_a_im[l], ssm_b_re[l], ssm_b_im[l], ssm_c_re[l], ssm_c_im[l],
                         ssm_d[l], ssm_log_dt[l], ssm_w_glu[l], ssm_b_glu[l], lsub)
        ssm = _s5(proj, s5p)
        wr_hi = w_router[l].astype(BF16)
        wr_lo = (w_router[l] - wr_hi.astype(F32)).astype(BF16)
        x1, hp, top_idx, gates, rank, cnt = _out_proj(
            attn_n, ssm, g_ssm_out[l].astype(F32), w_out[l].astype(BF16), x, mod3,
            jnp.stack([ln1_g[l], ln1_b[l]]).astype(F32),
            jnp.concatenate([wr_hi, wr_lo], axis=1), b_router[l].reshape(1, N_EXPERTS).astype(F32))
        base, cap, blk_e, first, nreal, pad0, padn = _route_tables(cnt, n_tok)
        xs, dest = _dispatch(hp.reshape(n_tok, d // 2), top_idx, rank, base, pad0, padn, nreal, cap)
        ys = _experts(xs, blk_e, first, nreal, w_gate[l], b_gate[l], w_up[l], b_up[l], w_down[l], b_down[l])
        x = _combine(ys, dest, gates, x1, mod3, jnp.stack([ln2_g[l], ln2_b[l]]).astype(F32))
    return x
```

```python
import functools
import math

import jax
import jax.numpy as jnp
from jax import lax
from jax.experimental import pallas as pl
from jax.experimental.pallas import tpu as pltpu

F32 = jnp.float32
BF16 = jnp.bfloat16

HEAD_DIM = 64
N_Q_HEADS = 32
N_KV_HEADS = 4
GQ = N_Q_HEADS // N_KV_HEADS
ATTN_WIDTH = N_Q_HEADS * HEAD_DIM
KV_WIDTH = N_KV_HEADS * HEAD_DIM
QKV_WIDTH = ATTN_WIDTH + 2 * KV_WIDTH
BLK = 128
ROT_DIM = HEAD_DIM // 4
ROPE_THETA = 500000.0
GROUP_CH = 16
STATE = 64
N_EXPERTS = 32
TOP_K = 4
SWIGLU_LIMIT = 7.0
SWIGLU_ALPHA = 1.702
EXPERT_BLK = 256
ROUTE_TILE = 128
MOVE_TILE = 256
DOWN_TILE = 4096
DEPTH = 1
DN_ALPHA = (2.0 * DEPTH) ** 0.25
EPS = 1e-5

LANES = 128
SUBLANES = 8
N_SUBSEQ = SUBLANES
GROUPS_PER_BLK = 16
SSM_BLK_IN = GROUPS_PER_BLK * GROUP_CH
SSM_BLK_ST = GROUPS_PER_BLK * STATE
VMEM_LIMIT = 56 * 1024 * 1024


def _cparams(sem, vmem=VMEM_LIMIT):
    return pltpu.CompilerParams(dimension_semantics=sem, vmem_limit_bytes=vmem)


def _resident(shape, index_map):
    return pl.BlockSpec(shape, index_map, pipeline_mode=pl.Buffered(1))


def _ada_kernel(c_ref, w_ref, b_ref, o_ref):
    c = c_ref[...]
    ca = c * jax.nn.sigmoid(c)
    o_ref[...] = jnp.dot(ca.astype(BF16), w_ref[...].astype(BF16),
                         preferred_element_type=F32) + b_ref[...]


def _ada_mod(c, w_ada, b_ada, tn=512):
    bsz, d = c.shape
    n = w_ada.shape[1]
    c8 = jnp.zeros((SUBLANES, d), F32).at[:bsz].set(c)
    out = pl.pallas_call(
        _ada_kernel,
        grid=(n // tn,),
        in_specs=[pl.BlockSpec((SUBLANES, d), lambda j: (0, 0)),
                  pl.BlockSpec((d, tn), lambda j: (0, j)),
                  pl.BlockSpec((1, tn), lambda j: (0, j))],
        out_specs=pl.BlockSpec((SUBLANES, tn), lambda j: (0, j)),
        out_shape=jax.ShapeDtypeStruct((SUBLANES, n), F32),
        compiler_params=_cparams(("arbitrary",)),
        name="ada_mod",
    )(c8, w_ada, b_ada.reshape(1, n))
    return out[:bsz]


def _inproj_kernel(x_ref, mod_ref, w_ref, o_ref, *, nc):
    m = mod_ref[...]
    h = (x_ref[...] * (1.0 + m[1:2, :]) + m[0:1, :]).astype(BF16)
    for n0 in range(0, o_ref.shape[-1], nc):
        o_ref[:, n0:n0 + nc] = jnp.dot(
            h, w_ref[:, n0:n0 + nc], preferred_element_type=F32).astype(BF16)


def _in_proj(x, mod3, w_in_bf, tm=128):
    bsz, seqlen, d = x.shape
    n_in = w_in_bf.shape[1]
    return pl.pallas_call(
        functools.partial(_inproj_kernel, nc=512),
        grid=(bsz, seqlen // tm),
        in_specs=[pl.BlockSpec((None, tm, d), lambda b, i: (b, i, 0)),
                  pl.BlockSpec((None, 6, d), lambda b, i: (b, 0, 0)),
                  _resident((d, n_in), lambda b, i: (0, 0))],
        out_specs=pl.BlockSpec((None, tm, n_in), lambda b, i: (b, i, 0)),
        out_shape=jax.ShapeDtypeStruct((bsz, seqlen, n_in), BF16),
        compiler_params=_cparams(("arbitrary", "arbitrary")),
        name="in_proj",
    )(x, mod3, w_in_bf)


def _rope(t, tab):
    c, s_lo, s_hi = tab[:, :LANES], tab[:, LANES:2 * LANES], tab[:, 2 * LANES:]
    half = ROT_DIM // 2
    out = []
    for j in range(t.shape[1] // LANES):
        tj = t[:, j * LANES:(j + 1) * LANES]
        out.append(tj * c + pltpu.roll(tj, LANES - half, 1) * s_lo + pltpu.roll(tj, half, 1) * s_hi)
    return out


def _attn_kernel(sink_ref, q_ref, kc_ref, kp_ref, vc_ref, vp_ref, tc_ref, tp_ref, g_ref, o_ref):
    n = pl.program_id(1)
    low = lax.broadcasted_iota(jnp.int32, (2 * BLK, LANES), 1) < HEAD_DIM
    low_q = lax.broadcasted_iota(jnp.int32, (BLK, LANES), 1) < HEAD_DIM

    q_chunks = _rope(q_ref[...].astype(F32) * (HEAD_DIM ** -0.5), tc_ref[...])
    k_raw = jnp.concatenate([kp_ref[...], kc_ref[...]], axis=0).astype(F32)
    k_chunks = _rope(k_raw, jnp.concatenate([tp_ref[...], tc_ref[...]], axis=0))
    v_raw = jnp.concatenate([vp_ref[...], vc_ref[...]], axis=0).astype(F32)

    qi = lax.broadcasted_iota(jnp.int32, (BLK, 2 * BLK), 0)
    kj = lax.broadcasted_iota(jnp.int32, (BLK, 2 * BLK), 1)
    first_key = jnp.where(n > 0, 0, BLK)
    valid = ((kj < BLK) & (kj > qi) & (kj >= first_key)) | ((kj >= BLK) & ((kj - BLK) <= qi))

    o_chunks = []
    for hk in range(N_KV_HEADS):
        kc = k_chunks[hk // 2]
        vc = v_raw[:, (hk // 2) * LANES:(hk // 2 + 1) * LANES]
        k_sw = pltpu.roll(kc, HEAD_DIM, 1)
        v_sw = pltpu.roll(vc, HEAD_DIM, 1)
        if hk % 2 == 0:
            kk2 = jnp.where(low, kc, k_sw)
            v_lo = jnp.where(low, vc, 0.0)
            v_hi = jnp.where(low, 0.0, v_sw)
        else:
            kk2 = jnp.where(low, k_sw, kc)
            v_lo = jnp.where(low, v_sw, 0.0)
            v_hi = jnp.where(low, 0.0, vc)
        kk2 = kk2.astype(BF16)
        v_lo = v_lo.astype(BF16)
        v_hi = v_hi.astype(BF16)
        lhs = []
        for j in range(GQ // 2):
            q2 = q_chunks[hk * (GQ // 2) + j]
            lhs.append(jnp.where(low_q, q2, 0.0).astype(BF16))
            lhs.append(jnp.where(low_q, 0.0, q2).astype(BF16))
        s_all = lax.dot_general(jnp.concatenate(lhs, axis=0), kk2,
                                (((1,), (1,)), ((), ())), preferred_element_type=F32)
        for j in range(GQ // 2):
            acc = None
            for side, vv in ((0, v_lo), (1, v_hi)):
                i = 2 * j + side
                s = jnp.where(valid, s_all[i * BLK:(i + 1) * BLK], -1e30)
                sink = sink_ref[hk * GQ + i]
                m = jnp.maximum(jnp.max(s, axis=-1, keepdims=True), sink)
                p = jnp.exp(s - m)
                denom = jnp.sum(p, axis=-1, keepdims=True) + jnp.exp(sink - m)
                p = (p * (1.0 / denom)).astype(BF16)
                o = jnp.dot(p, vv, preferred_element_type=F32)
                acc = o if acc is None else acc + o
            o_chunks.append(acc)

    ssq = None
    for oc in o_chunks:
        t = jnp.sum(oc * oc, axis=-1, keepdims=True)
        ssq = t if ssq is None else ssq + t
    inv = lax.rsqrt(ssq * (1.0 / ATTN_WIDTH) + EPS)
    for j, oc in enumerate(o_chunks):
        o_ref[:, j * LANES:(j + 1) * LANES] = (oc * inv * g_ref[:, j * LANES:(j + 1) * LANES]).astype(BF16)


def _attention(qkv, rope_tab, sinks, g_attn):
    bsz, seqlen, _ = qkv.shape
    nb = seqlen // BLK
    kcol = ATTN_WIDTH // KV_WIDTH
    cur = lambda b, n: (b, n, 0)
    prev = lambda b, n: (b, jnp.maximum(n - 1, 0), 0)
    return pl.pallas_call(
        _attn_kernel,
        grid=(bsz, nb),
        in_specs=[pl.BlockSpec(memory_space=pltpu.SMEM),
                  pl.BlockSpec((None, BLK, ATTN_WIDTH), cur),
                  pl.BlockSpec((None, BLK, KV_WIDTH), lambda b, n: (b, n, kcol)),
                  pl.BlockSpec((None, BLK, KV_WIDTH), lambda b, n: (b, jnp.maximum(n - 1, 0), kcol)),
                  pl.BlockSpec((None, BLK, KV_WIDTH), lambda b, n: (b, n, kcol + 1)),
                  pl.BlockSpec((None, BLK, KV_WIDTH), lambda b, n: (b, jnp.maximum(n - 1, 0), kcol + 1)),
                  pl.BlockSpec((None, BLK, 3 * LANES), cur),
                  pl.BlockSpec((None, BLK, 3 * LANES), prev),
                  pl.BlockSpec((1, ATTN_WIDTH), lambda b, n: (0, 0))],
        out_specs=pl.BlockSpec((None, BLK, ATTN_WIDTH), cur),
        out_shape=jax.ShapeDtypeStruct((bsz, seqlen, ATTN_WIDTH), BF16),
        compiler_params=_cparams(("arbitrary", "arbitrary")),
        name="swa_attention",
    )(sinks, qkv, qkv, qkv, qkv, qkv, rope_tab, rope_tab, g_attn.reshape(1, ATTN_WIDTH))


def _rope_tables(positions):
    half = ROT_DIM // 2
    inv_freq = ROPE_THETA ** (-jnp.arange(0, ROT_DIM, 2, dtype=F32) / ROT_DIM)
    ang = positions.astype(F32)[..., None] * inv_freq
    cos, sin = jnp.cos(ang), jnp.sin(ang)
    shp = cos.shape[:-1] + (HEAD_DIM - ROT_DIM,)
    c = jnp.concatenate([cos, cos, jnp.ones(shp, F32)], axis=-1)
    z8 = jnp.zeros_like(sin)
    s_lo = jnp.concatenate([-sin, z8, jnp.zeros(shp, F32)], axis=-1)
    s_hi = jnp.concatenate([z8, sin, jnp.zeros(shp, F32)], axis=-1)
    rep = LANES // HEAD_DIM
    return jnp.concatenate([jnp.tile(c, rep), jnp.tile(s_lo, rep), jnp.tile(s_hi, rep)], axis=-1)


def _s5_scan(buf, ar, ai, hr, hi, ti, store):
    def step(i, carry):
        hr, hi = carry
        r0 = pl.multiple_of(i * SUBLANES, SUBLANES)
        row = buf[pl.ds(r0, SUBLANES), :]
        nhr = ar * hr - ai * hi + row[:, :SSM_BLK_ST]
        nhi = ar * hi + ai * hr + row[:, SSM_BLK_ST:]
        if store:
            buf[pl.ds(r0, SUBLANES), :] = jnp.concatenate([nhr, nhi], axis=1)
        return nhr, nhi
    return lax.fori_loop(0, ti, step, (hr, hi), unroll=4)


def _time_major(u_ref, ti):
    u = pltpu.einshape("jid->ijd", u_ref[...].astype(F32))
    return u.reshape(ti * N_SUBSEQ, u.shape[-1])


def _s5_pass1_kernel(u_ref, bdb_ref, a_ref, f_ref, buf, hst, *, ti):
    ic = pl.program_id(2)

    @pl.when(ic == 0)
    def _():
        hst[...] = jnp.zeros_like(hst)

    u = _time_major(u_ref, ti)
    buf[...] = jnp.dot(u.astype(BF16), bdb_ref[...], preferred_element_type=F32)
    ar = jnp.broadcast_to(a_ref[0:1, :], (SUBLANES, SSM_BLK_ST))
    ai = jnp.broadcast_to(a_ref[1:2, :], (SUBLANES, SSM_BLK_ST))
    hr, hi = _s5_scan(buf, ar, ai, hst[:, :SSM_BLK_ST], hst[:, SSM_BLK_ST:], ti, store=False)
    hst[...] = jnp.concatenate([hr, hi], axis=1)

    @pl.when(ic == pl.num_programs(2) - 1)
    def _():
        f_ref[...] = hst[...]


def _s5_pass2_kernel(u_ref, f_ref, bdb_ref, a_ref, bdc_ref, glu_ref, vec_ref, o_ref, buf, hst, *, ti):
    ic = pl.program_id(2)

    @pl.when(ic == 0)
    def _():
        fr, fi = f_ref[:, :SSM_BLK_ST], f_ref[:, SSM_BLK_ST:]
        pr = jnp.broadcast_to(a_ref[2:3, :], (SUBLANES, SSM_BLK_ST))
        pi = jnp.broadcast_to(a_ref[3:4, :], (SUBLANES, SSM_BLK_ST))
        row = lax.broadcasted_iota(jnp.int32, (SUBLANES, SSM_BLK_ST), 0)
        hr = jnp.zeros((SUBLANES, SSM_BLK_ST), F32)
        hi = jnp.zeros((SUBLANES, SSM_BLK_ST), F32)
        for _ in range(N_SUBSEQ - 1):
            nr = pr * hr - pi * hi + fr
            ni = pr * hi + pi * hr + fi
            hr = jnp.where(row == 0, 0.0, pltpu.roll(nr, 1, 0))
            hi = jnp.where(row == 0, 0.0, pltpu.roll(ni, 1, 0))
        hst[...] = jnp.concatenate([hr, hi], axis=1)

    u = _time_major(u_ref, ti)
    buf[...] = jnp.dot(u.astype(BF16), bdb_ref[...], preferred_element_type=F32)
    ar = jnp.broadcast_to(a_ref[0:1, :], (SUBLANES, SSM_BLK_ST))
    ai = jnp.broadcast_to(a_ref[1:2, :], (SUBLANES, SSM_BLK_ST))
    hr, hi = _s5_scan(buf, ar, ai, hst[:, :SSM_BLK_ST], hst[:, SSM_BLK_ST:], ti, store=True)
    hst[...] = jnp.concatenate([hr, hi], axis=1)

    y = jnp.dot(buf[...].astype(BF16), bdc_ref[...], preferred_element_type=F32)
    y = jax.nn.gelu(y + vec_ref[0:1, :SSM_BLK_IN] * u)
    z = jnp.dot(y.astype(BF16), glu_ref[...], preferred_element_type=F32) + vec_ref[1:2, :]
    out = z[:, :SSM_BLK_IN] * jax.nn.sigmoid(z[:, SSM_BLK_IN:])
    out = pltpu.einshape("ijd->jid", out.reshape(ti, N_SUBSEQ, SSM_BLK_IN))
    o_ref[...] = out.astype(BF16)


def _s5_params(a_re, a_im, b_re, b_im, c_re, c_im, d_skip, log_dt, w_glu, b_glu, lsub):
    g = a_re.shape[0]
    nf = g // GROUPS_PER_BLK
    a = lax.complex(a_re.astype(F32), a_im.astype(F32))
    dt = jnp.exp(log_dt.astype(F32))[:, None]
    a_bar = jnp.exp(a * dt)
    a_pow = jnp.exp(a * dt * lsub)
    b_bar = ((a_bar - 1.0) / a)[..., None] * lax.complex(b_re.astype(F32), b_im.astype(F32))
    eye = jnp.eye(GROUPS_PER_BLK, dtype=F32)

    def bd_in(m):
        m = m.reshape(nf, GROUPS_PER_BLK, STATE, GROUP_CH)
        return jnp.einsum('fgpc,gh->fgchp', m, eye).reshape(nf, SSM_BLK_IN, SSM_BLK_ST)

    def bd_out(m):
        m = m.reshape(nf, GROUPS_PER_BLK, GROUP_CH, STATE)
        return jnp.einsum('fgcp,gh->fgphc', m, eye).reshape(nf, SSM_BLK_ST, SSM_BLK_IN)

    def bd_glu(m):
        m = m.reshape(nf, GROUPS_PER_BLK, GROUP_CH, GROUP_CH)
        return jnp.einsum('fgcd,gh->fgchd', m, eye).reshape(nf, SSM_BLK_IN, SSM_BLK_IN)

    bdb = jnp.concatenate([bd_in(jnp.real(b_bar)), bd_in(jnp.imag(b_bar))], axis=2).astype(BF16)
    bdc = jnp.concatenate([bd_out(c_re.astype(F32)), bd_out(-c_im.astype(F32))], axis=1).astype(BF16)
    wg = w_glu.astype(F32)
    glu = jnp.concatenate([bd_glu(wg[..., :GROUP_CH]), bd_glu(wg[..., GROUP_CH:])], axis=2).astype(BF16)
    flat = lambda m: m.reshape(nf, 1, SSM_BLK_ST)
    avec = jnp.concatenate([flat(jnp.real(a_bar)), flat(jnp.imag(a_bar)),
                            flat(jnp.real(a_pow)), flat(jnp.imag(a_pow))], axis=1)
    bg = b_glu.astype(F32).reshape(nf, GROUPS_PER_BLK, 2 * GROUP_CH)
    bvec = jnp.concatenate([bg[..., :GROUP_CH].reshape(nf, 1, SSM_BLK_IN),
                            bg[..., GROUP_CH:].reshape(nf, 1, SSM_BLK_IN)], axis=2)
    dvec = jnp.concatenate([d_skip.astype(F32).reshape(nf, 1, SSM_BLK_IN),
                            jnp.zeros((nf, 1, SSM_BLK_IN), F32)], axis=2)
    vec = jnp.concatenate([dvec, bvec], axis=1)
    return bdb, bdc, glu, avec, vec


def _s5(proj, params, ti=128):
    bdb, bdc, glu, avec, vec = params
    bsz, seqlen, n_in = proj.shape
    width = n_in - QKV_WIDTH
    nf = width // SSM_BLK_IN
    lsub = seqlen // N_SUBSEQ
    ti = min(ti, lsub)
    u_col0 = QKV_WIDTH // SSM_BLK_IN
    p4 = proj.reshape(bsz, N_SUBSEQ, lsub, n_in)
    grid = (bsz, nf, lsub // ti)
    u_spec = pl.BlockSpec((None, N_SUBSEQ, ti, SSM_BLK_IN), lambda b, f, i: (b, 0, i, u_col0 + f))
    o_spec = pl.BlockSpec((None, N_SUBSEQ, ti, SSM_BLK_IN), lambda b, f, i: (b, 0, i, f))
    blk = lambda r, c: pl.BlockSpec((None, r, c), lambda b, f, i: (f, 0, 0))
    f_spec = pl.BlockSpec((None, None, N_SUBSEQ, 2 * SSM_BLK_ST), lambda b, f, i: (b, f, 0, 0))
    scratch = [pltpu.VMEM((ti * SUBLANES, 2 * SSM_BLK_ST), F32),
               pltpu.VMEM((SUBLANES, 2 * SSM_BLK_ST), F32)]
    sem = ("arbitrary", "arbitrary", "arbitrary")
    fin = pl.pallas_call(
        functools.partial(_s5_pass1_kernel, ti=ti),
        grid=grid,
        in_specs=[u_spec, blk(SSM_BLK_IN, 2 * SSM_BLK_ST), blk(4, SSM_BLK_ST)],
        out_specs=f_spec,
        out_shape=jax.ShapeDtypeStruct((bsz, nf, N_SUBSEQ, 2 * SSM_BLK_ST), F32),
        scratch_shapes=scratch,
        compiler_params=_cparams(sem),
        name="s5_pass1",
    )(p4, bdb, avec)
    out = pl.pallas_call(
        functools.partial(_s5_pass2_kernel, ti=ti),
        grid=grid,
        in_specs=[u_spec, f_spec, blk(SSM_BLK_IN, 2 * SSM_BLK_ST), blk(4, SSM_BLK_ST),
                  blk(2 * SSM_BLK_ST, SSM_BLK_IN), blk(SSM_BLK_IN, 2 * SSM_BLK_IN),
                  blk(2, 2 * SSM_BLK_IN)],
        out_specs=o_spec,
        out_shape=jax.ShapeDtypeStruct((bsz, N_SUBSEQ, lsub, width), BF16),
        scratch_shapes=scratch,
        compiler_params=_cparams(sem),
        name="s5_pass2",
    )(p4, fin, bdb, avec, bdc, glu, vec)
    return out.reshape(bsz, seqlen, width)


def _layer_norm_rows(y, g, b):
    mu = jnp.mean(y, axis=-1, keepdims=True)
    yc = y - mu
    var = jnp.mean(yc * yc, axis=-1, keepdims=True)
    return yc * lax.rsqrt(var + EPS) * g + b


def _pack_bf16_pairs(lo, hi):
    lo_bits = lax.bitcast_convert_type(lo.astype(BF16).astype(F32), jnp.uint32)
    hi_bits = lax.bitcast_convert_type(hi.astype(BF16).astype(F32), jnp.uint32)
    return (lo_bits >> 16) | (hi_bits & jnp.uint32(0xFFFF0000))


def _unpack_bf16_pairs(w):
    lo = lax.bitcast_convert_type(w << 16, F32)
    hi = lax.bitcast_convert_type(w & jnp.uint32(0xFFFF0000), F32)
    return lo, hi


def _outproj_kernel(attn_ref, ssm_ref, gs_ref, w_ref, x_ref, mod_ref, ln_ref, wr_ref, br_ref,
                    x1_ref, hp_ref, idx_ref, gate_ref, rank_ref, cnt_ref, ybuf, *, nc):
    m = mod_ref[...]
    ssm = ssm_ref[...].astype(F32)
    ms = jnp.mean(ssm * ssm, axis=-1, keepdims=True)
    ssm_n = (ssm * lax.rsqrt(ms + EPS) * gs_ref[...]).astype(BF16)
    attn = attn_ref[...]
    ka = attn.shape[1]
    d = x_ref.shape[1]
    for n0 in range(0, d, nc):
        mix = (jnp.dot(attn, w_ref[:ka, n0:n0 + nc], preferred_element_type=F32)
               + jnp.dot(ssm_n, w_ref[ka:, n0:n0 + nc], preferred_element_type=F32))
        ybuf[:, n0:n0 + nc] = DN_ALPHA * x_ref[:, n0:n0 + nc] + (1.0 + m[2:3, n0:n0 + nc]) * mix
    x1 = _layer_norm_rows(ybuf[...], ln_ref[0:1, :], ln_ref[1:2, :])
    x1_ref[...] = x1
    h2 = x1 * (1.0 + m[4:5, :]) + m[3:4, :]
    hp_ref[...] = _pack_bf16_pairs(h2[:, :d // 2], h2[:, d // 2:])
    hi = h2.astype(BF16)
    lo = (h2 - hi.astype(F32)).astype(BF16)
    tm = h2.shape[0]
    r = jnp.dot(jnp.concatenate([hi, lo], axis=0), wr_ref[...], preferred_element_type=F32)
    logits = r[:tm, :N_EXPERTS] + r[:tm, N_EXPERTS:] + r[tm:, :N_EXPERTS] + br_ref[...]
    lane = lax.broadcasted_iota(jnp.int32, logits.shape, 1)
    vals, idxs = [], []
    for _ in range(TOP_K):
        mx = jnp.max(logits, axis=-1, keepdims=True)
        ix = jnp.min(jnp.where(logits == mx, lane, N_EXPERTS), axis=-1, keepdims=True)
        vals.append(mx)
        idxs.append(ix)
        logits = jnp.where(lane == ix, -jnp.inf, logits)
    tv = jnp.concatenate(vals, axis=1)
    e = jnp.exp(tv - vals[0])
    gate_ref[...] = e / jnp.sum(e, axis=-1, keepdims=True)
    idx_ref[...] = jnp.concatenate(idxs, axis=1)
    tri = (lax.broadcasted_iota(jnp.int32, (tm, tm), 0) > lax.broadcasted_iota(jnp.int32, (tm, tm), 1)).astype(BF16)
    run = jnp.zeros((1, N_EXPERTS), F32)
    ranks = []
    for ix in idxs:
        onehot = (lane == ix).astype(F32)
        before = jnp.dot(tri, onehot.astype(BF16), preferred_element_type=F32) + run
        ranks.append(jnp.sum(onehot * before, axis=-1, keepdims=True))
        run = run + jnp.sum(onehot, axis=0, keepdims=True)
    rank_ref[...] = jnp.concatenate(ranks, axis=1).astype(jnp.int32)
    cnt_ref[...] = jnp.zeros_like(cnt_ref)
    cnt_ref[0:1, 0:N_EXPERTS] = run


def _out_proj(attn_n, ssm, g_ssm, w_out_bf, x, mod3, ln1, wr, br, tm=ROUTE_TILE):
    bsz, seqlen, d = x.shape
    ka = attn_n.shape[-1]
    ks = w_out_bf.shape[0] - ka
    nt = seqlen // tm
    row = lambda b, i: (b, i, 0)
    const = lambda b, i: (0, 0)
    return pl.pallas_call(
        functools.partial(_outproj_kernel, nc=512),
        grid=(bsz, nt),
        in_specs=[pl.BlockSpec((None, tm, ka), row),
                  pl.BlockSpec((None, tm, ks), row),
                  pl.BlockSpec((1, ks), const),
                  _resident((ka + ks, d), const),
                  pl.BlockSpec((None, tm, d), row),
                  pl.BlockSpec((None, 6, d), lambda b, i: (b, 0, 0)),
                  pl.BlockSpec((2, d), const),
                  pl.BlockSpec((d, 2 * N_EXPERTS), const),
                  pl.BlockSpec((1, N_EXPERTS), const)],
        out_specs=[pl.BlockSpec((None, tm, d), row),
                   pl.BlockSpec((None, tm, d // 2), row),
                   pl.BlockSpec((None, tm, TOP_K), row),
                   pl.BlockSpec((None, tm, TOP_K), row),
                   pl.BlockSpec((None, tm, TOP_K), row),
                   pl.BlockSpec((None, SUBLANES, LANES), lambda b, i: (b * nt + i, 0, 0))],
        out_shape=[jax.ShapeDtypeStruct((bsz, seqlen, d), F32),
                   jax.ShapeDtypeStruct((bsz, seqlen, d // 2), jnp.uint32),
                   jax.ShapeDtypeStruct((bsz, seqlen, TOP_K), jnp.int32),
                   jax.ShapeDtypeStruct((bsz, seqlen, TOP_K), F32),
                   jax.ShapeDtypeStruct((bsz, seqlen, TOP_K), jnp.int32),
                   jax.ShapeDtypeStruct((bsz * nt, SUBLANES, LANES), F32)],
        scratch_shapes=[pltpu.VMEM((tm, d), F32)],
        compiler_params=_cparams(("arbitrary", "arbitrary")),
        name="out_proj_ln_router",
    )(attn_n, ssm, g_ssm.reshape(1, ks), w_out_bf, x, mod3, ln1, wr, br)


def _stream_expert_weights(blk_e, first, nxt, last, w_hbm, stage, wbf, sem):
    ct = pl.program_id(0)
    rb = pl.program_id(1)
    width = wbf[0].shape[1]

    def copies(e, col_tile):
        c0 = pl.multiple_of(col_tile * width, width)
        return [pltpu.make_async_copy(w.at[e, :, pl.ds(c0, width)], s, sem.at[j])
                for j, (w, s) in enumerate(zip(w_hbm, stage))]

    @pl.when((ct == 0) & (rb == 0))
    def _():
        for cp in copies(blk_e[0], 0):
            cp.start()

    @pl.when(first[rb] == 1)
    def _():
        for cp in copies(blk_e[rb], ct):
            cp.wait()
        for s, w in zip(stage, wbf):
            w[...] = s[...].astype(BF16)

        @pl.when(last[rb] == 0)
        def _():
            for cp in copies(nxt[rb], ct):
                cp.start()

        @pl.when((last[rb] == 1) & (ct + 1 < pl.num_programs(0)))
        def _():
            for cp in copies(nxt[rb], ct + 1):
                cp.start()


def _expert_up_kernel(blk_e, first, nreal, nxt, last, xs_ref, wg_hbm, wu_hbm, bg_ref, bu_ref, act_ref,
                      stg_g, stg_u, wgb, wub, sem):
    rb = pl.program_id(1)
    _stream_expert_weights(blk_e, first, nxt, last, (wg_hbm, wu_hbm), (stg_g, stg_u), (wgb, wub), sem)

    @pl.when(rb < nreal[0])
    def _():
        x = jnp.concatenate(_unpack_bf16_pairs(xs_ref[...]), axis=1).astype(BF16)
        g = jnp.dot(x, wgb[...], preferred_element_type=F32) + bg_ref[...]
        up = jnp.dot(x, wub[...], preferred_element_type=F32) + bu_ref[...]
        g = jnp.minimum(g, SWIGLU_LIMIT)
        up = jnp.clip(up, -SWIGLU_LIMIT, SWIGLU_LIMIT)
        act_ref[...] = (g * jax.nn.sigmoid(SWIGLU_ALPHA * g) * (up + 1.0)).astype(BF16)

    @pl.when(rb >= nreal[0])
    def _():
        act_ref[...] = jnp.zeros_like(act_ref)


def _expert_down_kernel(blk_e, first, nreal, nxt, last, act_ref, wd_hbm, bd_ref, y_ref, stg, wdb, sem):
    rb = pl.program_id(1)
    _stream_expert_weights(blk_e, first, nxt, last, (wd_hbm,), (stg,), (wdb,), sem)

    @pl.when(rb < nreal[0])
    def _():
        y = jnp.dot(act_ref[...], wdb[...], preferred_element_type=F32) + bd_ref[...]
        half = y.shape[1] // 2
        y_ref[...] = _pack_bf16_pairs(y[:, :half], y[:, half:])

    @pl.when(rb >= nreal[0])
    def _():
        y_ref[...] = jnp.zeros_like(y_ref)


def _experts(xs, sched, w_gate, b_gate, w_up, b_up, w_down, b_down, tf=512, tn=DOWN_TILE):
    cap = xs.shape[0]
    n_e, d, dff = w_gate.shape
    nblk = cap // EXPERT_BLK
    hbm = pl.BlockSpec(memory_space=pl.ANY)
    act = pl.pallas_call(
        _expert_up_kernel,
        grid_spec=pltpu.PrefetchScalarGridSpec(
            num_scalar_prefetch=5,
            grid=(dff // tf, nblk),
            in_specs=[pl.BlockSpec((EXPERT_BLK, d // 2), lambda f, r, be, *_: (r, 0)),
                      hbm, hbm,
                      pl.BlockSpec((None, 1, tf), lambda f, r, be, *_: (be[r], 0, f)),
                      pl.BlockSpec((None, 1, tf), lambda f, r, be, *_: (be[r], 0, f))],
            out_specs=pl.BlockSpec((EXPERT_BLK, tf), lambda f, r, be, *_: (r, f)),
            scratch_shapes=[pltpu.VMEM((d, tf), F32), pltpu.VMEM((d, tf), F32),
                            pltpu.VMEM((d, tf), BF16), pltpu.VMEM((d, tf), BF16),
                            pltpu.SemaphoreType.DMA((2,))]),
        out_shape=jax.ShapeDtypeStruct((cap, dff), BF16),
        compiler_params=_cparams(("arbitrary", "arbitrary")),
        name="expert_gate_up",
    )(*sched, xs, w_gate, w_up, b_gate.reshape(n_e, 1, dff), b_up.reshape(n_e, 1, dff))
    ys = pl.pallas_call(
        _expert_down_kernel,
        grid_spec=pltpu.PrefetchScalarGridSpec(
            num_scalar_prefetch=5,
            grid=(d // tn, nblk),
            in_specs=[pl.BlockSpec((EXPERT_BLK, dff), lambda n, r, be, *_: (r, 0)),
                      hbm,
                      pl.BlockSpec((None, 1, tn), lambda n, r, be, *_: (be[r], 0, n))],
            out_specs=pl.BlockSpec((EXPERT_BLK, tn // 2), lambda n, r, be, *_: (r, n)),
            scratch_shapes=[pltpu.VMEM((dff, tn), F32), pltpu.VMEM((dff, tn), BF16),
                            pltpu.SemaphoreType.DMA((1,))]),
        out_shape=jax.ShapeDtypeStruct((cap, d // 2), jnp.uint32),
        compiler_params=_cparams(("arbitrary", "arbitrary")),
        name="expert_down",
    )(*sched, act, w_down, b_down.reshape(n_e, 1, d))
    return ys


def _route_tables(cnt, top_idx, rank):
    n_assign = top_idx.size
    counts = cnt[:, 0, :N_EXPERTS].astype(jnp.int32)
    tot = jnp.sum(counts, axis=0)
    padded = (tot + EXPERT_BLK - 1) // EXPERT_BLK * EXPERT_BLK
    pend = jnp.cumsum(padded)
    pstart = pend - padded
    base = pstart[None, :] + jnp.cumsum(counts, axis=0) - counts
    experts = jnp.arange(N_EXPERTS, dtype=jnp.int32)
    idx_t = top_idx.reshape(-1, ROUTE_TILE * TOP_K)
    dest = jnp.sum(jnp.where(idx_t[..., None] == experts, base[:, None, :], 0), axis=-1)
    dest = (dest.reshape(-1) + rank.reshape(-1)).astype(jnp.int32)
    cap = ((n_assign + EXPERT_BLK - 1) // EXPERT_BLK) * EXPERT_BLK + N_EXPERTS * EXPERT_BLK
    nblk = cap // EXPERT_BLK
    blk = jnp.arange(nblk, dtype=jnp.int32)
    blk_e = jnp.sum((pend[None, :] <= blk[:, None] * EXPERT_BLK).astype(jnp.int32), axis=1)
    blk_e = jnp.minimum(blk_e, N_EXPERTS - 1).astype(jnp.int32)
    nreal = (pend[-1:] // EXPERT_BLK).astype(jnp.int32)
    real = blk < nreal[0]
    first = real & jnp.concatenate([jnp.ones((1,), bool), blk_e[1:] != blk_e[:-1]])
    starts = jnp.where(first, blk, nblk)
    nxt_blk = jnp.concatenate([lax.cummin(starts, reverse=True)[1:], jnp.full((1,), nblk, jnp.int32)])
    last = nxt_blk >= nblk
    nxt = jnp.where(last, blk_e[0], blk_e[jnp.minimum(nxt_blk, nblk - 1)]).astype(jnp.int32)
    sched = (blk_e, first.astype(jnp.int32), nreal, nxt, last.astype(jnp.int32))
    return dest, cap, sched, (pstart + tot).astype(jnp.int32), (padded - tot).astype(jnp.int32)


def _row_copy(src, src_row, dst, dst_row, sem):
    return pltpu.make_async_copy(src.at[pl.ds(src_row, 1)], dst.at[pl.ds(dst_row, 1)], sem)


def _dispatch_kernel(pad0_ref, padn_ref, nreal_ref, hp_ref, dest_ref, xs_ref, zblk, sem, zsem, bsem):
    i = pl.program_id(0)
    tm = hp_ref.shape[0]
    nblk = xs_ref.shape[0] // EXPERT_BLK

    def for_each_pad_row(fn):
        def per_expert(e, c):
            def per_row(r, c2):
                fn(_row_copy(zblk, 0, xs_ref, pad0_ref[e] + r, zsem))
                return c2
            return lax.fori_loop(0, padn_ref[e], per_row, c)
        lax.fori_loop(0, N_EXPERTS, per_expert, 0)

    def for_each_unused_block(fn):
        def per_block(b, c):
            r0 = pl.multiple_of(b * EXPERT_BLK, EXPERT_BLK)
            fn(pltpu.make_async_copy(zblk, xs_ref.at[pl.ds(r0, EXPERT_BLK)], bsem))
            return c
        lax.fori_loop(nreal_ref[0], nblk, per_block, 0)

    @pl.when(i == 0)
    def _():
        zblk[...] = jnp.zeros_like(zblk)
        for_each_pad_row(lambda cp: cp.start())
        for_each_unused_block(lambda cp: cp.start())

    def issue(r, c):
        for k in range(TOP_K):
            _row_copy(hp_ref, r, xs_ref, dest_ref[r * TOP_K + k], sem).start()
        return c
    lax.fori_loop(0, tm, issue, 0, unroll=2)

    def drain(r, c):
        for k in range(TOP_K):
            _row_copy(hp_ref, 0, xs_ref, 0, sem).wait()
        return c
    lax.fori_loop(0, tm, drain, 0)

    @pl.when(i == 0)
    def _():
        for_each_pad_row(lambda cp: cp.wait())
        for_each_unused_block(lambda cp: cp.wait())


def _dispatch(hp, dest, pad0, padn, nreal, cap, tm=MOVE_TILE):
    n_tok, half = hp.shape
    return pl.pallas_call(
        _dispatch_kernel,
        grid_spec=pltpu.PrefetchScalarGridSpec(
            num_scalar_prefetch=3,
            grid=(n_tok // tm,),
            in_specs=[pl.BlockSpec((tm, half), lambda i, *_: (i, 0)),
                      pl.BlockSpec((tm * TOP_K,), lambda i, *_: (i,), memory_space=pltpu.SMEM)],
            out_specs=pl.BlockSpec(memory_space=pl.ANY),
            scratch_shapes=[pltpu.VMEM((EXPERT_BLK, half), jnp.uint32),
                            pltpu.SemaphoreType.DMA, pltpu.SemaphoreType.DMA, pltpu.SemaphoreType.DMA]),
        out_shape=jax.ShapeDtypeStruct((cap, half), jnp.uint32),
        compiler_params=_cparams(("arbitrary",)),
        name="moe_dispatch",
    )(pad0, padn, nreal, hp, dest)


def _combine_kernel(dest_ref, gate_ref, x1_ref, mod_ref, ln_ref, ys_ref, o_ref, buf, sem, *, tn):
    tm = x1_ref.shape[0]

    def issue(r, c):
        for k in range(TOP_K):
            pltpu.make_async_copy(ys_ref.at[pl.ds(dest_ref[r * TOP_K + k], 1)],
                                  buf.at[k, pl.ds(r, 1)], sem).start()
        return c
    lax.fori_loop(0, tm, issue, 0, unroll=2)

    def drain(r, c):
        for k in range(TOP_K):
            pltpu.make_async_copy(ys_ref.at[pl.ds(0, 1)], buf.at[0, pl.ds(0, 1)], sem).wait()
        return c
    lax.fori_loop(0, tm, drain, 0)

    g = gate_ref[...]
    m = mod_ref[...]
    d = x1_ref.shape[1]
    hw = tn // 2
    cw = 512
    for w0 in range(0, d // 2, cw):
        lo = hi = None
        for k in range(TOP_K):
            wl, wh = _unpack_bf16_pairs(buf[k, :, w0:w0 + cw])
            lo = g[:, k:k + 1] * wl if lo is None else lo + g[:, k:k + 1] * wl
            hi = g[:, k:k + 1] * wh if hi is None else hi + g[:, k:k + 1] * wh
        for half, moe in ((0, lo), (1, hi)):
            c0 = (w0 // hw) * tn + half * hw + w0 % hw
            o_ref[:, c0:c0 + cw] = DN_ALPHA * x1_ref[:, c0:c0 + cw] + (1.0 + m[5:6, c0:c0 + cw]) * moe
    o_ref[...] = _layer_norm_rows(o_ref[...], ln_ref[0:1, :], ln_ref[1:2, :])


def _combine(ys, dest, gates, x1, mod3, ln2, tm=MOVE_TILE, tn=DOWN_TILE):
    bsz, seqlen, d = x1.shape
    n_tok = bsz * seqlen
    per_b = seqlen // tm
    out = pl.pallas_call(
        functools.partial(_combine_kernel, tn=tn),
        grid=(n_tok // tm,),
        in_specs=[pl.BlockSpec((tm * TOP_K,), lambda i: (i,), memory_space=pltpu.SMEM),
                  pl.BlockSpec((tm, TOP_K), lambda i: (i, 0)),
                  pl.BlockSpec((tm, d), lambda i: (i, 0)),
                  pl.BlockSpec((None, 6, d), lambda i: (i // per_b, 0, 0)),
                  pl.BlockSpec((2, d), lambda i: (0, 0)),
                  pl.BlockSpec(memory_space=pl.ANY)],
        out_specs=pl.BlockSpec((tm, d), lambda i: (i, 0)),
        out_shape=jax.ShapeDtypeStruct((n_tok, d), F32),
        scratch_shapes=[pltpu.VMEM((TOP_K, tm, d // 2), jnp.uint32), pltpu.SemaphoreType.DMA],
        compiler_params=_cparams(("arbitrary",)),
        name="moe_combine_ln",
    )(dest, gates.reshape(n_tok, TOP_K), x1.reshape(n_tok, d), mod3, ln2, ys)
    return out.reshape(bsz, seqlen, d)


def kernel(x, c, positions, w_ada, b_ada, w_in, attn_sinks, ssm_a_re, ssm_a_im, ssm_b_re, ssm_b_im,
           ssm_c_re, ssm_c_im, ssm_d, ssm_log_dt, ssm_w_glu, ssm_b_glu, g_attn_out, g_ssm_out, w_out,
           ln1_g, ln1_b, w_router, b_router, w_gate, b_gate, w_up, b_up, w_down, b_down, ln2_g, ln2_b):
    bsz, seqlen, d = x.shape
    lsub = seqlen // N_SUBSEQ
    n_tok = bsz * seqlen
    rope_tab = _rope_tables(positions)
    for l in range(w_ada.shape[0]):
        mod3 = _ada_mod(c, w_ada[l], b_ada[l]).reshape(bsz, 6, d)
        proj = _in_proj(x, mod3, w_in[l].astype(BF16))
        attn_n = _attention(proj, rope_tab, attn_sinks[l].astype(F32), g_attn_out[l].astype(F32))
        s5p = _s5_params(ssm_a_re[l], ssm_a_im[l], ssm_b_re[l], ssm_b_im[l], ssm_c_re[l], ssm_c_im[l],
                         ssm_d[l], ssm_log_dt[l], ssm_w_glu[l], ssm_b_glu[l], lsub)
        ssm = _s5(proj, s5p)
        wr_hi = w_router[l].astype(BF16)
        wr_lo = (w_router[l] - wr_hi.astype(F32)).astype(BF16)
        x1, hp, top_idx, gates, rank, cnt = _out_proj(
            attn_n, ssm, g_ssm_out[l].astype(F32), w_out[l].astype(BF16), x, mod3,
            jnp.stack([ln1_g[l], ln1_b[l]]).astype(F32),
            jnp.concatenate([wr_hi, wr_lo], axis=1), b_router[l].reshape(1, N_EXPERTS).astype(F32))
        dest, cap, sched, pad0, padn = _route_tables(cnt, top_idx, rank)
        xs = _dispatch(hp.reshape(n_tok, d // 2), dest, pad0, padn, sched[2], cap)
        ys = _experts(xs, sched, w_gate[l], b_gate[l], w_up[l], b_up[l], w_down[l], b_down[l])
        x = _combine(ys, dest, gates, x1, mod3, jnp.stack([ln2_g[l], ln2_b[l]]).astype(F32))
    return x
```

```python
import functools
import math

import jax
import jax.numpy as jnp
from jax import lax
from jax.experimental import pallas as pl
from jax.experimental.pallas import tpu as pltpu

F32 = jnp.float32
BF16 = jnp.bfloat16

HEAD_DIM = 64
N_Q_HEADS = 32
N_KV_HEADS = 4
GQ = N_Q_HEADS // N_KV_HEADS
ATTN_WIDTH = N_Q_HEADS * HEAD_DIM
KV_WIDTH = N_KV_HEADS * HEAD_DIM
QKV_WIDTH = ATTN_WIDTH + 2 * KV_WIDTH
BLK = 128
ROT_DIM = HEAD_DIM // 4
ROPE_THETA = 500000.0
GROUP_CH = 16
STATE = 64
N_EXPERTS = 32
TOP_K = 4
SWIGLU_LIMIT = 7.0
SWIGLU_ALPHA = 1.702
EXPERT_BLK = 256
ROUTE_TILE = 128
MOVE_TILE = 256
DOWN_TILE = 4096
CAST_ROWS = 128
DEPTH = 1
DN_ALPHA = (2.0 * DEPTH) ** 0.25
EPS = 1e-5

LANES = 128
SUBLANES = 8
N_SUBSEQ = SUBLANES
GROUPS_PER_BLK = 16
SSM_BLK_IN = GROUPS_PER_BLK * GROUP_CH
SSM_BLK_ST = GROUPS_PER_BLK * STATE
VMEM_LIMIT = 56 * 1024 * 1024


def _cparams(sem, vmem=VMEM_LIMIT):
    return pltpu.CompilerParams(dimension_semantics=sem, vmem_limit_bytes=vmem)


def _resident(shape, index_map):
    return pl.BlockSpec(shape, index_map, pipeline_mode=pl.Buffered(1))


def _ada_kernel(c_ref, w_ref, b_ref, o_ref):
    c = c_ref[...]
    ca = c * jax.nn.sigmoid(c)
    o_ref[...] = jnp.dot(ca.astype(BF16), w_ref[...].astype(BF16),
                         preferred_element_type=F32) + b_ref[...]


def _ada_mod(c, w_ada, b_ada, tn=512):
    bsz, d = c.shape
    n = w_ada.shape[1]
    c8 = jnp.zeros((SUBLANES, d), F32).at[:bsz].set(c)
    out = pl.pallas_call(
        _ada_kernel,
        grid=(n // tn,),
        in_specs=[pl.BlockSpec((SUBLANES, d), lambda j: (0, 0)),
                  pl.BlockSpec((d, tn), lambda j: (0, j)),
                  pl.BlockSpec((1, tn), lambda j: (0, j))],
        out_specs=pl.BlockSpec((SUBLANES, tn), lambda j: (0, j)),
        out_shape=jax.ShapeDtypeStruct((SUBLANES, n), F32),
        compiler_params=_cparams(("arbitrary",)),
        name="ada_mod",
    )(c8, w_ada, b_ada.reshape(1, n))
    return out[:bsz]


def _inproj_kernel(x_ref, mod_ref, w_ref, o_ref, *, nc):
    m = mod_ref[...]
    h = (x_ref[...] * (1.0 + m[1:2, :]) + m[0:1, :]).astype(BF16)
    for n0 in range(0, o_ref.shape[-1], nc):
        o_ref[:, n0:n0 + nc] = jnp.dot(
            h, w_ref[:, n0:n0 + nc], preferred_element_type=F32).astype(BF16)


def _in_proj(x, mod3, w_in_bf, tm=128):
    bsz, seqlen, d = x.shape
    n_in = w_in_bf.shape[1]
    return pl.pallas_call(
        functools.partial(_inproj_kernel, nc=512),
        grid=(bsz, seqlen // tm),
        in_specs=[pl.BlockSpec((None, tm, d), lambda b, i: (b, i, 0)),
                  pl.BlockSpec((None, 6, d), lambda b, i: (b, 0, 0)),
                  _resident((d, n_in), lambda b, i: (0, 0))],
        out_specs=pl.BlockSpec((None, tm, n_in), lambda b, i: (b, i, 0)),
        out_shape=jax.ShapeDtypeStruct((bsz, seqlen, n_in), BF16),
        compiler_params=_cparams(("arbitrary", "arbitrary")),
        name="in_proj",
    )(x, mod3, w_in_bf)


def _rope(t, tab):
    c, s_lo, s_hi = tab[:, :LANES], tab[:, LANES:2 * LANES], tab[:, 2 * LANES:]
    half = ROT_DIM // 2
    out = []
    for j in range(t.shape[1] // LANES):
        tj = t[:, j * LANES:(j + 1) * LANES]
        out.append(tj * c + pltpu.roll(tj, LANES - half, 1) * s_lo + pltpu.roll(tj, half, 1) * s_hi)
    return out


def _attn_kernel(sink_ref, q_ref, kc_ref, kp_ref, vc_ref, vp_ref, tc_ref, tp_ref, g_ref, o_ref):
    n = pl.program_id(1)
    low = lax.broadcasted_iota(jnp.int32, (2 * BLK, LANES), 1) < HEAD_DIM
    low_q = lax.broadcasted_iota(jnp.int32, (BLK, LANES), 1) < HEAD_DIM

    q_chunks = _rope(q_ref[...].astype(F32) * (HEAD_DIM ** -0.5), tc_ref[...])
    k_raw = jnp.concatenate([kp_ref[...], kc_ref[...]], axis=0).astype(F32)
    k_chunks = _rope(k_raw, jnp.concatenate([tp_ref[...], tc_ref[...]], axis=0))
    v_raw = jnp.concatenate([vp_ref[...], vc_ref[...]], axis=0).astype(F32)

    qi = lax.broadcasted_iota(jnp.int32, (BLK, BLK), 0)
    kj = lax.broadcasted_iota(jnp.int32, (BLK, BLK), 1)
    own = kj <= qi
    prev_ok = kj >= jnp.where(n > 0, 0, BLK)

    o_chunks = []
    for hk in range(N_KV_HEADS):
        kc = k_chunks[hk // 2]
        vc = v_raw[:, (hk // 2) * LANES:(hk // 2 + 1) * LANES]
        k_sw = pltpu.roll(kc, HEAD_DIM, 1)
        v_sw = pltpu.roll(vc, HEAD_DIM, 1)
        if hk % 2 == 0:
            kk2 = jnp.where(low, kc, k_sw)
            v_lo = jnp.where(low, vc, 0.0)
            v_hi = jnp.where(low, 0.0, v_sw)
        else:
            kk2 = jnp.where(low, k_sw, kc)
            v_lo = jnp.where(low, v_sw, 0.0)
            v_hi = jnp.where(low, 0.0, vc)
        kk2 = kk2.astype(BF16)
        v_lo = v_lo.astype(BF16)
        v_hi = v_hi.astype(BF16)
        lhs = []
        for j in range(GQ // 2):
            q2 = q_chunks[hk * (GQ // 2) + j]
            lhs.append(jnp.where(low_q, q2, 0.0).astype(BF16))
            lhs.append(jnp.where(low_q, 0.0, q2).astype(BF16))
        s_all = lax.dot_general(jnp.concatenate(lhs, axis=0), kk2,
                                (((1,), (1,)), ((), ())), preferred_element_type=F32)
        for j in range(GQ // 2):
            acc = None
            for side, vv in ((0, v_lo), (1, v_hi)):
                i = 2 * j + side
                s_prev = jnp.where(prev_ok, s_all[i * BLK:(i + 1) * BLK, :BLK], -1e30)
                s = jnp.where(own, s_all[i * BLK:(i + 1) * BLK, BLK:], s_prev)
                sink = sink_ref[hk * GQ + i]
                m = jnp.maximum(jnp.max(s, axis=-1, keepdims=True), sink)
                p = jnp.exp(s - m)
                denom = jnp.sum(p, axis=-1, keepdims=True) + jnp.exp(sink - m)
                p = p * (1.0 / denom)
                p = jnp.concatenate([jnp.where(own, 0.0, p), jnp.where(own, p, 0.0)], axis=1).astype(BF16)
                o = jnp.dot(p, vv, preferred_element_type=F32)
                acc = o if acc is None else acc + o
            o_chunks.append(acc)

    ssq = None
    for oc in o_chunks:
        t = jnp.sum(oc * oc, axis=-1, keepdims=True)
        ssq = t if ssq is None else ssq + t
    inv = lax.rsqrt(ssq * (1.0 / ATTN_WIDTH) + EPS)
    for j, oc in enumerate(o_chunks):
        o_ref[:, j * LANES:(j + 1) * LANES] = (oc * inv * g_ref[:, j * LANES:(j + 1) * LANES]).astype(BF16)


def _attention(qkv, rope_tab, sinks, g_attn):
    bsz, seqlen, _ = qkv.shape
    nb = seqlen // BLK
    kcol = ATTN_WIDTH // KV_WIDTH
    cur = lambda b, n: (b, n, 0)
    prev = lambda b, n: (b, jnp.maximum(n - 1, 0), 0)
    return pl.pallas_call(
        _attn_kernel,
        grid=(bsz, nb),
        in_specs=[pl.BlockSpec(memory_space=pltpu.SMEM),
                  pl.BlockSpec((None, BLK, ATTN_WIDTH), cur),
                  pl.BlockSpec((None, BLK, KV_WIDTH), lambda b, n: (b, n, kcol)),
                  pl.BlockSpec((None, BLK, KV_WIDTH), lambda b, n: (b, jnp.maximum(n - 1, 0), kcol)),
                  pl.BlockSpec((None, BLK, KV_WIDTH), lambda b, n: (b, n, kcol + 1)),
                  pl.BlockSpec((None, BLK, KV_WIDTH), lambda b, n: (b, jnp.maximum(n - 1, 0), kcol + 1)),
                  pl.BlockSpec((None, BLK, 3 * LANES), cur),
                  pl.BlockSpec((None, BLK, 3 * LANES), prev),
                  pl.BlockSpec((1, ATTN_WIDTH), lambda b, n: (0, 0))],
        out_specs=pl.BlockSpec((None, BLK, ATTN_WIDTH), cur),
        out_shape=jax.ShapeDtypeStruct((bsz, seqlen, ATTN_WIDTH), BF16),
        compiler_params=_cparams(("arbitrary", "arbitrary")),
        name="swa_attention",
    )(sinks, qkv, qkv, qkv, qkv, qkv, rope_tab, rope_tab, g_attn.reshape(1, ATTN_WIDTH))


def _rope_tables(positions):
    half = ROT_DIM // 2
    inv_freq = ROPE_THETA ** (-jnp.arange(0, ROT_DIM, 2, dtype=F32) / ROT_DIM)
    ang = positions.astype(F32)[..., None] * inv_freq
    cos, sin = jnp.cos(ang), jnp.sin(ang)
    shp = cos.shape[:-1] + (HEAD_DIM - ROT_DIM,)
    c = jnp.concatenate([cos, cos, jnp.ones(shp, F32)], axis=-1)
    z8 = jnp.zeros_like(sin)
    s_lo = jnp.concatenate([-sin, z8, jnp.zeros(shp, F32)], axis=-1)
    s_hi = jnp.concatenate([z8, sin, jnp.zeros(shp, F32)], axis=-1)
    rep = LANES // HEAD_DIM
    return jnp.concatenate([jnp.tile(c, rep), jnp.tile(s_lo, rep), jnp.tile(s_hi, rep)], axis=-1)


def _s5_scan(buf, ar, ai, hr, hi, ti, store):
    def step(i, carry):
        hr, hi = carry
        r0 = pl.multiple_of(i * SUBLANES, SUBLANES)
        row = buf[pl.ds(r0, SUBLANES), :]
        nhr = ar * hr - ai * hi + row[:, :SSM_BLK_ST]
        nhi = ar * hi + ai * hr + row[:, SSM_BLK_ST:]
        if store:
            buf[pl.ds(r0, SUBLANES), :] = jnp.concatenate([nhr, nhi], axis=1)
        return nhr, nhi
    return lax.fori_loop(0, ti, step, (hr, hi), unroll=4)


def _time_major(u_ref, ti):
    u = pltpu.einshape("jid->ijd", u_ref[...].astype(F32))
    return u.reshape(ti * N_SUBSEQ, u.shape[-1])


def _s5_pass1_kernel(u_ref, bdb_ref, a_ref, f_ref, buf, hst, *, ti):
    ic = pl.program_id(2)

    @pl.when(ic == 0)
    def _():
        hst[...] = jnp.zeros_like(hst)

    u = _time_major(u_ref, ti)
    buf[...] = jnp.dot(u.astype(BF16), bdb_ref[...], preferred_element_type=F32)
    ar = jnp.broadcast_to(a_ref[0:1, :], (SUBLANES, SSM_BLK_ST))
    ai = jnp.broadcast_to(a_ref[1:2, :], (SUBLANES, SSM_BLK_ST))
    hr, hi = _s5_scan(buf, ar, ai, hst[:, :SSM_BLK_ST], hst[:, SSM_BLK_ST:], ti, store=False)
    hst[...] = jnp.concatenate([hr, hi], axis=1)

    @pl.when(ic == pl.num_programs(2) - 1)
    def _():
        f_ref[...] = hst[...]


def _s5_pass2_kernel(u_ref, f_ref, bdb_ref, a_ref, bdc_ref, glu_ref, vec_ref, o_ref, buf, hst, *, ti):
    ic = pl.program_id(2)

    @pl.when(ic == 0)
    def _():
        fr, fi = f_ref[:, :SSM_BLK_ST], f_ref[:, SSM_BLK_ST:]
        pr = jnp.broadcast_to(a_ref[2:3, :], (SUBLANES, SSM_BLK_ST))
        pi = jnp.broadcast_to(a_ref[3:4, :], (SUBLANES, SSM_BLK_ST))
        row = lax.broadcasted_iota(jnp.int32, (SUBLANES, SSM_BLK_ST), 0)
        hr = jnp.zeros((SUBLANES, SSM_BLK_ST), F32)
        hi = jnp.zeros((SUBLANES, SSM_BLK_ST), F32)
        for _ in range(N_SUBSEQ - 1):
            nr = pr * hr - pi * hi + fr
            ni = pr * hi + pi * hr + fi
            hr = jnp.where(row == 0, 0.0, pltpu.roll(nr, 1, 0))
            hi = jnp.where(row == 0, 0.0, pltpu.roll(ni, 1, 0))
        hst[...] = jnp.concatenate([hr, hi], axis=1)

    u = _time_major(u_ref, ti)
    buf[...] = jnp.dot(u.astype(BF16), bdb_ref[...], preferred_element_type=F32)
    ar = jnp.broadcast_to(a_ref[0:1, :], (SUBLANES, SSM_BLK_ST))
    ai = jnp.broadcast_to(a_ref[1:2, :], (SUBLANES, SSM_BLK_ST))
    hr, hi = _s5_scan(buf, ar, ai, hst[:, :SSM_BLK_ST], hst[:, SSM_BLK_ST:], ti, store=True)
    hst[...] = jnp.concatenate([hr, hi], axis=1)

    y = jnp.dot(buf[...].astype(BF16), bdc_ref[...], preferred_element_type=F32)
    y = jax.nn.gelu(y + vec_ref[0:1, :SSM_BLK_IN] * u)
    z = jnp.dot(y.astype(BF16), glu_ref[...], preferred_element_type=F32) + vec_ref[1:2, :]
    out = z[:, :SSM_BLK_IN] * jax.nn.sigmoid(z[:, SSM_BLK_IN:])
    out = pltpu.einshape("ijd->jid", out.reshape(ti, N_SUBSEQ, SSM_BLK_IN))
    o_ref[...] = out.astype(BF16)


def _s5_params(a_re, a_im, b_re, b_im, c_re, c_im, d_skip, log_dt, w_glu, b_glu, lsub):
    g = a_re.shape[0]
    nf = g // GROUPS_PER_BLK
    a = lax.complex(a_re.astype(F32), a_im.astype(F32))
    dt = jnp.exp(log_dt.astype(F32))[:, None]
    a_bar = jnp.exp(a * dt)
    a_pow = jnp.exp(a * dt * lsub)
    b_bar = ((a_bar - 1.0) / a)[..., None] * lax.complex(b_re.astype(F32), b_im.astype(F32))
    eye = jnp.eye(GROUPS_PER_BLK, dtype=F32)

    def bd_in(m):
        m = m.reshape(nf, GROUPS_PER_BLK, STATE, GROUP_CH)
        return jnp.einsum('fgpc,gh->fgchp', m, eye).reshape(nf, SSM_BLK_IN, SSM_BLK_ST)

    def bd_out(m):
        m = m.reshape(nf, GROUPS_PER_BLK, GROUP_CH, STATE)
        return jnp.einsum('fgcp,gh->fgphc', m, eye).reshape(nf, SSM_BLK_ST, SSM_BLK_IN)

    def bd_glu(m):
        m = m.reshape(nf, GROUPS_PER_BLK, GROUP_CH, GROUP_CH)
        return jnp.einsum('fgcd,gh->fgchd', m, eye).reshape(nf, SSM_BLK_IN, SSM_BLK_IN)

    bdb = jnp.concatenate([bd_in(jnp.real(b_bar)), bd_in(jnp.imag(b_bar))], axis=2).astype(BF16)
    bdc = jnp.concatenate([bd_out(c_re.astype(F32)), bd_out(-c_im.astype(F32))], axis=1).astype(BF16)
    wg = w_glu.astype(F32)
    glu = jnp.concatenate([bd_glu(wg[..., :GROUP_CH]), bd_glu(wg[..., GROUP_CH:])], axis=2).astype(BF16)
    flat = lambda m: m.reshape(nf, 1, SSM_BLK_ST)
    avec = jnp.concatenate([flat(jnp.real(a_bar)), flat(jnp.imag(a_bar)),
                            flat(jnp.real(a_pow)), flat(jnp.imag(a_pow))], axis=1)
    bg = b_glu.astype(F32).reshape(nf, GROUPS_PER_BLK, 2 * GROUP_CH)
    bvec = jnp.concatenate([bg[..., :GROUP_CH].reshape(nf, 1, SSM_BLK_IN),
                            bg[..., GROUP_CH:].reshape(nf, 1, SSM_BLK_IN)], axis=2)
    dvec = jnp.concatenate([d_skip.astype(F32).reshape(nf, 1, SSM_BLK_IN),
                            jnp.zeros((nf, 1, SSM_BLK_IN), F32)], axis=2)
    vec = jnp.concatenate([dvec, bvec], axis=1)
    return bdb, bdc, glu, avec, vec


def _s5(proj, params, ti=128):
    bdb, bdc, glu, avec, vec = params
    bsz, seqlen, n_in = proj.shape
    width = n_in - QKV_WIDTH
    nf = width // SSM_BLK_IN
    lsub = seqlen // N_SUBSEQ
    ti = min(ti, lsub)
    u_col0 = QKV_WIDTH // SSM_BLK_IN
    p4 = proj.reshape(bsz, N_SUBSEQ, lsub, n_in)
    grid = (bsz, nf, lsub // ti)
    u_spec = pl.BlockSpec((None, N_SUBSEQ, ti, SSM_BLK_IN), lambda b, f, i: (b, 0, i, u_col0 + f))
    o_spec = pl.BlockSpec((None, N_SUBSEQ, ti, SSM_BLK_IN), lambda b, f, i: (b, 0, i, f))
    blk = lambda r, c: pl.BlockSpec((None, r, c), lambda b, f, i: (f, 0, 0))
    f_spec = pl.BlockSpec((None, None, N_SUBSEQ, 2 * SSM_BLK_ST), lambda b, f, i: (b, f, 0, 0))
    scratch = [pltpu.VMEM((ti * SUBLANES, 2 * SSM_BLK_ST), F32),
               pltpu.VMEM((SUBLANES, 2 * SSM_BLK_ST), F32)]
    sem = ("arbitrary", "arbitrary", "arbitrary")
    fin = pl.pallas_call(
        functools.partial(_s5_pass1_kernel, ti=ti),
        grid=grid,
        in_specs=[u_spec, blk(SSM_BLK_IN, 2 * SSM_BLK_ST), blk(4, SSM_BLK_ST)],
        out_specs=f_spec,
        out_shape=jax.ShapeDtypeStruct((bsz, nf, N_SUBSEQ, 2 * SSM_BLK_ST), F32),
        scratch_shapes=scratch,
        compiler_params=_cparams(sem),
        name="s5_pass1",
    )(p4, bdb, avec)
    out = pl.pallas_call(
        functools.partial(_s5_pass2_kernel, ti=ti),
        grid=grid,
        in_specs=[u_spec, f_spec, blk(SSM_BLK_IN, 2 * SSM_BLK_ST), blk(4, SSM_BLK_ST),
                  blk(2 * SSM_BLK_ST, SSM_BLK_IN), blk(SSM_BLK_IN, 2 * SSM_BLK_IN),
                  blk(2, 2 * SSM_BLK_IN)],
        out_specs=o_spec,
        out_shape=jax.ShapeDtypeStruct((bsz, N_SUBSEQ, lsub, width), BF16),
        scratch_shapes=scratch,
        compiler_params=_cparams(sem),
        name="s5_pass2",
    )(p4, fin, bdb, avec, bdc, glu, vec)
    return out.reshape(bsz, seqlen, width)


def _layer_norm_rows(y, g, b):
    mu = jnp.mean(y, axis=-1, keepdims=True)
    yc = y - mu
    var = jnp.mean(yc * yc, axis=-1, keepdims=True)
    return yc * lax.rsqrt(var + EPS) * g + b


def _pack_bf16_pairs(lo, hi):
    lo_bits = lax.bitcast_convert_type(lo.astype(BF16).astype(F32), jnp.uint32)
    hi_bits = lax.bitcast_convert_type(hi.astype(BF16).astype(F32), jnp.uint32)
    return (lo_bits >> 16) | (hi_bits & jnp.uint32(0xFFFF0000))


def _unpack_bf16_pairs(w):
    lo = lax.bitcast_convert_type(w << 16, F32)
    hi = lax.bitcast_convert_type(w & jnp.uint32(0xFFFF0000), F32)
    return lo, hi


def _outproj_kernel(attn_ref, ssm_ref, gs_ref, w_ref, x_ref, mod_ref, ln_ref, wr_ref, br_ref,
                    x1_ref, hp_ref, idx_ref, gate_ref, rank_ref, cnt_ref, mix_a, mix_b, *, nc):
    s = pl.program_id(0)

    @pl.when(s == 0)
    def _():
        mix_b[...] = jnp.zeros_like(mix_b)

    def step(mix_w, mix_r):
        ssm = ssm_ref[...].astype(F32)
        ms = jnp.mean(ssm * ssm, axis=-1, keepdims=True)
        ssm_n = (ssm * lax.rsqrt(ms + EPS) * gs_ref[...]).astype(BF16)
        attn = attn_ref[...]
        ka = attn.shape[1]
        for n0 in range(0, mix_w.shape[1], nc):
            mix_w[:, n0:n0 + nc] = (jnp.dot(attn, w_ref[:ka, n0:n0 + nc], preferred_element_type=F32)
                                    + jnp.dot(ssm_n, w_ref[ka:, n0:n0 + nc], preferred_element_type=F32))
        _outproj_epilogue(mix_r, x_ref, mod_ref, ln_ref, wr_ref, br_ref,
                          x1_ref, hp_ref, idx_ref, gate_ref, rank_ref, cnt_ref)

    @pl.when(s % 2 == 0)
    def _():
        step(mix_a, mix_b)

    @pl.when(s % 2 == 1)
    def _():
        step(mix_b, mix_a)


def _outproj_epilogue(mix_ref, x_ref, mod_ref, ln_ref, wr_ref, br_ref,
                      x1_ref, hp_ref, idx_ref, gate_ref, rank_ref, cnt_ref):
    m = mod_ref[...]
    d = x_ref.shape[1]
    y = DN_ALPHA * x_ref[...] + (1.0 + m[2:3, :]) * mix_ref[...]
    x1 = _layer_norm_rows(y, ln_ref[0:1, :], ln_ref[1:2, :])
    x1_ref[...] = x1
    h2 = x1 * (1.0 + m[4:5, :]) + m[3:4, :]
    hp_ref[...] = _pack_bf16_pairs(h2[:, :d // 2], h2[:, d // 2:])
    hi = h2.astype(BF16)
    lo = (h2 - hi.astype(F32)).astype(BF16)
    tm = h2.shape[0]
    r = jnp.dot(jnp.concatenate([hi, lo], axis=0), wr_ref[...], preferred_element_type=F32)
    logits = r[:tm, :N_EXPERTS] + r[:tm, N_EXPERTS:] + r[tm:, :N_EXPERTS] + br_ref[...]
    lane = lax.broadcasted_iota(jnp.int32, logits.shape, 1)
    vals, idxs = [], []
    for _ in range(TOP_K):
        mx = jnp.max(logits, axis=-1, keepdims=True)
        ix = jnp.min(jnp.where(logits == mx, lane, N_EXPERTS), axis=-1, keepdims=True)
        vals.append(mx)
        idxs.append(ix)
        logits = jnp.where(lane == ix, -jnp.inf, logits)
    tv = jnp.concatenate(vals, axis=1)
    e = jnp.exp(tv - vals[0])
    gate_ref[...] = e / jnp.sum(e, axis=-1, keepdims=True)
    idx_ref[...] = jnp.concatenate(idxs, axis=1)
    tri = (lax.broadcasted_iota(jnp.int32, (tm, tm), 0) > lax.broadcasted_iota(jnp.int32, (tm, tm), 1)).astype(BF16)
    run = jnp.zeros((1, N_EXPERTS), F32)
    ranks = []
    for ix in idxs:
        onehot = (lane == ix).astype(F32)
        before = jnp.dot(tri, onehot.astype(BF16), preferred_element_type=F32) + run
        ranks.append(jnp.sum(onehot * before, axis=-1, keepdims=True))
        run = run + jnp.sum(onehot, axis=0, keepdims=True)
    rank_ref[...] = jnp.concatenate(ranks, axis=1).astype(jnp.int32)
    cnt_ref[...] = jnp.zeros_like(cnt_ref)
    cnt_ref[0:1, 0:N_EXPERTS] = run


def _out_proj(attn_n, ssm, g_ssm, w_out_bf, x, mod3, ln1, wr, br, tm=ROUTE_TILE):
    bsz, seqlen, d = x.shape
    n_tok = bsz * seqlen
    ka = attn_n.shape[-1]
    ks = w_out_bf.shape[0] - ka
    nt = n_tok // tm
    per_b = seqlen // tm
    mm = lambda s: (jnp.minimum(s, nt - 1), 0)
    ep = lambda s: (jnp.maximum(s - 1, 0), 0)
    const = lambda s: (0, 0)
    outs = pl.pallas_call(
        functools.partial(_outproj_kernel, nc=512),
        grid=(nt + 1,),
        in_specs=[pl.BlockSpec((tm, ka), mm),
                  pl.BlockSpec((tm, ks), mm),
                  pl.BlockSpec((1, ks), const),
                  _resident((ka + ks, d), const),
                  pl.BlockSpec((tm, d), ep),
                  pl.BlockSpec((None, 6, d), lambda s: (jnp.maximum(s - 1, 0) // per_b, 0, 0)),
                  pl.BlockSpec((2, d), const),
                  pl.BlockSpec((d, 2 * N_EXPERTS), const),
                  pl.BlockSpec((1, N_EXPERTS), const)],
        out_specs=[pl.BlockSpec((tm, d), ep),
                   pl.BlockSpec((tm, d // 2), ep),
                   pl.BlockSpec((tm, TOP_K), ep),
                   pl.BlockSpec((tm, TOP_K), ep),
                   pl.BlockSpec((tm, TOP_K), ep),
                   pl.BlockSpec((None, SUBLANES, LANES), lambda s: (jnp.maximum(s - 1, 0), 0, 0))],
        out_shape=[jax.ShapeDtypeStruct((n_tok, d), F32),
                   jax.ShapeDtypeStruct((n_tok, d // 2), jnp.uint32),
                   jax.ShapeDtypeStruct((n_tok, TOP_K), jnp.int32),
                   jax.ShapeDtypeStruct((n_tok, TOP_K), F32),
                   jax.ShapeDtypeStruct((n_tok, TOP_K), jnp.int32),
                   jax.ShapeDtypeStruct((nt, SUBLANES, LANES), F32)],
        scratch_shapes=[pltpu.VMEM((tm, d), F32), pltpu.VMEM((tm, d), F32)],
        compiler_params=_cparams(("arbitrary",)),
        name="out_proj_ln_router",
    )(attn_n.reshape(n_tok, ka), ssm.reshape(n_tok, ks), g_ssm.reshape(1, ks), w_out_bf,
      x.reshape(n_tok, d), mod3, ln1, wr, br)
    x1, hp = outs[0].reshape(bsz, seqlen, d), outs[1]
    return (x1, hp) + tuple(outs[2:])


def _stream_expert_weights(blk_e, first, nxt, last, w_hbm, stage, wbf, sem):
    ct = pl.program_id(0)
    rb = pl.program_id(1)
    width = wbf[0].shape[1]

    def copies(e, col_tile):
        c0 = pl.multiple_of(col_tile * width, width)
        return [pltpu.make_async_copy(w.at[e, :, pl.ds(c0, width)], s, sem.at[j])
                for j, (w, s) in enumerate(zip(w_hbm, stage))]

    @pl.when((ct == 0) & (rb == 0))
    def _():
        for cp in copies(blk_e[0], 0):
            cp.start()

    @pl.when(first[rb] == 1)
    def _():
        for cp in copies(blk_e[rb], ct):
            cp.wait()
        def convert(c, carry):
            r0 = pl.multiple_of(c * CAST_ROWS, CAST_ROWS)
            for s, w in zip(stage, wbf):
                w[pl.ds(r0, CAST_ROWS), :] = s[pl.ds(r0, CAST_ROWS), :].astype(BF16)
            return carry
        lax.fori_loop(0, stage[0].shape[0] // CAST_ROWS, convert, 0)

        @pl.when(last[rb] == 0)
        def _():
            for cp in copies(nxt[rb], ct):
                cp.start()

        @pl.when((last[rb] == 1) & (ct + 1 < pl.num_programs(0)))
        def _():
            for cp in copies(nxt[rb], ct + 1):
                cp.start()


def _expert_up_kernel(blk_e, first, nreal, nxt, last, xs_ref, wg_hbm, wu_hbm, bg_ref, bu_ref, act_ref,
                      stg_g, stg_u, wgb, wub, sem):
    rb = pl.program_id(1)
    _stream_expert_weights(blk_e, first, nxt, last, (wg_hbm, wu_hbm), (stg_g, stg_u), (wgb, wub), sem)

    @pl.when(rb < nreal[0])
    def _():
        x = jnp.concatenate(_unpack_bf16_pairs(xs_ref[...]), axis=1).astype(BF16)
        g = jnp.dot(x, wgb[...], preferred_element_type=F32) + bg_ref[...]
        up = jnp.dot(x, wub[...], preferred_element_type=F32) + bu_ref[...]
        g = jnp.minimum(g, SWIGLU_LIMIT)
        up = jnp.clip(up, -SWIGLU_LIMIT, SWIGLU_LIMIT)
        act_ref[...] = (g * jax.nn.sigmoid(SWIGLU_ALPHA * g) * (up + 1.0)).astype(BF16)

    @pl.when(rb >= nreal[0])
    def _():
        act_ref[...] = jnp.zeros_like(act_ref)


def _expert_down_kernel(blk_e, first, nreal, nxt, last, act_ref, wd_hbm, bd_ref, y_ref, stg, wdb, sem):
    rb = pl.program_id(1)
    _stream_expert_weights(blk_e, first, nxt, last, (wd_hbm,), (stg,), (wdb,), sem)

    @pl.when(rb < nreal[0])
    def _():
        y = jnp.dot(act_ref[...], wdb[...], preferred_element_type=F32) + bd_ref[...]
        half = y.shape[1] // 2
        y_ref[...] = _pack_bf16_pairs(y[:, :half], y[:, half:])

    @pl.when(rb >= nreal[0])
    def _():
        y_ref[...] = jnp.zeros_like(y_ref)


def _experts(xs, sched, w_gate, b_gate, w_up, b_up, w_down, b_down, tf=512, tn=DOWN_TILE):
    cap = xs.shape[0]
    n_e, d, dff = w_gate.shape
    nblk = cap // EXPERT_BLK
    hbm = pl.BlockSpec(memory_space=pl.ANY)
    act = pl.pallas_call(
        _expert_up_kernel,
        grid_spec=pltpu.PrefetchScalarGridSpec(
            num_scalar_prefetch=5,
            grid=(dff // tf, nblk),
            in_specs=[pl.BlockSpec((EXPERT_BLK, d // 2), lambda f, r, be, *_: (r, 0)),
                      hbm, hbm,
                      pl.BlockSpec((None, 1, tf), lambda f, r, be, *_: (be[r], 0, f)),
                      pl.BlockSpec((None, 1, tf), lambda f, r, be, *_: (be[r], 0, f))],
            out_specs=pl.BlockSpec((EXPERT_BLK, tf), lambda f, r, be, *_: (r, f)),
            scratch_shapes=[pltpu.VMEM((d, tf), F32), pltpu.VMEM((d, tf), F32),
                            pltpu.VMEM((d, tf), BF16), pltpu.VMEM((d, tf), BF16),
                            pltpu.SemaphoreType.DMA((2,))]),
        out_shape=jax.ShapeDtypeStruct((cap, dff), BF16),
        compiler_params=_cparams(("arbitrary", "arbitrary")),
        name="expert_gate_up",
    )(*sched, xs, w_gate, w_up, b_gate.reshape(n_e, 1, dff), b_up.reshape(n_e, 1, dff))
    ys = pl.pallas_call(
        _expert_down_kernel,
        grid_spec=pltpu.PrefetchScalarGridSpec(
            num_scalar_prefetch=5,
            grid=(d // tn, nblk),
            in_specs=[pl.BlockSpec((EXPERT_BLK, dff), lambda n, r, be, *_: (r, 0)),
                      hbm,
                      pl.BlockSpec((None, 1, tn), lambda n, r, be, *_: (be[r], 0, n))],
            out_specs=pl.BlockSpec((EXPERT_BLK, tn // 2), lambda n, r, be, *_: (r, n)),
            scratch_shapes=[pltpu.VMEM((dff, tn), F32), pltpu.VMEM((dff, tn), BF16),
                            pltpu.SemaphoreType.DMA((1,))]),
        out_shape=jax.ShapeDtypeStruct((cap, d // 2), jnp.uint32),
        compiler_params=_cparams(("arbitrary", "arbitrary")),
        name="expert_down",
    )(*sched, act, w_down, b_down.reshape(n_e, 1, d))
    return ys


def _route_tables(cnt, top_idx, rank):
    n_assign = top_idx.size
    counts = cnt[:, 0, :N_EXPERTS].astype(jnp.int32)
    tot = jnp.sum(counts, axis=0)
    padded = (tot + EXPERT_BLK - 1) // EXPERT_BLK * EXPERT_BLK
    pend = jnp.cumsum(padded)
    pstart = pend - padded
    base = pstart[None, :] + jnp.cumsum(counts, axis=0) - counts
    experts = jnp.arange(N_EXPERTS, dtype=jnp.int32)
    idx_t = top_idx.reshape(-1, ROUTE_TILE * TOP_K)
    dest = jnp.sum(jnp.where(idx_t[..., None] == experts, base[:, None, :], 0), axis=-1)
    dest = (dest.reshape(-1) + rank.reshape(-1)).astype(jnp.int32)
    cap = ((n_assign + EXPERT_BLK - 1) // EXPERT_BLK) * EXPERT_BLK + N_EXPERTS * EXPERT_BLK
    nblk = cap // EXPERT_BLK
    blk = jnp.arange(nblk, dtype=jnp.int32)
    blk_e = jnp.sum((pend[None, :] <= blk[:, None] * EXPERT_BLK).astype(jnp.int32), axis=1)
    blk_e = jnp.minimum(blk_e, N_EXPERTS - 1).astype(jnp.int32)
    nreal = (pend[-1:] // EXPERT_BLK).astype(jnp.int32)
    real = blk < nreal[0]
    first = real & jnp.concatenate([jnp.ones((1,), bool), blk_e[1:] != blk_e[:-1]])
    starts = jnp.where(first, blk, nblk)
    nxt_blk = jnp.concatenate([lax.cummin(starts, reverse=True)[1:], jnp.full((1,), nblk, jnp.int32)])
    last = nxt_blk >= nblk
    nxt = jnp.where(last, blk_e[0], blk_e[jnp.minimum(nxt_blk, nblk - 1)]).astype(jnp.int32)
    sched = (blk_e, first.astype(jnp.int32), nreal, nxt, last.astype(jnp.int32))
    return dest, cap, sched, (pstart + tot).astype(jnp.int32), (padded - tot).astype(jnp.int32)


def _row_copy(src, src_row, dst, dst_row, sem):
    return pltpu.make_async_copy(src.at[pl.ds(src_row, 1)], dst.at[pl.ds(dst_row, 1)], sem)


def _dispatch_kernel(pad0_ref, padn_ref, nreal_ref, hp_ref, dest_ref, xs_ref, zblk, sem, zsem, bsem):
    i = pl.program_id(0)
    tm = hp_ref.shape[0]
    nblk = xs_ref.shape[0] // EXPERT_BLK

    def for_each_pad_row(fn):
        def per_expert(e, c):
            def per_row(r, c2):
                fn(_row_copy(zblk, 0, xs_ref, pad0_ref[e] + r, zsem))
                return c2
            return lax.fori_loop(0, padn_ref[e], per_row, c)
        lax.fori_loop(0, N_EXPERTS, per_expert, 0)

    def for_each_unused_block(fn):
        def per_block(b, c):
            r0 = pl.multiple_of(b * EXPERT_BLK, EXPERT_BLK)
            fn(pltpu.make_async_copy(zblk, xs_ref.at[pl.ds(r0, EXPERT_BLK)], bsem))
            return c
        lax.fori_loop(nreal_ref[0], nblk, per_block, 0)

    @pl.when(i == 0)
    def _():
        zblk[...] = jnp.zeros_like(zblk)
        for_each_pad_row(lambda cp: cp.start())
        for_each_unused_block(lambda cp: cp.start())

    def issue(r, c):
        for k in range(TOP_K):
            _row_copy(hp_ref, r, xs_ref, dest_ref[r * TOP_K + k], sem).start()
        return c
    lax.fori_loop(0, tm, issue, 0, unroll=2)

    def drain(r, c):
        for k in range(TOP_K):
            _row_copy(hp_ref, 0, xs_ref, 0, sem).wait()
        return c
    lax.fori_loop(0, tm, drain, 0)

    @pl.when(i == 0)
    def _():
        for_each_pad_row(lambda cp: cp.wait())
        for_each_unused_block(lambda cp: cp.wait())


def _dispatch(hp, dest, pad0, padn, nreal, cap, tm=MOVE_TILE):
    n_tok, half = hp.shape
    return pl.pallas_call(
        _dispatch_kernel,
        grid_spec=pltpu.PrefetchScalarGridSpec(
            num_scalar_prefetch=3,
            grid=(n_tok // tm,),
            in_specs=[pl.BlockSpec((tm, half), lambda i, *_: (i, 0)),
                      pl.BlockSpec((tm * TOP_K,), lambda i, *_: (i,), memory_space=pltpu.SMEM)],
            out_specs=pl.BlockSpec(memory_space=pl.ANY),
            scratch_shapes=[pltpu.VMEM((EXPERT_BLK, half), jnp.uint32),
                            pltpu.SemaphoreType.DMA, pltpu.SemaphoreType.DMA, pltpu.SemaphoreType.DMA]),
        out_shape=jax.ShapeDtypeStruct((cap, half), jnp.uint32),
        compiler_params=_cparams(("arbitrary",)),
        name="moe_dispatch",
    )(pad0, padn, nreal, hp, dest)


def _combine_kernel(dest_ref, gate_ref, x1_ref, mod_ref, ln_ref, ys_ref, o_ref, buf, sem, *, tn):
    tm = x1_ref.shape[0]

    def issue(r, c):
        for k in range(TOP_K):
            pltpu.make_async_copy(ys_ref.at[pl.ds(dest_ref[r * TOP_K + k], 1)],
                                  buf.at[k, pl.ds(r, 1)], sem).start()
        return c
    lax.fori_loop(0, tm, issue, 0, unroll=2)

    def drain(r, c):
        for k in range(TOP_K):
            pltpu.make_async_copy(ys_ref.at[pl.ds(0, 1)], buf.at[0, pl.ds(0, 1)], sem).wait()
        return c
    lax.fori_loop(0, tm, drain, 0)

    g = gate_ref[...]
    m = mod_ref[...]
    d = x1_ref.shape[1]
    hw = tn // 2
    cw = 512
    for w0 in range(0, d // 2, cw):
        lo = hi = None
        for k in range(TOP_K):
            wl, wh = _unpack_bf16_pairs(buf[k, :, w0:w0 + cw])
            lo = g[:, k:k + 1] * wl if lo is None else lo + g[:, k:k + 1] * wl
            hi = g[:, k:k + 1] * wh if hi is None else hi + g[:, k:k + 1] * wh
        for half, moe in ((0, lo), (1, hi)):
            c0 = (w0 // hw) * tn + half * hw + w0 % hw
            o_ref[:, c0:c0 + cw] = DN_ALPHA * x1_ref[:, c0:c0 + cw] + (1.0 + m[5:6, c0:c0 + cw]) * moe
    o_ref[...] = _layer_norm_rows(o_ref[...], ln_ref[0:1, :], ln_ref[1:2, :])


def _combine(ys, dest, gates, x1, mod3, ln2, tm=MOVE_TILE, tn=DOWN_TILE):
    bsz, seqlen, d = x1.shape
    n_tok = bsz * seqlen
    per_b = seqlen // tm
    out = pl.pallas_call(
        functools.partial(_combine_kernel, tn=tn),
        grid=(n_tok // tm,),
        in_specs=[pl.BlockSpec((tm * TOP_K,), lambda i: (i,), memory_space=pltpu.SMEM),
                  pl.BlockSpec((tm, TOP_K), lambda i: (i, 0)),
                  pl.BlockSpec((tm, d), lambda i: (i, 0)),
                  pl.BlockSpec((None, 6, d), lambda i: (i // per_b, 0, 0)),
                  pl.BlockSpec((2, d), lambda i: (0, 0)),
                  pl.BlockSpec(memory_space=pl.ANY)],
        out_specs=pl.BlockSpec((tm, d), lambda i: (i, 0)),
        out_shape=jax.ShapeDtypeStruct((n_tok, d), F32),
        scratch_shapes=[pltpu.VMEM((TOP_K, tm, d // 2), jnp.uint32), pltpu.SemaphoreType.DMA],
        compiler_params=_cparams(("arbitrary",)),
        name="moe_combine_ln",
    )(dest, gates.reshape(n_tok, TOP_K), x1.reshape(n_tok, d), mod3, ln2, ys)
    return out.reshape(bsz, seqlen, d)


def kernel(x, c, positions, w_ada, b_ada, w_in, attn_sinks, ssm_a_re, ssm_a_im, ssm_b_re, ssm_b_im,
           ssm_c_re, ssm_c_im, ssm_d, ssm_log_dt, ssm_w_glu, ssm_b_glu, g_attn_out, g_ssm_out, w_out,
           ln1_g, ln1_b, w_router, b_router, w_gate, b_gate, w_up, b_up, w_down, b_down, ln2_g, ln2_b):
    bsz, seqlen, d = x.shape
    lsub = seqlen // N_SUBSEQ
    n_tok = bsz * seqlen
    rope_tab = _rope_tables(positions)
    for l in range(w_ada.shape[0]):
        mod3 = _ada_mod(c, w_ada[l], b_ada[l]).reshape(bsz, 6, d)
        proj = _in_proj(x, mod3, w_in[l].astype(BF16))
        attn_n = _attention(proj, rope_tab, attn_sinks[l].astype(F32), g_attn_out[l].astype(F32))
        s5p = _s5_params(ssm_a_re[l], ssm_a_im[l], ssm_b_re[l], ssm_b_im[l], ssm_c_re[l], ssm_c_im[l],
                         ssm_d[l], ssm_log_dt[l], ssm_w_glu[l], ssm_b_glu[l], lsub)
        ssm = _s5(proj, s5p)
        wr_hi = w_router[l].astype(BF16)
        wr_lo = (w_router[l] - wr_hi.astype(F32)).astype(BF16)
        x1, hp, top_idx, gates, rank, cnt = _out_proj(
            attn_n, ssm, g_ssm_out[l].astype(F32), w_out[l].astype(BF16), x, mod3,
            jnp.stack([ln1_g[l], ln1_b[l]]).astype(F32),
            jnp.concatenate([wr_hi, wr_lo], axis=1), b_router[l].reshape(1, N_EXPERTS).astype(F32))
        dest, cap, sched, pad0, padn = _route_tables(cnt, top_idx, rank)
        xs = _dispatch(hp.reshape(n_tok, d // 2), dest, pad0, padn, sched[2], cap)
        ys = _experts(xs, sched, w_gate[l], b_gate[l], w_up[l], b_up[l], w_down[l], b_down[l])
        x = _combine(ys, dest, gates, x1, mod3, jnp.stack([ln2_g[l], ln2_b[l]]).astype(F32))
    return x
```

```python
import functools
import math

import jax
import jax.numpy as jnp
from jax import lax
from jax.experimental import pallas as pl
from jax.experimental.pallas import tpu as pltpu

F32 = jnp.float32
BF16 = jnp.bfloat16

HEAD_DIM = 64
N_Q_HEADS = 32
N_KV_HEADS = 4
GQ = N_Q_HEADS // N_KV_HEADS
ATTN_WIDTH = N_Q_HEADS * HEAD_DIM
KV_WIDTH = N_KV_HEADS * HEAD_DIM
QKV_WIDTH = ATTN_WIDTH + 2 * KV_WIDTH
BLK = 128
ROT_DIM = HEAD_DIM // 4
ROPE_THETA = 500000.0
GROUP_CH = 16
STATE = 64
N_EXPERTS = 32
TOP_K = 4
SWIGLU_LIMIT = 7.0
SWIGLU_ALPHA = 1.702
EXPERT_BLK = 256
ROUTE_TILE = 128
MOVE_TILE = 256
DOWN_TILE = 4096
CAST_ROWS = 128
DEPTH = 1
DN_ALPHA = (2.0 * DEPTH) ** 0.25
EPS = 1e-5

LANES = 128
SUBLANES = 8
N_SUBSEQ = SUBLANES
GROUPS_PER_BLK = 16
SSM_BLK_IN = GROUPS_PER_BLK * GROUP_CH
SSM_BLK_ST = GROUPS_PER_BLK * STATE
VMEM_LIMIT = 56 * 1024 * 1024


def _cparams(sem, vmem=VMEM_LIMIT):
    return pltpu.CompilerParams(dimension_semantics=sem, vmem_limit_bytes=vmem)


def _resident(shape, index_map):
    return pl.BlockSpec(shape, index_map, pipeline_mode=pl.Buffered(1))


def _ada_kernel(c_ref, w_ref, b_ref, o_ref):
    c = c_ref[...]
    ca = c * jax.nn.sigmoid(c)
    o_ref[...] = jnp.dot(ca.astype(BF16), w_ref[...].astype(BF16),
                         preferred_element_type=F32) + b_ref[...]


def _ada_mod(c, w_ada, b_ada, tn=512):
    bsz, d = c.shape
    n = w_ada.shape[1]
    c8 = jnp.zeros((SUBLANES, d), F32).at[:bsz].set(c)
    out = pl.pallas_call(
        _ada_kernel,
        grid=(n // tn,),
        in_specs=[pl.BlockSpec((SUBLANES, d), lambda j: (0, 0)),
                  pl.BlockSpec((d, tn), lambda j: (0, j)),
                  pl.BlockSpec((1, tn), lambda j: (0, j))],
        out_specs=pl.BlockSpec((SUBLANES, tn), lambda j: (0, j)),
        out_shape=jax.ShapeDtypeStruct((SUBLANES, n), F32),
        compiler_params=_cparams(("arbitrary",)),
        name="ada_mod",
    )(c8, w_ada, b_ada.reshape(1, n))
    return out[:bsz]


def _inproj_kernel(x_ref, mod_ref, w_ref, o_ref, *, nc):
    m = mod_ref[...]
    h = (x_ref[...] * (1.0 + m[1:2, :]) + m[0:1, :]).astype(BF16)
    for n0 in range(0, o_ref.shape[-1], nc):
        o_ref[:, n0:n0 + nc] = jnp.dot(
            h, w_ref[:, n0:n0 + nc], preferred_element_type=F32).astype(BF16)


def _in_proj(x, mod3, w_in_bf, tm=128):
    bsz, seqlen, d = x.shape
    n_in = w_in_bf.shape[1]
    return pl.pallas_call(
        functools.partial(_inproj_kernel, nc=512),
        grid=(bsz, seqlen // tm),
        in_specs=[pl.BlockSpec((None, tm, d), lambda b, i: (b, i, 0)),
                  pl.BlockSpec((None, 6, d), lambda b, i: (b, 0, 0)),
                  _resident((d, n_in), lambda b, i: (0, 0))],
        out_specs=pl.BlockSpec((None, tm, n_in), lambda b, i: (b, i, 0)),
        out_shape=jax.ShapeDtypeStruct((bsz, seqlen, n_in), BF16),
        compiler_params=_cparams(("arbitrary", "arbitrary")),
        name="in_proj",
    )(x, mod3, w_in_bf)


def _rope(t, tab):
    c, s_lo, s_hi = tab[:, :LANES], tab[:, LANES:2 * LANES], tab[:, 2 * LANES:]
    half = ROT_DIM // 2
    out = []
    for j in range(t.shape[1] // LANES):
        tj = t[:, j * LANES:(j + 1) * LANES]
        out.append(tj * c + pltpu.roll(tj, LANES - half, 1) * s_lo + pltpu.roll(tj, half, 1) * s_hi)
    return out


def _attn_kernel(sink_ref, q_ref, kc_ref, kp_ref, vc_ref, vp_ref, tc_ref, tp_ref, g_ref, o_ref):
    n = pl.program_id(1)
    low = lax.broadcasted_iota(jnp.int32, (2 * BLK, LANES), 1) < HEAD_DIM
    low_q = lax.broadcasted_iota(jnp.int32, (BLK, LANES), 1) < HEAD_DIM

    q_chunks = _rope(q_ref[...].astype(F32) * (HEAD_DIM ** -0.5), tc_ref[...])
    k_raw = jnp.concatenate([kp_ref[...], kc_ref[...]], axis=0).astype(F32)
    k_chunks = _rope(k_raw, jnp.concatenate([tp_ref[...], tc_ref[...]], axis=0))
    v_raw = jnp.concatenate([vp_ref[...], vc_ref[...]], axis=0).astype(F32)

    qi = lax.broadcasted_iota(jnp.int32, (BLK, BLK), 0)
    kj = lax.broadcasted_iota(jnp.int32, (BLK, BLK), 1)
    own = kj <= qi
    prev_ok = kj >= jnp.where(n > 0, 0, BLK)

    o_chunks = []
    for hk in range(N_KV_HEADS):
        kc = k_chunks[hk // 2]
        vc = v_raw[:, (hk // 2) * LANES:(hk // 2 + 1) * LANES]
        k_sw = pltpu.roll(kc, HEAD_DIM, 1)
        v_sw = pltpu.roll(vc, HEAD_DIM, 1)
        if hk % 2 == 0:
            kk2 = jnp.where(low, kc, k_sw)
            v_lo = jnp.where(low, vc, 0.0)
            v_hi = jnp.where(low, 0.0, v_sw)
        else:
            kk2 = jnp.where(low, k_sw, kc)
            v_lo = jnp.where(low, v_sw, 0.0)
            v_hi = jnp.where(low, 0.0, vc)
        kk2 = kk2.astype(BF16)
        v_lo = v_lo.astype(BF16)
        v_hi = v_hi.astype(BF16)
        lhs = []
        for j in range(GQ // 2):
            q2 = q_chunks[hk * (GQ // 2) + j]
            lhs.append(jnp.where(low_q, q2, 0.0).astype(BF16))
            lhs.append(jnp.where(low_q, 0.0, q2).astype(BF16))
        s_all = lax.dot_general(jnp.concatenate(lhs, axis=0), kk2,
                                (((1,), (1,)), ((), ())), preferred_element_type=F32)
        for j in range(GQ // 2):
            acc = None
            for side, vv in ((0, v_lo), (1, v_hi)):
                i = 2 * j + side
                s_prev = jnp.where(prev_ok, s_all[i * BLK:(i + 1) * BLK, :BLK], -1e30)
                s = jnp.where(own, s_all[i * BLK:(i + 1) * BLK, BLK:], s_prev)
                sink = sink_ref[hk * GQ + i]
                m = jnp.maximum(jnp.max(s, axis=-1, keepdims=True), sink)
                p = jnp.exp(s - m)
                denom = jnp.sum(p, axis=-1, keepdims=True) + jnp.exp(sink - m)
                p = p * (1.0 / denom)
                p = jnp.concatenate([jnp.where(own, 0.0, p), jnp.where(own, p, 0.0)], axis=1).astype(BF16)
                o = jnp.dot(p, vv, preferred_element_type=F32)
                acc = o if acc is None else acc + o
            o_chunks.append(acc)

    ssq = None
    for oc in o_chunks:
        t = jnp.sum(oc * oc, axis=-1, keepdims=True)
        ssq = t if ssq is None else ssq + t
    inv = lax.rsqrt(ssq * (1.0 / ATTN_WIDTH) + EPS)
    for j, oc in enumerate(o_chunks):
        o_ref[:, j * LANES:(j + 1) * LANES] = (oc * inv * g_ref[:, j * LANES:(j + 1) * LANES]).astype(BF16)


def _attention(qkv, rope_tab, sinks, g_attn):
    bsz, seqlen, _ = qkv.shape
    nb = seqlen // BLK
    kcol = ATTN_WIDTH // KV_WIDTH
    cur = lambda b, n: (b, n, 0)
    prev = lambda b, n: (b, jnp.maximum(n - 1, 0), 0)
    return pl.pallas_call(
        _attn_kernel,
        grid=(bsz, nb),
        in_specs=[pl.BlockSpec(memory_space=pltpu.SMEM),
                  pl.BlockSpec((None, BLK, ATTN_WIDTH), cur),
                  pl.BlockSpec((None, BLK, KV_WIDTH), lambda b, n: (b, n, kcol)),
                  pl.BlockSpec((None, BLK, KV_WIDTH), lambda b, n: (b, jnp.maximum(n - 1, 0), kcol)),
                  pl.BlockSpec((None, BLK, KV_WIDTH), lambda b, n: (b, n, kcol + 1)),
                  pl.BlockSpec((None, BLK, KV_WIDTH), lambda b, n: (b, jnp.maximum(n - 1, 0), kcol + 1)),
                  pl.BlockSpec((None, BLK, 3 * LANES), cur),
                  pl.BlockSpec((None, BLK, 3 * LANES), prev),
                  pl.BlockSpec((1, ATTN_WIDTH), lambda b, n: (0, 0))],
        out_specs=pl.BlockSpec((None, BLK, ATTN_WIDTH), cur),
        out_shape=jax.ShapeDtypeStruct((bsz, seqlen, ATTN_WIDTH), BF16),
        compiler_params=_cparams(("arbitrary", "arbitrary")),
        name="swa_attention",
    )(sinks, qkv, qkv, qkv, qkv, qkv, rope_tab, rope_tab, g_attn.reshape(1, ATTN_WIDTH))


def _rope_tables(positions):
    half = ROT_DIM // 2
    inv_freq = ROPE_THETA ** (-jnp.arange(0, ROT_DIM, 2, dtype=F32) / ROT_DIM)
    ang = positions.astype(F32)[..., None] * inv_freq
    cos, sin = jnp.cos(ang), jnp.sin(ang)
    shp = cos.shape[:-1] + (HEAD_DIM - ROT_DIM,)
    c = jnp.concatenate([cos, cos, jnp.ones(shp, F32)], axis=-1)
    z8 = jnp.zeros_like(sin)
    s_lo = jnp.concatenate([-sin, z8, jnp.zeros(shp, F32)], axis=-1)
    s_hi = jnp.concatenate([z8, sin, jnp.zeros(shp, F32)], axis=-1)
    rep = LANES // HEAD_DIM
    return jnp.concatenate([jnp.tile(c, rep), jnp.tile(s_lo, rep), jnp.tile(s_hi, rep)], axis=-1)


def _s5_scan(buf, ar, ai, hr, hi, ti, store):
    def step(i, carry):
        hr, hi = carry
        r0 = pl.multiple_of(i * SUBLANES, SUBLANES)
        row = buf[pl.ds(r0, SUBLANES), :]
        nhr = ar * hr - ai * hi + row[:, :SSM_BLK_ST]
        nhi = ar * hi + ai * hr + row[:, SSM_BLK_ST:]
        if store:
            buf[pl.ds(r0, SUBLANES), :] = jnp.concatenate([nhr, nhi], axis=1)
        return nhr, nhi
    return lax.fori_loop(0, ti, step, (hr, hi), unroll=4)


def _time_major(u_ref, ti):
    u = pltpu.einshape("jid->ijd", u_ref[...].astype(F32))
    return u.reshape(ti * N_SUBSEQ, u.shape[-1])


def _s5_pass1_kernel(u_ref, bdb_ref, a_ref, f_ref, buf, hst, *, ti):
    ic = pl.program_id(2)

    @pl.when(ic == 0)
    def _():
        hst[...] = jnp.zeros_like(hst)

    u = _time_major(u_ref, ti)
    buf[...] = jnp.dot(u.astype(BF16), bdb_ref[...], preferred_element_type=F32)
    ar = jnp.broadcast_to(a_ref[0:1, :], (SUBLANES, SSM_BLK_ST))
    ai = jnp.broadcast_to(a_ref[1:2, :], (SUBLANES, SSM_BLK_ST))
    hr, hi = _s5_scan(buf, ar, ai, hst[:, :SSM_BLK_ST], hst[:, SSM_BLK_ST:], ti, store=False)
    hst[...] = jnp.concatenate([hr, hi], axis=1)

    @pl.when(ic == pl.num_programs(2) - 1)
    def _():
        f_ref[...] = hst[...]


def _s5_pass2_kernel(u_ref, f_ref, bdb_ref, a_ref, bdc_ref, glu_ref, vec_ref, o_ref, buf, hst, *, ti):
    ic = pl.program_id(2)

    @pl.when(ic == 0)
    def _():
        fr, fi = f_ref[:, :SSM_BLK_ST], f_ref[:, SSM_BLK_ST:]
        pr = jnp.broadcast_to(a_ref[2:3, :], (SUBLANES, SSM_BLK_ST))
        pi = jnp.broadcast_to(a_ref[3:4, :], (SUBLANES, SSM_BLK_ST))
        row = lax.broadcasted_iota(jnp.int32, (SUBLANES, SSM_BLK_ST), 0)
        hr = jnp.zeros((SUBLANES, SSM_BLK_ST), F32)
        hi = jnp.zeros((SUBLANES, SSM_BLK_ST), F32)
        for _ in range(N_SUBSEQ - 1):
            nr = pr * hr - pi * hi + fr
            ni = pr * hi + pi * hr + fi
            hr = jnp.where(row == 0, 0.0, pltpu.roll(nr, 1, 0))
            hi = jnp.where(row == 0, 0.0, pltpu.roll(ni, 1, 0))
        hst[...] = jnp.concatenate([hr, hi], axis=1)

    u = _time_major(u_ref, ti)
    buf[...] = jnp.dot(u.astype(BF16), bdb_ref[...], preferred_element_type=F32)
    ar = jnp.broadcast_to(a_ref[0:1, :], (SUBLANES, SSM_BLK_ST))
    ai = jnp.broadcast_to(a_ref[1:2, :], (SUBLANES, SSM_BLK_ST))
    hr, hi = _s5_scan(buf, ar, ai, hst[:, :SSM_BLK_ST], hst[:, SSM_BLK_ST:], ti, store=True)
    hst[...] = jnp.concatenate([hr, hi], axis=1)

    y = jnp.dot(buf[...].astype(BF16), bdc_ref[...], preferred_element_type=F32)
    y = jax.nn.gelu(y + vec_ref[0:1, :SSM_BLK_IN] * u)
    z = jnp.dot(y.astype(BF16), glu_ref[...], preferred_element_type=F32) + vec_ref[1:2, :]
    out = z[:, :SSM_BLK_IN] * jax.nn.sigmoid(z[:, SSM_BLK_IN:])
    out = pltpu.einshape("ijd->jid", out.reshape(ti, N_SUBSEQ, SSM_BLK_IN))
    o_ref[...] = out.astype(BF16)


def _s5_params(a_re, a_im, b_re, b_im, c_re, c_im, d_skip, log_dt, w_glu, b_glu, lsub):
    g = a_re.shape[0]
    nf = g // GROUPS_PER_BLK
    a = lax.complex(a_re.astype(F32), a_im.astype(F32))
    dt = jnp.exp(log_dt.astype(F32))[:, None]
    a_bar = jnp.exp(a * dt)
    a_pow = jnp.exp(a * dt * lsub)
    b_bar = ((a_bar - 1.0) / a)[..., None] * lax.complex(b_re.astype(F32), b_im.astype(F32))
    eye = jnp.eye(GROUPS_PER_BLK, dtype=F32)

    def bd_in(m):
        m = m.reshape(nf, GROUPS_PER_BLK, STATE, GROUP_CH)
        return jnp.einsum('fgpc,gh->fgchp', m, eye).reshape(nf, SSM_BLK_IN, SSM_BLK_ST)

    def bd_out(m):
        m = m.reshape(nf, GROUPS_PER_BLK, GROUP_CH, STATE)
        return jnp.einsum('fgcp,gh->fgphc', m, eye).reshape(nf, SSM_BLK_ST, SSM_BLK_IN)

    def bd_glu(m):
        m = m.reshape(nf, GROUPS_PER_BLK, GROUP_CH, GROUP_CH)
        return jnp.einsum('fgcd,gh->fgchd', m, eye).reshape(nf, SSM_BLK_IN, SSM_BLK_IN)

    bdb = jnp.concatenate([bd_in(jnp.real(b_bar)), bd_in(jnp.imag(b_bar))], axis=2).astype(BF16)
    bdc = jnp.concatenate([bd_out(c_re.astype(F32)), bd_out(-c_im.astype(F32))], axis=1).astype(BF16)
    wg = w_glu.astype(F32)
    glu = jnp.concatenate([bd_glu(wg[..., :GROUP_CH]), bd_glu(wg[..., GROUP_CH:])], axis=2).astype(BF16)
    flat = lambda m: m.reshape(nf, 1, SSM_BLK_ST)
    avec = jnp.concatenate([flat(jnp.real(a_bar)), flat(jnp.imag(a_bar)),
                            flat(jnp.real(a_pow)), flat(jnp.imag(a_pow))], axis=1)
    bg = b_glu.astype(F32).reshape(nf, GROUPS_PER_BLK, 2 * GROUP_CH)
    bvec = jnp.concatenate([bg[..., :GROUP_CH].reshape(nf, 1, SSM_BLK_IN),
                            bg[..., GROUP_CH:].reshape(nf, 1, SSM_BLK_IN)], axis=2)
    dvec = jnp.concatenate([d_skip.astype(F32).reshape(nf, 1, SSM_BLK_IN),
                            jnp.zeros((nf, 1, SSM_BLK_IN), F32)], axis=2)
    vec = jnp.concatenate([dvec, bvec], axis=1)
    return bdb, bdc, glu, avec, vec


def _s5(proj, params, ti=128):
    bdb, bdc, glu, avec, vec = params
    bsz, seqlen, n_in = proj.shape
    width = n_in - QKV_WIDTH
    nf = width // SSM_BLK_IN
    lsub = seqlen // N_SUBSEQ
    ti = min(ti, lsub)
    u_col0 = QKV_WIDTH // SSM_BLK_IN
    p4 = proj.reshape(bsz, N_SUBSEQ, lsub, n_in)
    grid = (bsz, nf, lsub // ti)
    u_spec = pl.BlockSpec((None, N_SUBSEQ, ti, SSM_BLK_IN), lambda b, f, i: (b, 0, i, u_col0 + f))
    o_spec = pl.BlockSpec((None, N_SUBSEQ, ti, SSM_BLK_IN), lambda b, f, i: (b, 0, i, f))
    blk = lambda r, c: pl.BlockSpec((None, r, c), lambda b, f, i: (f, 0, 0))
    f_spec = pl.BlockSpec((None, None, N_SUBSEQ, 2 * SSM_BLK_ST), lambda b, f, i: (b, f, 0, 0))
    scratch = [pltpu.VMEM((ti * SUBLANES, 2 * SSM_BLK_ST), F32),
               pltpu.VMEM((SUBLANES, 2 * SSM_BLK_ST), F32)]
    sem = ("arbitrary", "arbitrary", "arbitrary")
    fin = pl.pallas_call(
        functools.partial(_s5_pass1_kernel, ti=ti),
        grid=grid,
        in_specs=[u_spec, blk(SSM_BLK_IN, 2 * SSM_BLK_ST), blk(4, SSM_BLK_ST)],
        out_specs=f_spec,
        out_shape=jax.ShapeDtypeStruct((bsz, nf, N_SUBSEQ, 2 * SSM_BLK_ST), F32),
        scratch_shapes=scratch,
        compiler_params=_cparams(sem),
        name="s5_pass1",
    )(p4, bdb, avec)
    out = pl.pallas_call(
        functools.partial(_s5_pass2_kernel, ti=ti),
        grid=grid,
        in_specs=[u_spec, f_spec, blk(SSM_BLK_IN, 2 * SSM_BLK_ST), blk(4, SSM_BLK_ST),
                  blk(2 * SSM_BLK_ST, SSM_BLK_IN), blk(SSM_BLK_IN, 2 * SSM_BLK_IN),
                  blk(2, 2 * SSM_BLK_IN)],
        out_specs=o_spec,
        out_shape=jax.ShapeDtypeStruct((bsz, N_SUBSEQ, lsub, width), BF16),
        scratch_shapes=scratch,
        compiler_params=_cparams(sem),
        name="s5_pass2",
    )(p4, fin, bdb, avec, bdc, glu, vec)
    return out.reshape(bsz, seqlen, width)


def _layer_norm_rows(y, g, b):
    mu = jnp.mean(y, axis=-1, keepdims=True)
    yc = y - mu
    var = jnp.mean(yc * yc, axis=-1, keepdims=True)
    return yc * lax.rsqrt(var + EPS) * g + b


def _pack_bf16_pairs(lo, hi):
    lo_bits = lax.bitcast_convert_type(lo.astype(BF16).astype(F32), jnp.uint32)
    hi_bits = lax.bitcast_convert_type(hi.astype(BF16).astype(F32), jnp.uint32)
    return (lo_bits >> 16) | (hi_bits & jnp.uint32(0xFFFF0000))


def _unpack_bf16_pairs(w):
    lo = lax.bitcast_convert_type(w << 16, F32)
    hi = lax.bitcast_convert_type(w & jnp.uint32(0xFFFF0000), F32)
    return lo, hi


def _outproj_kernel(attn_ref, ssm_ref, gs_ref, w_ref, x_ref, mod_ref, ln_ref, wr_ref, br_ref,
                    x1_ref, hp_ref, idx_ref, gate_ref, rank_ref, cnt_ref, mix_a, mix_b, *, nc):
    s = pl.program_id(0)

    @pl.when(s == 0)
    def _():
        mix_b[...] = jnp.zeros_like(mix_b)

    def step(mix_w, mix_r):
        ssm = ssm_ref[...].astype(F32)
        ms = jnp.mean(ssm * ssm, axis=-1, keepdims=True)
        ssm_n = (ssm * lax.rsqrt(ms + EPS) * gs_ref[...]).astype(BF16)
        attn = attn_ref[...]
        ka = attn.shape[1]
        for n0 in range(0, mix_w.shape[1], nc):
            mix_w[:, n0:n0 + nc] = (jnp.dot(attn, w_ref[:ka, n0:n0 + nc], preferred_element_type=F32)
                                    + jnp.dot(ssm_n, w_ref[ka:, n0:n0 + nc], preferred_element_type=F32))
        _outproj_epilogue(mix_r, x_ref, mod_ref, ln_ref, wr_ref, br_ref,
                          x1_ref, hp_ref, idx_ref, gate_ref, rank_ref, cnt_ref)

    @pl.when(s % 2 == 0)
    def _():
        step(mix_a, mix_b)

    @pl.when(s % 2 == 1)
    def _():
        step(mix_b, mix_a)


def _outproj_epilogue(mix_ref, x_ref, mod_ref, ln_ref, wr_ref, br_ref,
                      x1_ref, hp_ref, idx_ref, gate_ref, rank_ref, cnt_ref):
    m = mod_ref[...]
    d = x_ref.shape[1]
    y = DN_ALPHA * x_ref[...] + (1.0 + m[2:3, :]) * mix_ref[...]
    x1 = _layer_norm_rows(y, ln_ref[0:1, :], ln_ref[1:2, :])
    x1_ref[...] = x1
    h2 = x1 * (1.0 + m[4:5, :]) + m[3:4, :]
    hp_ref[...] = _pack_bf16_pairs(h2[:, :d // 2], h2[:, d // 2:])
    hi = h2.astype(BF16)
    lo = (h2 - hi.astype(F32)).astype(BF16)
    tm = h2.shape[0]
    r = jnp.dot(jnp.concatenate([hi, lo], axis=0), wr_ref[...], preferred_element_type=F32)
    logits = r[:tm, :N_EXPERTS] + r[:tm, N_EXPERTS:] + r[tm:, :N_EXPERTS] + br_ref[...]
    lane = lax.broadcasted_iota(jnp.int32, logits.shape, 1)
    vals, idxs = [], []
    for _ in range(TOP_K):
        mx = jnp.max(logits, axis=-1, keepdims=True)
        ix = jnp.min(jnp.where(logits == mx, lane, N_EXPERTS), axis=-1, keepdims=True)
        vals.append(mx)
        idxs.append(ix)
        logits = jnp.where(lane == ix, -jnp.inf, logits)
    tv = jnp.concatenate(vals, axis=1)
    e = jnp.exp(tv - vals[0])
    gate_ref[...] = e / jnp.sum(e, axis=-1, keepdims=True)
    idx_ref[...] = jnp.concatenate(idxs, axis=1)
    tri = (lax.broadcasted_iota(jnp.int32, (tm, tm), 0) > lax.broadcasted_iota(jnp.int32, (tm, tm), 1)).astype(BF16)
    run = jnp.zeros((1, N_EXPERTS), F32)
    ranks = []
    for ix in idxs:
        onehot = (lane == ix).astype(F32)
        before = jnp.dot(tri, onehot.astype(BF16), preferred_element_type=F32) + run
        ranks.append(jnp.sum(onehot * before, axis=-1, keepdims=True))
        run = run + jnp.sum(onehot, axis=0, keepdims=True)
    rank_ref[...] = jnp.concatenate(ranks, axis=1).astype(jnp.int32)
    cnt_ref[...] = jnp.zeros_like(cnt_ref)
    cnt_ref[0:1, 0:N_EXPERTS] = run


def _out_proj(attn_n, ssm, g_ssm, w_out_bf, x, mod3, ln1, wr, br, tm=ROUTE_TILE):
    bsz, seqlen, d = x.shape
    n_tok = bsz * seqlen
    ka = attn_n.shape[-1]
    ks = w_out_bf.shape[0] - ka
    nt = n_tok // tm
    per_b = seqlen // tm
    mm = lambda s: (jnp.minimum(s, nt - 1), 0)
    ep = lambda s: (jnp.maximum(s - 1, 0), 0)
    const = lambda s: (0, 0)
    outs = pl.pallas_call(
        functools.partial(_outproj_kernel, nc=512),
        grid=(nt + 1,),
        in_specs=[pl.BlockSpec((tm, ka), mm),
                  pl.BlockSpec((tm, ks), mm),
                  pl.BlockSpec((1, ks), const),
                  _resident((ka + ks, d), const),
                  pl.BlockSpec((tm, d), ep),
                  pl.BlockSpec((None, 6, d), lambda s: (jnp.maximum(s - 1, 0) // per_b, 0, 0)),
                  pl.BlockSpec((2, d), const),
                  pl.BlockSpec((d, 2 * N_EXPERTS), const),
                  pl.BlockSpec((1, N_EXPERTS), const)],
        out_specs=[pl.BlockSpec((tm, d), ep),
                   pl.BlockSpec((tm, d // 2), ep),
                   pl.BlockSpec((tm, TOP_K), ep),
                   pl.BlockSpec((tm, TOP_K), ep),
                   pl.BlockSpec((tm, TOP_K), ep),
                   pl.BlockSpec((None, SUBLANES, LANES), lambda s: (jnp.maximum(s - 1, 0), 0, 0))],
        out_shape=[jax.ShapeDtypeStruct((n_tok, d), F32),
                   jax.ShapeDtypeStruct((n_tok, d // 2), jnp.uint32),
                   jax.ShapeDtypeStruct((n_tok, TOP_K), jnp.int32),
                   jax.ShapeDtypeStruct((n_tok, TOP_K), F32),
                   jax.ShapeDtypeStruct((n_tok, TOP_K), jnp.int32),
                   jax.ShapeDtypeStruct((nt, SUBLANES, LANES), F32)],
        scratch_shapes=[pltpu.VMEM((tm, d), F32), pltpu.VMEM((tm, d), F32)],
        compiler_params=_cparams(("arbitrary",)),
        name="out_proj_ln_router",
    )(attn_n.reshape(n_tok, ka), ssm.reshape(n_tok, ks), g_ssm.reshape(1, ks), w_out_bf,
      x.reshape(n_tok, d), mod3, ln1, wr, br)
    x1, hp = outs[0].reshape(bsz, seqlen, d), outs[1]
    return (x1, hp) + tuple(outs[2:])


def _stream_expert_weights(blk_e, first, nxt, last, w_hbm, stage, wbf, sem):
    ct = pl.program_id(0)
    rb = pl.program_id(1)
    width = wbf[0].shape[1]

    def copies(e, col_tile):
        c0 = pl.multiple_of(col_tile * width, width)
        return [pltpu.make_async_copy(w.at[e, :, pl.ds(c0, width)], s, sem.at[j])
                for j, (w, s) in enumerate(zip(w_hbm, stage))]

    @pl.when((ct == 0) & (rb == 0))
    def _():
        for cp in copies(blk_e[0], 0):
            cp.start(priority=1)

    @pl.when(first[rb] == 1)
    def _():
        for cp in copies(blk_e[rb], ct):
            cp.wait()
        def convert(c, carry):
            r0 = pl.multiple_of(c * CAST_ROWS, CAST_ROWS)
            for s, w in zip(stage, wbf):
                w[pl.ds(r0, CAST_ROWS), :] = s[pl.ds(r0, CAST_ROWS), :].astype(BF16)
            return carry
        lax.fori_loop(0, stage[0].shape[0] // CAST_ROWS, convert, 0)

        @pl.when(last[rb] == 0)
        def _():
            for cp in copies(nxt[rb], ct):
                cp.start(priority=1)

        @pl.when((last[rb] == 1) & (ct + 1 < pl.num_programs(0)))
        def _():
            for cp in copies(nxt[rb], ct + 1):
                cp.start(priority=1)


def _expert_up_kernel(blk_e, first, nreal, nxt, last, xs_ref, wg_hbm, wu_hbm, bg_ref, bu_ref, act_ref,
                      stg_g, stg_u, wgb, wub, sem):
    rb = pl.program_id(1)
    _stream_expert_weights(blk_e, first, nxt, last, (wg_hbm, wu_hbm), (stg_g, stg_u), (wgb, wub), sem)

    @pl.when(rb < nreal[0])
    def _():
        x = jnp.concatenate(_unpack_bf16_pairs(xs_ref[...]), axis=1).astype(BF16)
        g = jnp.dot(x, wgb[...], preferred_element_type=F32) + bg_ref[...]
        up = jnp.dot(x, wub[...], preferred_element_type=F32) + bu_ref[...]
        g = jnp.minimum(g, SWIGLU_LIMIT)
        up = jnp.clip(up, -SWIGLU_LIMIT, SWIGLU_LIMIT)
        act_ref[...] = (g * jax.nn.sigmoid(SWIGLU_ALPHA * g) * (up + 1.0)).astype(BF16)

    @pl.when(rb >= nreal[0])
    def _():
        act_ref[...] = jnp.zeros_like(act_ref)


def _expert_down_kernel(blk_e, first, nreal, nxt, last, act_ref, wd_hbm, bd_ref, y_ref, stg, wdb, sem):
    rb = pl.program_id(1)
    _stream_expert_weights(blk_e, first, nxt, last, (wd_hbm,), (stg,), (wdb,), sem)

    @pl.when(rb < nreal[0])
    def _():
        y = jnp.dot(act_ref[...], wdb[...], preferred_element_type=F32) + bd_ref[...]
        half = y.shape[1] // 2
        y_ref[...] = _pack_bf16_pairs(y[:, :half], y[:, half:])

    @pl.when(rb >= nreal[0])
    def _():
        y_ref[...] = jnp.zeros_like(y_ref)


def _experts(xs, sched, w_gate, b_gate, w_up, b_up, w_down, b_down, tf=512, tn=DOWN_TILE):
    cap = xs.shape[0]
    n_e, d, dff = w_gate.shape
    nblk = cap // EXPERT_BLK
    hbm = pl.BlockSpec(memory_space=pl.ANY)
    act = pl.pallas_call(
        _expert_up_kernel,
        grid_spec=pltpu.PrefetchScalarGridSpec(
            num_scalar_prefetch=5,
            grid=(dff // tf, nblk),
            in_specs=[pl.BlockSpec((EXPERT_BLK, d // 2), lambda f, r, be, *_: (r, 0)),
                      hbm, hbm,
                      pl.BlockSpec((None, 1, tf), lambda f, r, be, *_: (be[r], 0, f)),
                      pl.BlockSpec((None, 1, tf), lambda f, r, be, *_: (be[r], 0, f))],
            out_specs=pl.BlockSpec((EXPERT_BLK, tf), lambda f, r, be, *_: (r, f)),
            scratch_shapes=[pltpu.VMEM((d, tf), F32), pltpu.VMEM((d, tf), F32),
                            pltpu.VMEM((d, tf), BF16), pltpu.VMEM((d, tf), BF16),
                            pltpu.SemaphoreType.DMA((2,))]),
        out_shape=jax.ShapeDtypeStruct((cap, dff), BF16),
        compiler_params=_cparams(("arbitrary", "arbitrary")),
        name="expert_gate_up",
    )(*sched, xs, w_gate, w_up, b_gate.reshape(n_e, 1, dff), b_up.reshape(n_e, 1, dff))
    ys = pl.pallas_call(
        _expert_down_kernel,
        grid_spec=pltpu.PrefetchScalarGridSpec(
            num_scalar_prefetch=5,
            grid=(d // tn, nblk),
            in_specs=[pl.BlockSpec((EXPERT_BLK, dff), lambda n, r, be, *_: (r, 0)),
                      hbm,
                      pl.BlockSpec((None, 1, tn), lambda n, r, be, *_: (be[r], 0, n))],
            out_specs=pl.BlockSpec((EXPERT_BLK, tn // 2), lambda n, r, be, *_: (r, n)),
            scratch_shapes=[pltpu.VMEM((dff, tn), F32), pltpu.VMEM((dff, tn), BF16),
                            pltpu.SemaphoreType.DMA((1,))]),
        out_shape=jax.ShapeDtypeStruct((cap, d // 2), jnp.uint32),
        compiler_params=_cparams(("arbitrary", "arbitrary")),
        name="expert_down",
    )(*sched, act, w_down, b_down.reshape(n_e, 1, d))
    return ys


def _route_tables(cnt, top_idx, rank):
    n_assign = top_idx.size
    counts = cnt[:, 0, :N_EXPERTS].astype(jnp.int32)
    tot = jnp.sum(counts, axis=0)
    padded = (tot + EXPERT_BLK - 1) // EXPERT_BLK * EXPERT_BLK
    pend = jnp.cumsum(padded)
    pstart = pend - padded
    base = pstart[None, :] + jnp.cumsum(counts, axis=0) - counts
    experts = jnp.arange(N_EXPERTS, dtype=jnp.int32)
    idx_t = top_idx.reshape(-1, ROUTE_TILE * TOP_K)
    dest = jnp.sum(jnp.where(idx_t[..., None] == experts, base[:, None, :], 0), axis=-1)
    dest = (dest.reshape(-1) + rank.reshape(-1)).astype(jnp.int32)
    cap = ((n_assign + EXPERT_BLK - 1) // EXPERT_BLK) * EXPERT_BLK + N_EXPERTS * EXPERT_BLK
    nblk = cap // EXPERT_BLK
    blk = jnp.arange(nblk, dtype=jnp.int32)
    blk_e = jnp.sum((pend[None, :] <= blk[:, None] * EXPERT_BLK).astype(jnp.int32), axis=1)
    blk_e = jnp.minimum(blk_e, N_EXPERTS - 1).astype(jnp.int32)
    nreal = (pend[-1:] // EXPERT_BLK).astype(jnp.int32)
    real = blk < nreal[0]
    first = real & jnp.concatenate([jnp.ones((1,), bool), blk_e[1:] != blk_e[:-1]])
    starts = jnp.where(first, blk, nblk)
    nxt_blk = jnp.concatenate([lax.cummin(starts, reverse=True)[1:], jnp.full((1,), nblk, jnp.int32)])
    last = nxt_blk >= nblk
    nxt = jnp.where(last, blk_e[0], blk_e[jnp.minimum(nxt_blk, nblk - 1)]).astype(jnp.int32)
    sched = (blk_e, first.astype(jnp.int32), nreal, nxt, last.astype(jnp.int32))
    return dest, cap, sched, (pstart + tot).astype(jnp.int32), (padded - tot).astype(jnp.int32)


def _row_copy(src, src_row, dst, dst_row, sem):
    return pltpu.make_async_copy(src.at[pl.ds(src_row, 1)], dst.at[pl.ds(dst_row, 1)], sem)


def _dispatch_kernel(pad0_ref, padn_ref, nreal_ref, hp_ref, dest_ref, xs_ref, zblk, sem, zsem, bsem):
    i = pl.program_id(0)
    tm = hp_ref.shape[0]
    nblk = xs_ref.shape[0] // EXPERT_BLK

    def for_each_pad_row(fn):
        def per_expert(e, c):
            def per_row(r, c2):
                fn(_row_copy(zblk, 0, xs_ref, pad0_ref[e] + r, zsem))
                return c2
            return lax.fori_loop(0, padn_ref[e], per_row, c)
        lax.fori_loop(0, N_EXPERTS, per_expert, 0)

    def for_each_unused_block(fn):
        def per_block(b, c):
            r0 = pl.multiple_of(b * EXPERT_BLK, EXPERT_BLK)
            fn(pltpu.make_async_copy(zblk, xs_ref.at[pl.ds(r0, EXPERT_BLK)], bsem))
            return c
        lax.fori_loop(nreal_ref[0], nblk, per_block, 0)

    @pl.when(i == 0)
    def _():
        zblk[...] = jnp.zeros_like(zblk)
        for_each_pad_row(lambda cp: cp.start())
        for_each_unused_block(lambda cp: cp.start())

    def issue(r, c):
        for k in range(TOP_K):
            _row_copy(hp_ref, r, xs_ref, dest_ref[r * TOP_K + k], sem).start(priority=k % 2)
        return c
    lax.fori_loop(0, tm, issue, 0, unroll=2)

    def drain(r, c):
        for k in range(TOP_K):
            _row_copy(hp_ref, 0, xs_ref, 0, sem).wait()
        return c
    lax.fori_loop(0, tm, drain, 0)

    @pl.when(i == 0)
    def _():
        for_each_pad_row(lambda cp: cp.wait())
        for_each_unused_block(lambda cp: cp.wait())


def _dispatch(hp, dest, pad0, padn, nreal, cap, tm=MOVE_TILE):
    n_tok, half = hp.shape
    return pl.pallas_call(
        _dispatch_kernel,
        grid_spec=pltpu.PrefetchScalarGridSpec(
            num_scalar_prefetch=3,
            grid=(n_tok // tm,),
            in_specs=[pl.BlockSpec((tm, half), lambda i, *_: (i, 0)),
                      pl.BlockSpec((tm * TOP_K,), lambda i, *_: (i,), memory_space=pltpu.SMEM)],
            out_specs=pl.BlockSpec(memory_space=pl.ANY),
            scratch_shapes=[pltpu.VMEM((EXPERT_BLK, half), jnp.uint32),
                            pltpu.SemaphoreType.DMA, pltpu.SemaphoreType.DMA, pltpu.SemaphoreType.DMA]),
        out_shape=jax.ShapeDtypeStruct((cap, half), jnp.uint32),
        compiler_params=_cparams(("arbitrary",)),
        name="moe_dispatch",
    )(pad0, padn, nreal, hp, dest)


def _combine_kernel(dest_ref, gate_ref, x1_ref, mod_ref, ln_ref, ys_ref, o_ref, buf, sem, *, tn):
    tm = x1_ref.shape[0]

    def issue(r, c):
        for k in range(TOP_K):
            pltpu.make_async_copy(ys_ref.at[pl.ds(dest_ref[r * TOP_K + k], 1)],
                                  buf.at[k, pl.ds(r, 1)], sem).start(priority=k % 2)
        return c
    lax.fori_loop(0, tm, issue, 0, unroll=2)

    def drain(r, c):
        for k in range(TOP_K):
            pltpu.make_async_copy(ys_ref.at[pl.ds(0, 1)], buf.at[0, pl.ds(0, 1)], sem).wait()
        return c
    lax.fori_loop(0, tm, drain, 0)

    g = gate_ref[...]
    m = mod_ref[...]
    d = x1_ref.shape[1]
    hw = tn // 2
    cw = 512
    for w0 in range(0, d // 2, cw):
        lo = hi = None
        for k in range(TOP_K):
            wl, wh = _unpack_bf16_pairs(buf[k, :, w0:w0 + cw])
            lo = g[:, k:k + 1] * wl if lo is None else lo + g[:, k:k + 1] * wl
            hi = g[:, k:k + 1] * wh if hi is None else hi + g[:, k:k + 1] * wh
        for half, moe in ((0, lo), (1, hi)):
            c0 = (w0 // hw) * tn + half * hw + w0 % hw
            o_ref[:, c0:c0 + cw] = DN_ALPHA * x1_ref[:, c0:c0 + cw] + (1.0 + m[5:6, c0:c0 + cw]) * moe
    o_ref[...] = _layer_norm_rows(o_ref[...], ln_ref[0:1, :], ln_ref[1:2, :])


def _combine(ys, dest, gates, x1, mod3, ln2, tm=MOVE_TILE, tn=DOWN_TILE):
    bsz, seqlen, d = x1.shape
    n_tok = bsz * seqlen
    per_b = seqlen // tm
    out = pl.pallas_call(
        functools.partial(_combine_kernel, tn=tn),
        grid=(n_tok // tm,),
        in_specs=[pl.BlockSpec((tm * TOP_K,), lambda i: (i,), memory_space=pltpu.SMEM),
                  pl.BlockSpec((tm, TOP_K), lambda i: (i, 0)),
                  pl.BlockSpec((tm, d), lambda i: (i, 0)),
                  pl.BlockSpec((None, 6, d), lambda i: (i // per_b, 0, 0)),
                  pl.BlockSpec((2, d), lambda i: (0, 0)),
                  pl.BlockSpec(memory_space=pl.ANY)],
        out_specs=pl.BlockSpec((tm, d), lambda i: (i, 0)),
        out_shape=jax.ShapeDtypeStruct((n_tok, d), F32),
        scratch_shapes=[pltpu.VMEM((TOP_K, tm, d // 2), jnp.uint32), pltpu.SemaphoreType.DMA],
        compiler_params=_cparams(("arbitrary",)),
        name="moe_combine_ln",
    )(dest, gates.reshape(n_tok, TOP_K), x1.reshape(n_tok, d), mod3, ln2, ys)
    return out.reshape(bsz, seqlen, d)


def kernel(x, c, positions, w_ada, b_ada, w_in, attn_sinks, ssm_a_re, ssm_a_im, ssm_b_re, ssm_b_im,
           ssm_c_re, ssm_c_im, ssm_d, ssm_log_dt, ssm_w_glu, ssm_b_glu, g_attn_out, g_ssm_out, w_out,
           ln1_g, ln1_b, w_router, b_router, w_gate, b_gate, w_up, b_up, w_down, b_down, ln2_g, ln2_b):
    bsz, seqlen, d = x.shape
    lsub = seqlen // N_SUBSEQ
    n_tok = bsz * seqlen
    rope_tab = _rope_tables(positions)
    for l in range(w_ada.shape[0]):
        mod3 = _ada_mod(c, w_ada[l], b_ada[l]).reshape(bsz, 6, d)
        proj = _in_proj(x, mod3, w_in[l].astype(BF16))
        attn_n = _attention(proj, rope_tab, attn_sinks[l].astype(F32), g_attn_out[l].astype(F32))
        s5p = _s5_params(ssm_a_re[l], ssm_a_im[l], ssm_b_re[l], ssm_b_im[l], ssm_c_re[l], ssm_c_im[l],
                         ssm_d[l], ssm_log_dt[l], ssm_w_glu[l], ssm_b_glu[l], lsub)
        ssm = _s5(proj, s5p)
        wr_hi = w_router[l].astype(BF16)
        wr_lo = (w_router[l] - wr_hi.astype(F32)).astype(BF16)
        x1, hp, top_idx, gates, rank, cnt = _out_proj(
            attn_n, ssm, g_ssm_out[l].astype(F32), w_out[l].astype(BF16), x, mod3,
            jnp.stack([ln1_g[l], ln1_b[l]]).astype(F32),
            jnp.concatenate([wr_hi, wr_lo], axis=1), b_router[l].reshape(1, N_EXPERTS).astype(F32))
        dest, cap, sched, pad0, padn = _route_tables(cnt, top_idx, rank)
        xs = _dispatch(hp.reshape(n_tok, d // 2), dest, pad0, padn, sched[2], cap)
        ys = _experts(xs, sched, w_gate[l], b_gate[l], w_up[l], b_up[l], w_down[l], b_down[l])
        x = _combine(ys, dest, gates, x1, mod3, jnp.stack([ln2_g[l], ln2_b[l]]).astype(F32))
    return x
```

```python
import functools
import math

import jax
import jax.numpy as jnp
from jax import lax
from jax.experimental import pallas as pl
from jax.experimental.pallas import tpu as pltpu

F32 = jnp.float32
BF16 = jnp.bfloat16

HEAD_DIM = 64
N_Q_HEADS = 32
N_KV_HEADS = 4
GQ = N_Q_HEADS // N_KV_HEADS
ATTN_WIDTH = N_Q_HEADS * HEAD_DIM
KV_WIDTH = N_KV_HEADS * HEAD_DIM
QKV_WIDTH = ATTN_WIDTH + 2 * KV_WIDTH
BLK = 128
ROT_DIM = HEAD_DIM // 4
ROPE_THETA = 500000.0
GROUP_CH = 16
STATE = 64
N_EXPERTS = 32
TOP_K = 4
SWIGLU_LIMIT = 7.0
SWIGLU_ALPHA = 1.702
EXPERT_BLK = 256
ROUTE_TILE = 1024
MOVE_TILE = 256
DOWN_TILE = 4096
S5_STEPS = 512
CAST_ROWS = 128
DEPTH = 1
DN_ALPHA = (2.0 * DEPTH) ** 0.25
EPS = 1e-5

LANES = 128
SUBLANES = 8
N_SUBSEQ = SUBLANES
GROUPS_PER_BLK = 16
SSM_BLK_IN = GROUPS_PER_BLK * GROUP_CH
SSM_BLK_ST = GROUPS_PER_BLK * STATE
VMEM_LIMIT = 56 * 1024 * 1024


def _cparams(sem, vmem=VMEM_LIMIT):
    return pltpu.CompilerParams(dimension_semantics=sem, vmem_limit_bytes=vmem)


def _resident(shape, index_map):
    return pl.BlockSpec(shape, index_map, pipeline_mode=pl.Buffered(1))


def _ada_kernel(c_ref, w_ref, b_ref, o_ref):
    c = c_ref[...]
    ca = c * jax.nn.sigmoid(c)
    o_ref[...] = jnp.dot(ca.astype(BF16), w_ref[...].astype(BF16),
                         preferred_element_type=F32) + b_ref[...]


def _ada_mod(c, w_ada, b_ada, tn=512):
    bsz, d = c.shape
    n = w_ada.shape[1]
    c8 = jnp.zeros((SUBLANES, d), F32).at[:bsz].set(c)
    out = pl.pallas_call(
        _ada_kernel,
        grid=(n // tn,),
        in_specs=[pl.BlockSpec((SUBLANES, d), lambda j: (0, 0)),
                  pl.BlockSpec((d, tn), lambda j: (0, j)),
                  pl.BlockSpec((1, tn), lambda j: (0, j))],
        out_specs=pl.BlockSpec((SUBLANES, tn), lambda j: (0, j)),
        out_shape=jax.ShapeDtypeStruct((SUBLANES, n), F32),
        compiler_params=_cparams(("arbitrary",)),
        name="ada_mod",
    )(c8, w_ada, b_ada.reshape(1, n))
    return out[:bsz]


def _rope(t, tab):
    c, s_lo, s_hi = tab[:, :LANES], tab[:, LANES:2 * LANES], tab[:, 2 * LANES:]
    half = ROT_DIM // 2
    out = []
    for j in range(t.shape[1] // LANES):
        tj = t[:, j * LANES:(j + 1) * LANES]
        out.append(tj * c + pltpu.roll(tj, LANES - half, 1) * s_lo + pltpu.roll(tj, half, 1) * s_hi)
    return jnp.concatenate(out, axis=1)


def _inproj_kernel(x_ref, mod_ref, tab_ref, w_ref, o_ref, *, nc):
    m = mod_ref[...]
    h = (x_ref[...] * (1.0 + m[1:2, :]) + m[0:1, :]).astype(BF16)
    for n0 in range(0, o_ref.shape[-1], nc):
        p = jnp.dot(h, w_ref[:, n0:n0 + nc], preferred_element_type=F32)
        if n0 < ATTN_WIDTH:
            p = _rope(p * (HEAD_DIM ** -0.5), tab_ref[...])
        elif n0 == ATTN_WIDTH:
            p = jnp.concatenate([_rope(p[:, :KV_WIDTH], tab_ref[...]), p[:, KV_WIDTH:]], axis=1)
        o_ref[:, n0:n0 + nc] = p.astype(BF16)


def _in_proj(x, mod3, rope_tab, w_in_bf, tm=128, nc=512):
    bsz, seqlen, d = x.shape
    n_in = w_in_bf.shape[1]
    assert ATTN_WIDTH % nc == 0 and nc >= 2 * KV_WIDTH
    return pl.pallas_call(
        functools.partial(_inproj_kernel, nc=nc),
        grid=(bsz, seqlen // tm),
        in_specs=[pl.BlockSpec((None, tm, d), lambda b, i: (b, i, 0)),
                  pl.BlockSpec((None, 6, d), lambda b, i: (b, 0, 0)),
                  pl.BlockSpec((None, tm, 3 * LANES), lambda b, i: (b, i, 0)),
                  _resident((d, n_in), lambda b, i: (0, 0))],
        out_specs=pl.BlockSpec((None, tm, n_in), lambda b, i: (b, i, 0)),
        out_shape=jax.ShapeDtypeStruct((bsz, seqlen, n_in), BF16),
        compiler_params=_cparams(("arbitrary", "arbitrary")),
        name="in_proj",
    )(x, mod3, rope_tab, w_in_bf)


def _attn_kernel(sink_ref, q_ref, kc_ref, kp_ref, vc_ref, vp_ref, g_ref, o_ref):
    n = pl.program_id(1)
    low = lax.broadcasted_iota(jnp.int32, (2 * BLK, LANES), 1) < HEAD_DIM
    low_q = lax.broadcasted_iota(jnp.int32, (BLK, LANES), 1) < HEAD_DIM

    k_raw = jnp.concatenate([kp_ref[...], kc_ref[...]], axis=0).astype(F32)
    v_raw = jnp.concatenate([vp_ref[...], vc_ref[...]], axis=0).astype(F32)

    qi = lax.broadcasted_iota(jnp.int32, (BLK, BLK), 0)
    kj = lax.broadcasted_iota(jnp.int32, (BLK, BLK), 1)
    own = kj <= qi
    prev_ok = kj >= jnp.where(n > 0, 0, BLK)

    o_chunks = []
    for hk in range(N_KV_HEADS):
        kc = k_raw[:, (hk // 2) * LANES:(hk // 2 + 1) * LANES]
        vc = v_raw[:, (hk // 2) * LANES:(hk // 2 + 1) * LANES]
        k_sw = pltpu.roll(kc, HEAD_DIM, 1)
        v_sw = pltpu.roll(vc, HEAD_DIM, 1)
        if hk % 2 == 0:
            kk2 = jnp.where(low, kc, k_sw)
            v_lo = jnp.where(low, vc, 0.0)
            v_hi = jnp.where(low, 0.0, v_sw)
        else:
            kk2 = jnp.where(low, k_sw, kc)
            v_lo = jnp.where(low, v_sw, 0.0)
            v_hi = jnp.where(low, 0.0, vc)
        kk2 = kk2.astype(BF16)
        v_lo = v_lo.astype(BF16)
        v_hi = v_hi.astype(BF16)
        lhs = []
        for j in range(GQ // 2):
            c0 = (hk * (GQ // 2) + j) * LANES
            q2 = q_ref[:, c0:c0 + LANES].astype(F32)
            lhs.append(jnp.where(low_q, q2, 0.0).astype(BF16))
            lhs.append(jnp.where(low_q, 0.0, q2).astype(BF16))
        s_all = lax.dot_general(jnp.concatenate(lhs, axis=0), kk2,
                                (((1,), (1,)), ((), ())), preferred_element_type=F32)
        for j in range(GQ // 2):
            acc = None
            for side, vv in ((0, v_lo), (1, v_hi)):
                i = 2 * j + side
                s_prev = jnp.where(prev_ok, s_all[i * BLK:(i + 1) * BLK, :BLK], -1e30)
                s = jnp.where(own, s_all[i * BLK:(i + 1) * BLK, BLK:], s_prev)
                sink = sink_ref[hk * GQ + i]
                m = jnp.maximum(jnp.max(s, axis=-1, keepdims=True), sink)
                p = jnp.exp(s - m)
                denom = jnp.sum(p, axis=-1, keepdims=True) + jnp.exp(sink - m)
                p = p * (1.0 / denom)
                p = jnp.concatenate([jnp.where(own, 0.0, p), jnp.where(own, p, 0.0)], axis=1).astype(BF16)
                o = jnp.dot(p, vv, preferred_element_type=F32)
                acc = o if acc is None else acc + o
            o_chunks.append(acc)

    ssq = None
    for oc in o_chunks:
        t = jnp.sum(oc * oc, axis=-1, keepdims=True)
        ssq = t if ssq is None else ssq + t
    inv = lax.rsqrt(ssq * (1.0 / ATTN_WIDTH) + EPS)
    for j, oc in enumerate(o_chunks):
        o_ref[:, j * LANES:(j + 1) * LANES] = (oc * inv * g_ref[:, j * LANES:(j + 1) * LANES]).astype(BF16)


def _attention(qkv, sinks, g_attn):
    bsz, seqlen, _ = qkv.shape
    nb = seqlen // BLK
    kcol = ATTN_WIDTH // KV_WIDTH
    cur = lambda b, n: (b, n, 0)
    return pl.pallas_call(
        _attn_kernel,
        grid=(bsz, nb),
        in_specs=[pl.BlockSpec(memory_space=pltpu.SMEM),
                  pl.BlockSpec((None, BLK, ATTN_WIDTH), cur),
                  pl.BlockSpec((None, BLK, KV_WIDTH), lambda b, n: (b, n, kcol)),
                  pl.BlockSpec((None, BLK, KV_WIDTH), lambda b, n: (b, jnp.maximum(n - 1, 0), kcol)),
                  pl.BlockSpec((None, BLK, KV_WIDTH), lambda b, n: (b, n, kcol + 1)),
                  pl.BlockSpec((None, BLK, KV_WIDTH), lambda b, n: (b, jnp.maximum(n - 1, 0), kcol + 1)),
                  pl.BlockSpec((1, ATTN_WIDTH), lambda b, n: (0, 0))],
        out_specs=pl.BlockSpec((None, BLK, ATTN_WIDTH), cur),
        out_shape=jax.ShapeDtypeStruct((bsz, seqlen, ATTN_WIDTH), BF16),
        compiler_params=_cparams(("arbitrary", "arbitrary")),
        name="swa_attention",
    )(sinks, qkv, qkv, qkv, qkv, qkv, g_attn.reshape(1, ATTN_WIDTH))


def _rope_tables(positions):
    half = ROT_DIM // 2
    inv_freq = ROPE_THETA ** (-jnp.arange(0, ROT_DIM, 2, dtype=F32) / ROT_DIM)
    ang = positions.astype(F32)[..., None] * inv_freq
    cos, sin = jnp.cos(ang), jnp.sin(ang)
    shp = cos.shape[:-1] + (HEAD_DIM - ROT_DIM,)
    c = jnp.concatenate([cos, cos, jnp.ones(shp, F32)], axis=-1)
    z8 = jnp.zeros_like(sin)
    s_lo = jnp.concatenate([-sin, z8, jnp.zeros(shp, F32)], axis=-1)
    s_hi = jnp.concatenate([z8, sin, jnp.zeros(shp, F32)], axis=-1)
    rep = LANES // HEAD_DIM
    return jnp.concatenate([jnp.tile(c, rep), jnp.tile(s_lo, rep), jnp.tile(s_hi, rep)], axis=-1)


def _s5_scan(buf, ar, ai, hr, hi, ti, store):
    def step(i, carry):
        hr, hi = carry
        r0 = pl.multiple_of(i * SUBLANES, SUBLANES)
        row = buf[pl.ds(r0, SUBLANES), :]
        nhr = ar * hr - ai * hi + row[:, :SSM_BLK_ST]
        nhi = ar * hi + ai * hr + row[:, SSM_BLK_ST:]
        if store:
            buf[pl.ds(r0, SUBLANES), :] = jnp.concatenate([nhr, nhi], axis=1)
        return nhr, nhi
    return lax.fori_loop(0, ti, step, (hr, hi), unroll=4)


def _time_major(u_ref, ti):
    u = pltpu.einshape("jid->ijd", u_ref[...].astype(F32))
    return u.reshape(ti * N_SUBSEQ, u.shape[-1])


def _s5_pass1_kernel(u_ref, bdb_ref, a_ref, f_ref, buf, hst, *, ti):
    ic = pl.program_id(2)

    @pl.when(ic == 0)
    def _():
        hst[...] = jnp.zeros_like(hst)

    u = _time_major(u_ref, ti)
    buf[...] = jnp.dot(u.astype(BF16), bdb_ref[...], preferred_element_type=F32)
    ar = jnp.broadcast_to(a_ref[0:1, :], (SUBLANES, SSM_BLK_ST))
    ai = jnp.broadcast_to(a_ref[1:2, :], (SUBLANES, SSM_BLK_ST))
    hr, hi = _s5_scan(buf, ar, ai, hst[:, :SSM_BLK_ST], hst[:, SSM_BLK_ST:], ti, store=False)
    hst[...] = jnp.concatenate([hr, hi], axis=1)

    @pl.when(ic == pl.num_programs(2) - 1)
    def _():
        f_ref[...] = hst[...]


def _s5_pass2_kernel(u_ref, f_ref, bdb_ref, a_ref, bdc_ref, glu_ref, vec_ref, o_ref, buf, hst, *, ti):
    ic = pl.program_id(2)

    @pl.when(ic == 0)
    def _():
        fr, fi = f_ref[:, :SSM_BLK_ST], f_ref[:, SSM_BLK_ST:]
        pr = jnp.broadcast_to(a_ref[2:3, :], (SUBLANES, SSM_BLK_ST))
        pi = jnp.broadcast_to(a_ref[3:4, :], (SUBLANES, SSM_BLK_ST))
        row = lax.broadcasted_iota(jnp.int32, (SUBLANES, SSM_BLK_ST), 0)
        hr = jnp.zeros((SUBLANES, SSM_BLK_ST), F32)
        hi = jnp.zeros((SUBLANES, SSM_BLK_ST), F32)
        for _ in range(N_SUBSEQ - 1):
            nr = pr * hr - pi * hi + fr
            ni = pr * hi + pi * hr + fi
            hr = jnp.where(row == 0, 0.0, pltpu.roll(nr, 1, 0))
            hi = jnp.where(row == 0, 0.0, pltpu.roll(ni, 1, 0))
        hst[...] = jnp.concatenate([hr, hi], axis=1)

    u = _time_major(u_ref, ti)
    buf[...] = jnp.dot(u.astype(BF16), bdb_ref[...], preferred_element_type=F32)
    ar = jnp.broadcast_to(a_ref[0:1, :], (SUBLANES, SSM_BLK_ST))
    ai = jnp.broadcast_to(a_ref[1:2, :], (SUBLANES, SSM_BLK_ST))
    hr, hi = _s5_scan(buf, ar, ai, hst[:, :SSM_BLK_ST], hst[:, SSM_BLK_ST:], ti, store=True)
    hst[...] = jnp.concatenate([hr, hi], axis=1)

    y = jnp.dot(buf[...].astype(BF16), bdc_ref[...], preferred_element_type=F32)
    y = jax.nn.gelu(y + vec_ref[0:1, :SSM_BLK_IN] * u)
    z = jnp.dot(y.astype(BF16), glu_ref[...], preferred_element_type=F32) + vec_ref[1:2, :]
    out = z[:, :SSM_BLK_IN] * jax.nn.sigmoid(z[:, SSM_BLK_IN:])
    out = pltpu.einshape("ijd->jid", out.reshape(ti, N_SUBSEQ, SSM_BLK_IN))
    o_ref[...] = out.astype(BF16)


def _s5_params(a_re, a_im, b_re, b_im, c_re, c_im, d_skip, log_dt, w_glu, b_glu, lsub):
    g = a_re.shape[0]
    nf = g // GROUPS_PER_BLK
    a = lax.complex(a_re.astype(F32), a_im.astype(F32))
    dt = jnp.exp(log_dt.astype(F32))[:, None]
    a_bar = jnp.exp(a * dt)
    a_pow = jnp.exp(a * dt * lsub)
    b_bar = ((a_bar - 1.0) / a)[..., None] * lax.complex(b_re.astype(F32), b_im.astype(F32))
    eye = jnp.eye(GROUPS_PER_BLK, dtype=F32)

    def bd_in(m):
        m = m.reshape(nf, GROUPS_PER_BLK, STATE, GROUP_CH)
        return jnp.einsum('fgpc,gh->fgchp', m, eye).reshape(nf, SSM_BLK_IN, SSM_BLK_ST)

    def bd_out(m):
        m = m.reshape(nf, GROUPS_PER_BLK, GROUP_CH, STATE)
        return jnp.einsum('fgcp,gh->fgphc', m, eye).reshape(nf, SSM_BLK_ST, SSM_BLK_IN)

    def bd_glu(m):
        m = m.reshape(nf, GROUPS_PER_BLK, GROUP_CH, GROUP_CH)
        return jnp.einsum('fgcd,gh->fgchd', m, eye).reshape(nf, SSM_BLK_IN, SSM_BLK_IN)

    bdb = jnp.concatenate([bd_in(jnp.real(b_bar)), bd_in(jnp.imag(b_bar))], axis=2).astype(BF16)
    bdc = jnp.concatenate([bd_out(c_re.astype(F32)), bd_out(-c_im.astype(F32))], axis=1).astype(BF16)
    wg = w_glu.astype(F32)
    glu = jnp.concatenate([bd_glu(wg[..., :GROUP_CH]), bd_glu(wg[..., GROUP_CH:])], axis=2).astype(BF16)
    flat = lambda m: m.reshape(nf, 1, SSM_BLK_ST)
    avec = jnp.concatenate([flat(jnp.real(a_bar)), flat(jnp.imag(a_bar)),
                            flat(jnp.real(a_pow)), flat(jnp.imag(a_pow))], axis=1)
    bg = b_glu.astype(F32).reshape(nf, GROUPS_PER_BLK, 2 * GROUP_CH)
    bvec = jnp.concatenate([bg[..., :GROUP_CH].reshape(nf, 1, SSM_BLK_IN),
                            bg[..., GROUP_CH:].reshape(nf, 1, SSM_BLK_IN)], axis=2)
    dvec = jnp.concatenate([d_skip.astype(F32).reshape(nf, 1, SSM_BLK_IN),
                            jnp.zeros((nf, 1, SSM_BLK_IN), F32)], axis=2)
    vec = jnp.concatenate([dvec, bvec], axis=1)
    return bdb, bdc, glu, avec, vec


def _s5(proj, params, ti=S5_STEPS):
    bdb, bdc, glu, avec, vec = params
    bsz, seqlen, n_in = proj.shape
    width = n_in - QKV_WIDTH
    nf = width // SSM_BLK_IN
    lsub = seqlen // N_SUBSEQ
    ti = min(ti, lsub)
    u_col0 = QKV_WIDTH // SSM_BLK_IN
    p4 = proj.reshape(bsz, N_SUBSEQ, lsub, n_in)
    grid = (bsz, nf, lsub // ti)
    u_spec = pl.BlockSpec((None, N_SUBSEQ, ti, SSM_BLK_IN), lambda b, f, i: (b, 0, i, u_col0 + f))
    o_spec = pl.BlockSpec((None, N_SUBSEQ, ti, SSM_BLK_IN), lambda b, f, i: (b, 0, i, f))
    blk = lambda r, c: pl.BlockSpec((None, r, c), lambda b, f, i: (f, 0, 0))
    f_spec = pl.BlockSpec((None, None, N_SUBSEQ, 2 * SSM_BLK_ST), lambda b, f, i: (b, f, 0, 0))
    scratch = [pltpu.VMEM((ti * SUBLANES, 2 * SSM_BLK_ST), F32),
               pltpu.VMEM((SUBLANES, 2 * SSM_BLK_ST), F32)]
    sem = ("arbitrary", "arbitrary", "arbitrary")
    fin = pl.pallas_call(
        functools.partial(_s5_pass1_kernel, ti=ti),
        grid=grid,
        in_specs=[u_spec, blk(SSM_BLK_IN, 2 * SSM_BLK_ST), blk(4, SSM_BLK_ST)],
        out_specs=f_spec,
        out_shape=jax.ShapeDtypeStruct((bsz, nf, N_SUBSEQ, 2 * SSM_BLK_ST), F32),
        scratch_shapes=scratch,
        compiler_params=_cparams(sem),
        name="s5_pass1",
    )(p4, bdb, avec)
    out = pl.pallas_call(
        functools.partial(_s5_pass2_kernel, ti=ti),
        grid=grid,
        in_specs=[u_spec, f_spec, blk(SSM_BLK_IN, 2 * SSM_BLK_ST), blk(4, SSM_BLK_ST),
                  blk(2 * SSM_BLK_ST, SSM_BLK_IN), blk(SSM_BLK_IN, 2 * SSM_BLK_IN),
                  blk(2, 2 * SSM_BLK_IN)],
        out_specs=o_spec,
        out_shape=jax.ShapeDtypeStruct((bsz, N_SUBSEQ, lsub, width), BF16),
        scratch_shapes=scratch,
        compiler_params=_cparams(sem),
        name="s5_pass2",
    )(p4, fin, bdb, avec, bdc, glu, vec)
    return out.reshape(bsz, seqlen, width)


def _layer_norm_rows(y, g, b):
    mu = jnp.mean(y, axis=-1, keepdims=True)
    yc = y - mu
    var = jnp.mean(yc * yc, axis=-1, keepdims=True)
    return yc * lax.rsqrt(var + EPS) * g + b


def _pack_bf16_pairs(lo, hi):
    lo_bits = lax.bitcast_convert_type(lo.astype(BF16).astype(F32), jnp.uint32)
    hi_bits = lax.bitcast_convert_type(hi.astype(BF16).astype(F32), jnp.uint32)
    return (lo_bits >> 16) | (hi_bits & jnp.uint32(0xFFFF0000))


def _unpack_bf16_pairs(w):
    lo = lax.bitcast_convert_type(w << 16, F32)
    hi = lax.bitcast_convert_type(w & jnp.uint32(0xFFFF0000), F32)
    return lo, hi


def _outproj_kernel(attn_ref, ssm_ref, gs_ref, w_ref, x_ref, mod_ref, ln_ref, wr_ref, br_ref,
                    x1_ref, hp_ref, logit_ref, ybuf, *, nc):
    m = mod_ref[...]
    ssm = ssm_ref[...].astype(F32)
    ms = jnp.mean(ssm * ssm, axis=-1, keepdims=True)
    ssm_n = (ssm * lax.rsqrt(ms + EPS) * gs_ref[...]).astype(BF16)
    attn = attn_ref[...]
    ka = attn.shape[1]
    d = x_ref.shape[1]
    for n0 in range(0, d, nc):
        mix = (jnp.dot(attn, w_ref[:ka, n0:n0 + nc], preferred_element_type=F32)
               + jnp.dot(ssm_n, w_ref[ka:, n0:n0 + nc], preferred_element_type=F32))
        ybuf[:, n0:n0 + nc] = DN_ALPHA * x_ref[:, n0:n0 + nc] + (1.0 + m[2:3, n0:n0 + nc]) * mix
    x1 = _layer_norm_rows(ybuf[...], ln_ref[0:1, :], ln_ref[1:2, :])
    x1_ref[...] = x1
    h2 = x1 * (1.0 + m[4:5, :]) + m[3:4, :]
    hp_ref[...] = _pack_bf16_pairs(h2[:, :d // 2], h2[:, d // 2:])
    hi = h2.astype(BF16)
    lo = (h2 - hi.astype(F32)).astype(BF16)
    tm = h2.shape[0]
    r = jnp.dot(jnp.concatenate([hi, lo], axis=0), wr_ref[...], preferred_element_type=F32)
    logit_ref[...] = r[:tm, :N_EXPERTS] + r[:tm, N_EXPERTS:] + r[tm:, :N_EXPERTS] + br_ref[...]


def _route_kernel(logit_ref, idx_ref, gate_ref, rank_ref, cnt_ref):
    logits = logit_ref[...]
    tm = logits.shape[0]
    lane = lax.broadcasted_iota(jnp.int32, logits.shape, 1)
    vals, idxs = [], []
    for _ in range(TOP_K):
        mx = jnp.max(logits, axis=-1, keepdims=True)
        ix = jnp.min(jnp.where(logits == mx, lane, N_EXPERTS), axis=-1, keepdims=True)
        vals.append(mx)
        idxs.append(ix)
        logits = jnp.where(lane == ix, -jnp.inf, logits)
    tv = jnp.concatenate(vals, axis=1)
    e = jnp.exp(tv - vals[0])
    gate_ref[...] = e / jnp.sum(e, axis=-1, keepdims=True)
    idx_ref[...] = jnp.concatenate(idxs, axis=1)
    tri = (lax.broadcasted_iota(jnp.int32, (tm, tm), 0) > lax.broadcasted_iota(jnp.int32, (tm, tm), 1)).astype(BF16)
    run = jnp.zeros((1, N_EXPERTS), F32)
    ranks = []
    for ix in idxs:
        onehot = (lane == ix).astype(F32)
        before = jnp.dot(tri, onehot.astype(BF16), preferred_element_type=F32) + run
        ranks.append(jnp.sum(onehot * before, axis=-1, keepdims=True))
        run = run + jnp.sum(onehot, axis=0, keepdims=True)
    rank_ref[...] = jnp.concatenate(ranks, axis=1).astype(jnp.int32)
    cnt_ref[...] = jnp.zeros_like(cnt_ref)
    cnt_ref[0:1, 0:N_EXPERTS] = run


def _out_proj(attn_n, ssm, g_ssm, w_out_bf, x, mod3, ln1, wr, br, tm=128):
    bsz, seqlen, d = x.shape
    ka = attn_n.shape[-1]
    ks = w_out_bf.shape[0] - ka
    row = lambda b, i: (b, i, 0)
    const = lambda b, i: (0, 0)
    return pl.pallas_call(
        functools.partial(_outproj_kernel, nc=512),
        grid=(bsz, seqlen // tm),
        in_specs=[pl.BlockSpec((None, tm, ka), row),
                  pl.BlockSpec((None, tm, ks), row),
                  pl.BlockSpec((1, ks), const),
                  _resident((ka + ks, d), const),
                  pl.BlockSpec((None, tm, d), row),
                  pl.BlockSpec((None, 6, d), lambda b, i: (b, 0, 0)),
                  pl.BlockSpec((2, d), const),
                  pl.BlockSpec((d, 2 * N_EXPERTS), const),
                  pl.BlockSpec((1, N_EXPERTS), const)],
        out_specs=[pl.BlockSpec((None, tm, d), row),
                   pl.BlockSpec((None, tm, d // 2), row),
                   pl.BlockSpec((None, tm, N_EXPERTS), row)],
        out_shape=[jax.ShapeDtypeStruct((bsz, seqlen, d), F32),
                   jax.ShapeDtypeStruct((bsz, seqlen, d // 2), jnp.uint32),
                   jax.ShapeDtypeStruct((bsz, seqlen, N_EXPERTS), F32)],
        scratch_shapes=[pltpu.VMEM((tm, d), F32)],
        compiler_params=_cparams(("arbitrary", "arbitrary")),
        name="out_proj_ln_router",
    )(attn_n, ssm, g_ssm.reshape(1, ks), w_out_bf, x, mod3, ln1, wr, br)


def _route(logits, tm=ROUTE_TILE):
    n_tok = logits.shape[0]
    nt = n_tok // tm
    row = lambda i: (i, 0)
    k_shape = lambda dt: jax.ShapeDtypeStruct((n_tok, TOP_K), dt)
    return pl.pallas_call(
        _route_kernel,
        grid=(nt,),
        in_specs=[pl.BlockSpec((tm, N_EXPERTS), row)],
        out_specs=[pl.BlockSpec((tm, TOP_K), row), pl.BlockSpec((tm, TOP_K), row), pl.BlockSpec((tm, TOP_K), row),
                   pl.BlockSpec((None, SUBLANES, LANES), lambda i: (i, 0, 0))],
        out_shape=[k_shape(jnp.int32), k_shape(F32), k_shape(jnp.int32),
                   jax.ShapeDtypeStruct((nt, SUBLANES, LANES), F32)],
        compiler_params=_cparams(("arbitrary",)),
        name="route_topk",
    )(logits)


def _stream_expert_weights(blk_e, first, nxt, last, w_hbm, stage, wbf, sem):
    ct = pl.program_id(0)
    rb = pl.program_id(1)
    width = wbf[0].shape[1]

    def copies(e, col_tile):
        c0 = pl.multiple_of(col_tile * width, width)
        return [pltpu.make_async_copy(w.at[e, :, pl.ds(c0, width)], s, sem.at[j])
                for j, (w, s) in enumerate(zip(w_hbm, stage))]

    @pl.when((ct == 0) & (rb == 0))
    def _():
        for cp in copies(blk_e[0], 0):
            cp.start(priority=1)

    @pl.when(first[rb] == 1)
    def _():
        for cp in copies(blk_e[rb], ct):
            cp.wait()
        def convert(c, carry):
            r0 = pl.multiple_of(c * CAST_ROWS, CAST_ROWS)
            for s, w in zip(stage, wbf):
                w[pl.ds(r0, CAST_ROWS), :] = s[pl.ds(r0, CAST_ROWS), :].astype(BF16)
            return carry
        lax.fori_loop(0, stage[0].shape[0] // CAST_ROWS, convert, 0)

        @pl.when(last[rb] == 0)
        def _():
            for cp in copies(nxt[rb], ct):
                cp.start(priority=1)

        @pl.when((last[rb] == 1) & (ct + 1 < pl.num_programs(0)))
        def _():
            for cp in copies(nxt[rb], ct + 1):
                cp.start(priority=1)


def _expert_up_kernel(blk_e, first, nreal, nxt, last, xs_ref, wg_hbm, wu_hbm, bg_ref, bu_ref, act_ref,
                      stg_g, stg_u, wgb, wub, sem):
    rb = pl.program_id(1)
    _stream_expert_weights(blk_e, first, nxt, last, (wg_hbm, wu_hbm), (stg_g, stg_u), (wgb, wub), sem)

    @pl.when(rb < nreal[0])
    def _():
        x = jnp.concatenate(_unpack_bf16_pairs(xs_ref[...]), axis=1).astype(BF16)
        g = jnp.dot(x, wgb[...], preferred_element_type=F32) + bg_ref[...]
        up = jnp.dot(x, wub[...], preferred_element_type=F32) + bu_ref[...]
        g = jnp.minimum(g, SWIGLU_LIMIT)
        up = jnp.clip(up, -SWIGLU_LIMIT, SWIGLU_LIMIT)
        act_ref[...] = (g * jax.nn.sigmoid(SWIGLU_ALPHA * g) * (up + 1.0)).astype(BF16)

    @pl.when(rb >= nreal[0])
    def _():
        act_ref[...] = jnp.zeros_like(act_ref)


def _expert_down_kernel(blk_e, first, nreal, nxt, last, act_ref, wd_hbm, bd_ref, y_ref, stg, wdb, sem):
    rb = pl.program_id(1)
    _stream_expert_weights(blk_e, first, nxt, last, (wd_hbm,), (stg,), (wdb,), sem)

    @pl.when(rb < nreal[0])
    def _():
        y = jnp.dot(act_ref[...], wdb[...], preferred_element_type=F32) + bd_ref[...]
        half = y.shape[1] // 2
        y_ref[...] = _pack_bf16_pairs(y[:, :half], y[:, half:])

    @pl.when(rb >= nreal[0])
    def _():
        y_ref[...] = jnp.zeros_like(y_ref)


def _experts(xs, sched, w_gate, b_gate, w_up, b_up, w_down, b_down, tf=512, tn=DOWN_TILE):
    cap = xs.shape[0]
    n_e, d, dff = w_gate.shape
    nblk = cap // EXPERT_BLK
    hbm = pl.BlockSpec(memory_space=pl.ANY)
    act = pl.pallas_call(
        _expert_up_kernel,
        grid_spec=pltpu.PrefetchScalarGridSpec(
            num_scalar_prefetch=5,
            grid=(dff // tf, nblk),
            in_specs=[pl.BlockSpec((EXPERT_BLK, d // 2), lambda f, r, be, *_: (r, 0)),
                      hbm, hbm,
                      pl.BlockSpec((None, 1, tf), lambda f, r, be, *_: (be[r], 0, f)),
                      pl.BlockSpec((None, 1, tf), lambda f, r, be, *_: (be[r], 0, f))],
            out_specs=pl.BlockSpec((EXPERT_BLK, tf), lambda f, r, be, *_: (r, f)),
            scratch_shapes=[pltpu.VMEM((d, tf), F32), pltpu.VMEM((d, tf), F32),
                            pltpu.VMEM((d, tf), BF16), pltpu.VMEM((d, tf), BF16),
                            pltpu.SemaphoreType.DMA((2,))]),
        out_shape=jax.ShapeDtypeStruct((cap, dff), BF16),
        compiler_params=_cparams(("arbitrary", "arbitrary")),
        name="expert_gate_up",
    )(*sched, xs, w_gate, w_up, b_gate.reshape(n_e, 1, dff), b_up.reshape(n_e, 1, dff))
    ys = pl.pallas_call(
        _expert_down_kernel,
        grid_spec=pltpu.PrefetchScalarGridSpec(
            num_scalar_prefetch=5,
            grid=(d // tn, nblk),
            in_specs=[pl.BlockSpec((EXPERT_BLK, dff), lambda n, r, be, *_: (r, 0)),
                      hbm,
                      pl.BlockSpec((None, 1, tn), lambda n, r, be, *_: (be[r], 0, n))],
            out_specs=pl.BlockSpec((EXPERT_BLK, tn // 2), lambda n, r, be, *_: (r, n)),
            scratch_shapes=[pltpu.VMEM((dff, tn), F32), pltpu.VMEM((dff, tn), BF16),
                            pltpu.SemaphoreType.DMA((1,))]),
        out_shape=jax.ShapeDtypeStruct((cap, d // 2), jnp.uint32),
        compiler_params=_cparams(("arbitrary", "arbitrary")),
        name="expert_down",
    )(*sched, act, w_down, b_down.reshape(n_e, 1, d))
    return ys


def _route_tables(cnt, top_idx, rank):
    n_assign = top_idx.size
    counts = cnt[:, 0, :N_EXPERTS].astype(jnp.int32)
    tot = jnp.sum(counts, axis=0)
    padded = (tot + EXPERT_BLK - 1) // EXPERT_BLK * EXPERT_BLK
    pend = jnp.cumsum(padded)
    pstart = pend - padded
    base = pstart[None, :] + jnp.cumsum(counts, axis=0) - counts
    experts = jnp.arange(N_EXPERTS, dtype=jnp.int32)
    idx_t = top_idx.reshape(-1, ROUTE_TILE * TOP_K)
    dest = jnp.sum(jnp.where(idx_t[..., None] == experts, base[:, None, :], 0), axis=-1)
    dest = (dest.reshape(-1) + rank.reshape(-1)).astype(jnp.int32)
    cap = ((n_assign + EXPERT_BLK - 1) // EXPERT_BLK) * EXPERT_BLK + N_EXPERTS * EXPERT_BLK
    nblk = cap // EXPERT_BLK
    blk = jnp.arange(nblk, dtype=jnp.int32)
    blk_e = jnp.sum((pend[None, :] <= blk[:, None] * EXPERT_BLK).astype(jnp.int32), axis=1)
    blk_e = jnp.minimum(blk_e, N_EXPERTS - 1).astype(jnp.int32)
    nreal = (pend[-1:] // EXPERT_BLK).astype(jnp.int32)
    real = blk < nreal[0]
    first = real & jnp.concatenate([jnp.ones((1,), bool), blk_e[1:] != blk_e[:-1]])
    starts = jnp.where(first, blk, nblk)
    nxt_blk = jnp.concatenate([lax.cummin(starts, reverse=True)[1:], jnp.full((1,), nblk, jnp.int32)])
    last = nxt_blk >= nblk
    nxt = jnp.where(last, blk_e[0], blk_e[jnp.minimum(nxt_blk, nblk - 1)]).astype(jnp.int32)
    sched = (blk_e, first.astype(jnp.int32), nreal, nxt, last.astype(jnp.int32))
    return dest, cap, sched, (pstart + tot).astype(jnp.int32), (padded - tot).astype(jnp.int32)


def _row_copy(src, src_row, dst, dst_row, sem):
    return pltpu.make_async_copy(src.at[pl.ds(src_row, 1)], dst.at[pl.ds(dst_row, 1)], sem)


def _dispatch_kernel(pad0_ref, padn_ref, nreal_ref, hp_ref, dest_ref, xs_ref, zblk, sem, zsem, bsem):
    i = pl.program_id(0)
    tm = hp_ref.shape[0]
    nblk = xs_ref.shape[0] // EXPERT_BLK

    def for_each_pad_row(fn):
        def per_expert(e, c):
            def per_row(r, c2):
                fn(_row_copy(zblk, 0, xs_ref, pad0_ref[e] + r, zsem))
                return c2
            return lax.fori_loop(0, padn_ref[e], per_row, c)
        lax.fori_loop(0, N_EXPERTS, per_expert, 0)

    def for_each_unused_block(fn):
        def per_block(b, c):
            r0 = pl.multiple_of(b * EXPERT_BLK, EXPERT_BLK)
            fn(pltpu.make_async_copy(zblk, xs_ref.at[pl.ds(r0, EXPERT_BLK)], bsem))
            return c
        lax.fori_loop(nreal_ref[0], nblk, per_block, 0)

    @pl.when(i == 0)
    def _():
        zblk[...] = jnp.zeros_like(zblk)
        for_each_pad_row(lambda cp: cp.start())
        for_each_unused_block(lambda cp: cp.start())

    def issue(r, c):
        for k in range(TOP_K):
            _row_copy(hp_ref, r, xs_ref, dest_ref[r * TOP_K + k], sem).start(priority=k % 2)
        return c
    lax.fori_loop(0, tm, issue, 0, unroll=2)

    def drain(r, c):
        for k in range(TOP_K):
            _row_copy(hp_ref, 0, xs_ref, 0, sem).wait()
        return c
    lax.fori_loop(0, tm, drain, 0)

    @pl.when(i == 0)
    def _():
        for_each_pad_row(lambda cp: cp.wait())
        for_each_unused_block(lambda cp: cp.wait())


def _dispatch(hp, dest, pad0, padn, nreal, cap, tm=MOVE_TILE):
    n_tok, half = hp.shape
    return pl.pallas_call(
        _dispatch_kernel,
        grid_spec=pltpu.PrefetchScalarGridSpec(
            num_scalar_prefetch=3,
            grid=(n_tok // tm,),
            in_specs=[pl.BlockSpec((tm, half), lambda i, *_: (i, 0)),
                      pl.BlockSpec((tm * TOP_K,), lambda i, *_: (i,), memory_space=pltpu.SMEM)],
            out_specs=pl.BlockSpec(memory_space=pl.ANY),
            scratch_shapes=[pltpu.VMEM((EXPERT_BLK, half), jnp.uint32),
                            pltpu.SemaphoreType.DMA, pltpu.SemaphoreType.DMA, pltpu.SemaphoreType.DMA]),
        out_shape=jax.ShapeDtypeStruct((cap, half), jnp.uint32),
        compiler_params=_cparams(("arbitrary",)),
        name="moe_dispatch",
    )(pad0, padn, nreal, hp, dest)


def _combine_kernel(dest_ref, gate_ref, x1_ref, mod_ref, ln_ref, ys_ref, o_ref, buf, sem, *, tn):
    tm = x1_ref.shape[0]

    def issue(r, c):
        for k in range(TOP_K):
            pltpu.make_async_copy(ys_ref.at[pl.ds(dest_ref[r * TOP_K + k], 1)],
                                  buf.at[k, pl.ds(r, 1)], sem).start(priority=k % 2)
        return c
    lax.fori_loop(0, tm, issue, 0, unroll=2)

    def drain(r, c):
        for k in range(TOP_K):
            pltpu.make_async_copy(ys_ref.at[pl.ds(0, 1)], buf.at[0, pl.ds(0, 1)], sem).wait()
        return c
    lax.fori_loop(0, tm, drain, 0)

    g = gate_ref[...]
    m = mod_ref[...]
    d = x1_ref.shape[1]
    hw = tn // 2
    cw = 512
    for w0 in range(0, d // 2, cw):
        lo = hi = None
        for k in range(TOP_K):
            wl, wh = _unpack_bf16_pairs(buf[k, :, w0:w0 + cw])
            lo = g[:, k:k + 1] * wl if lo is None else lo + g[:, k:k + 1] * wl
            hi = g[:, k:k + 1] * wh if hi is None else hi + g[:, k:k + 1] * wh
        for half, moe in ((0, lo), (1, hi)):
            c0 = (w0 // hw) * tn + half * hw + w0 % hw
            o_ref[:, c0:c0 + cw] = DN_ALPHA * x1_ref[:, c0:c0 + cw] + (1.0 + m[5:6, c0:c0 + cw]) * moe
    o_ref[...] = _layer_norm_rows(o_ref[...], ln_ref[0:1, :], ln_ref[1:2, :])


def _combine(ys, dest, gates, x1, mod3, ln2, tm=MOVE_TILE, tn=DOWN_TILE):
    bsz, seqlen, d = x1.shape
    n_tok = bsz * seqlen
    per_b = seqlen // tm
    out = pl.pallas_call(
        functools.partial(_combine_kernel, tn=tn),
        grid=(n_tok // tm,),
        in_specs=[pl.BlockSpec((tm * TOP_K,), lambda i: (i,), memory_space=pltpu.SMEM),
                  pl.BlockSpec((tm, TOP_K), lambda i: (i, 0)),
                  pl.BlockSpec((tm, d), lambda i: (i, 0)),
                  pl.BlockSpec((None, 6, d), lambda i: (i // per_b, 0, 0)),
                  pl.BlockSpec((2, d), lambda i: (0, 0)),
                  pl.BlockSpec(memory_space=pl.ANY)],
        out_specs=pl.BlockSpec((tm, d), lambda i: (i, 0)),
        out_shape=jax.ShapeDtypeStruct((n_tok, d), F32),
        scratch_shapes=[pltpu.VMEM((TOP_K, tm, d // 2), jnp.uint32), pltpu.SemaphoreType.DMA],
        compiler_params=_cparams(("arbitrary",)),
        name="moe_combine_ln",
    )(dest, gates.reshape(n_tok, TOP_K), x1.reshape(n_tok, d), mod3, ln2, ys)
    return out.reshape(bsz, seqlen, d)


def kernel(x, c, positions, w_ada, b_ada, w_in, attn_sinks, ssm_a_re, ssm_a_im, ssm_b_re, ssm_b_im,
           ssm_c_re, ssm_c_im, ssm_d, ssm_log_dt, ssm_w_glu, ssm_b_glu, g_attn_out, g_ssm_out, w_out,
           ln1_g, ln1_b, w_router, b_router, w_gate, b_gate, w_up, b_up, w_down, b_down, ln2_g, ln2_b):
    bsz, seqlen, d = x.shape
    lsub = seqlen // N_SUBSEQ
    n_tok = bsz * seqlen
    rope_tab = _rope_tables(positions)
    for l in range(w_ada.shape[0]):
        mod3 = _ada_mod(c, w_ada[l], b_ada[l]).reshape(bsz, 6, d)
        proj = _in_proj(x, mod3, rope_tab, w_in[l].astype(BF16))
        attn_n = _attention(proj, attn_sinks[l].astype(F32), g_attn_out[l].astype(F32))
        s5p = _s5_params(ssm_a_re[l], ssm_a_im[l], ssm_b_re[l], ssm_b_im[l], ssm_c_re[l], ssm_c_im[l],
                         ssm_d[l], ssm_log_dt[l], ssm_w_glu[l], ssm_b_glu[l], lsub)
        ssm = _s5(proj, s5p)
        wr_hi = w_router[l].astype(BF16)
        wr_lo = (w_router[l] - wr_hi.astype(F32)).astype(BF16)
        x1, hp, logits = _out_proj(
            attn_n, ssm, g_ssm_out[l].astype(F32), w_out[l].astype(BF16), x, mod3,
            jnp.stack([ln1_g[l], ln1_b[l]]).astype(F32),
            jnp.concatenate([wr_hi, wr_lo], axis=1), b_router[l].reshape(1, N_EXPERTS).astype(F32))
        top_idx, gates, rank, cnt = _route(logits.reshape(n_tok, N_EXPERTS))
        dest, cap, sched, pad0, padn = _route_tables(cnt, top_idx, rank)
        xs = _dispatch(hp.reshape(n_tok, d // 2), dest, pad0, padn, sched[2], cap)
        ys = _experts(xs, sched, w_gate[l], b_gate[l], w_up[l], b_up[l], w_down[l], b_down[l])
        x = _combine(ys, dest, gates, x1, mod3, jnp.stack([ln2_g[l], ln2_b[l]]).astype(F32))
    return x
```

```python
import functools
import math

import jax
import jax.numpy as jnp
from jax import lax
from jax.experimental import pallas as pl
from jax.experimental.pallas import tpu as pltpu

F32 = jnp.float32
BF16 = jnp.bfloat16

HEAD_DIM = 64
N_Q_HEADS = 32
N_KV_HEADS = 4
GQ = N_Q_HEADS // N_KV_HEADS
ATTN_WIDTH = N_Q_HEADS * HEAD_DIM
KV_WIDTH = N_KV_HEADS * HEAD_DIM
QKV_WIDTH = ATTN_WIDTH + 2 * KV_WIDTH
BLK = 128
ROT_DIM = HEAD_DIM // 4
ROPE_THETA = 500000.0
GROUP_CH = 16
STATE = 64
N_EXPERTS = 32
TOP_K = 4
SWIGLU_LIMIT = 7.0
SWIGLU_ALPHA = 1.702
EXPERT_BLK = 256
ROUTE_TILE = 256
GATHER_CHUNK = 8
MOVE_TILE = 256
DOWN_TILE = 4096
S5_STEPS = 512
CAST_ROWS = 128
DEPTH = 1
DN_ALPHA = (2.0 * DEPTH) ** 0.25
EPS = 1e-5

LANES = 128
SUBLANES = 8
N_SUBSEQ = SUBLANES
GROUPS_PER_BLK = 16
SSM_BLK_IN = GROUPS_PER_BLK * GROUP_CH
SSM_BLK_ST = GROUPS_PER_BLK * STATE
VMEM_LIMIT = 56 * 1024 * 1024


def _cparams(sem, vmem=VMEM_LIMIT):
    return pltpu.CompilerParams(dimension_semantics=sem, vmem_limit_bytes=vmem)


def _resident(shape, index_map):
    return pl.BlockSpec(shape, index_map, pipeline_mode=pl.Buffered(1))


def _ada_kernel(c_ref, w_ref, b_ref, o_ref):
    c = c_ref[...]
    ca = c * jax.nn.sigmoid(c)
    o_ref[...] = jnp.dot(ca.astype(BF16), w_ref[...].astype(BF16),
                         preferred_element_type=F32) + b_ref[...]


def _ada_mod(c, w_ada, b_ada, tn=1024):
    bsz, d = c.shape
    n = w_ada.shape[1]
    c8 = jnp.zeros((SUBLANES, d), F32).at[:bsz].set(c)
    out = pl.pallas_call(
        _ada_kernel,
        grid=(n // tn,),
        in_specs=[pl.BlockSpec((SUBLANES, d), lambda j: (0, 0)),
                  pl.BlockSpec((d, tn), lambda j: (0, j)),
                  pl.BlockSpec((1, tn), lambda j: (0, j))],
        out_specs=pl.BlockSpec((SUBLANES, tn), lambda j: (0, j)),
        out_shape=jax.ShapeDtypeStruct((SUBLANES, n), F32),
        compiler_params=_cparams(("arbitrary",)),
        name="ada_mod",
    )(c8, w_ada, b_ada.reshape(1, n))
    return out[:bsz]


def _rope(t, tab):
    c, s_lo, s_hi = tab[:, :LANES], tab[:, LANES:2 * LANES], tab[:, 2 * LANES:]
    half = ROT_DIM // 2
    out = []
    for j in range(t.shape[1] // LANES):
        tj = t[:, j * LANES:(j + 1) * LANES]
        out.append(tj * c + pltpu.roll(tj, LANES - half, 1) * s_lo + pltpu.roll(tj, half, 1) * s_hi)
    return jnp.concatenate(out, axis=1)


def _inproj_kernel(x_ref, mod_ref, tab_ref, w_ref, o_ref, *, nc):
    m = mod_ref[...]
    h = (x_ref[...] * (1.0 + m[1:2, :]) + m[0:1, :]).astype(BF16)
    for n0 in range(0, o_ref.shape[-1], nc):
        p = jnp.dot(h, w_ref[:, n0:n0 + nc], preferred_element_type=F32)
        if n0 < ATTN_WIDTH:
            p = _rope(p * (HEAD_DIM ** -0.5), tab_ref[...])
        elif n0 == ATTN_WIDTH:
            p = jnp.concatenate([_rope(p[:, :KV_WIDTH], tab_ref[...]), p[:, KV_WIDTH:]], axis=1)
        o_ref[:, n0:n0 + nc] = p.astype(BF16)


def _in_proj(x, mod3, rope_tab, w_in_bf, tm=128, nc=512):
    bsz, seqlen, d = x.shape
    n_in = w_in_bf.shape[1]
    assert ATTN_WIDTH % nc == 0 and nc >= 2 * KV_WIDTH
    return pl.pallas_call(
        functools.partial(_inproj_kernel, nc=nc),
        grid=(bsz, seqlen // tm),
        in_specs=[pl.BlockSpec((None, tm, d), lambda b, i: (b, i, 0)),
                  pl.BlockSpec((None, 6, d), lambda b, i: (b, 0, 0)),
                  pl.BlockSpec((None, tm, 3 * LANES), lambda b, i: (b, i, 0)),
                  _resident((d, n_in), lambda b, i: (0, 0))],
        out_specs=pl.BlockSpec((None, tm, n_in), lambda b, i: (b, i, 0)),
        out_shape=jax.ShapeDtypeStruct((bsz, seqlen, n_in), BF16),
        compiler_params=_cparams(("arbitrary", "arbitrary")),
        name="in_proj",
    )(x, mod3, rope_tab, w_in_bf)


def _attn_kernel(sink_ref, q_ref, kc_ref, kp_ref, vc_ref, vp_ref, g_ref, o_ref):
    n = pl.program_id(1)
    low = lax.broadcasted_iota(jnp.int32, (2 * BLK, LANES), 1) < HEAD_DIM
    low_q = lax.broadcasted_iota(jnp.int32, (BLK, LANES), 1) < HEAD_DIM

    k_raw = jnp.concatenate([kp_ref[...], kc_ref[...]], axis=0).astype(F32)
    v_raw = jnp.concatenate([vp_ref[...], vc_ref[...]], axis=0).astype(F32)

    qi = lax.broadcasted_iota(jnp.int32, (BLK, BLK), 0)
    kj = lax.broadcasted_iota(jnp.int32, (BLK, BLK), 1)
    own = kj <= qi
    prev_ok = kj >= jnp.where(n > 0, 0, BLK)

    o_chunks = []
    for hk in range(N_KV_HEADS):
        kc = k_raw[:, (hk // 2) * LANES:(hk // 2 + 1) * LANES]
        vc = v_raw[:, (hk // 2) * LANES:(hk // 2 + 1) * LANES]
        k_sw = pltpu.roll(kc, HEAD_DIM, 1)
        v_sw = pltpu.roll(vc, HEAD_DIM, 1)
        if hk % 2 == 0:
            kk2 = jnp.where(low, kc, k_sw)
            v_lo = jnp.where(low, vc, 0.0)
            v_hi = jnp.where(low, 0.0, v_sw)
        else:
            kk2 = jnp.where(low, k_sw, kc)
            v_lo = jnp.where(low, v_sw, 0.0)
            v_hi = jnp.where(low, 0.0, vc)
        kk2 = kk2.astype(BF16)
        v_lo = v_lo.astype(BF16)
        v_hi = v_hi.astype(BF16)
        lhs = []
        for j in range(GQ // 2):
            c0 = (hk * (GQ // 2) + j) * LANES
            q2 = q_ref[:, c0:c0 + LANES].astype(F32)
            lhs.append(jnp.where(low_q, q2, 0.0).astype(BF16))
            lhs.append(jnp.where(low_q, 0.0, q2).astype(BF16))
        s_all = lax.dot_general(jnp.concatenate(lhs, axis=0), kk2,
                                (((1,), (1,)), ((), ())), preferred_element_type=F32)
        for j in range(GQ // 2):
            acc = None
            for side, vv in ((0, v_lo), (1, v_hi)):
                i = 2 * j + side
                s_prev = jnp.where(prev_ok, s_all[i * BLK:(i + 1) * BLK, :BLK], -1e30)
                s = jnp.where(own, s_all[i * BLK:(i + 1) * BLK, BLK:], s_prev)
                sink = sink_ref[hk * GQ + i]
                m = jnp.maximum(jnp.max(s, axis=-1, keepdims=True), sink)
                p = jnp.exp(s - m)
                denom = jnp.sum(p, axis=-1, keepdims=True) + jnp.exp(sink - m)
                p = p * (1.0 / denom)
                p = jnp.concatenate([jnp.where(own, 0.0, p), jnp.where(own, p, 0.0)], axis=1).astype(BF16)
                o = jnp.dot(p, vv, preferred_element_type=F32)
                acc = o if acc is None else acc + o
            o_chunks.append(acc)

    ssq = None
    for oc in o_chunks:
        t = jnp.sum(oc * oc, axis=-1, keepdims=True)
        ssq = t if ssq is None else ssq + t
    inv = lax.rsqrt(ssq * (1.0 / ATTN_WIDTH) + EPS)
    for j, oc in enumerate(o_chunks):
        o_ref[:, j * LANES:(j + 1) * LANES] = (oc * inv * g_ref[:, j * LANES:(j + 1) * LANES]).astype(BF16)


def _attention(qkv, sinks, g_attn):
    bsz, seqlen, _ = qkv.shape
    nb = seqlen // BLK
    kcol = ATTN_WIDTH // KV_WIDTH
    cur = lambda b, n: (b, n, 0)
    return pl.pallas_call(
        _attn_kernel,
        grid=(bsz, nb),
        in_specs=[pl.BlockSpec(memory_space=pltpu.SMEM),
                  pl.BlockSpec((None, BLK, ATTN_WIDTH), cur),
                  pl.BlockSpec((None, BLK, KV_WIDTH), lambda b, n: (b, n, kcol)),
                  pl.BlockSpec((None, BLK, KV_WIDTH), lambda b, n: (b, jnp.maximum(n - 1, 0), kcol)),
                  pl.BlockSpec((None, BLK, KV_WIDTH), lambda b, n: (b, n, kcol + 1)),
                  pl.BlockSpec((None, BLK, KV_WIDTH), lambda b, n: (b, jnp.maximum(n - 1, 0), kcol + 1)),
                  pl.BlockSpec((1, ATTN_WIDTH), lambda b, n: (0, 0))],
        out_specs=pl.BlockSpec((None, BLK, ATTN_WIDTH), cur),
        out_shape=jax.ShapeDtypeStruct((bsz, seqlen, ATTN_WIDTH), BF16),
        compiler_params=_cparams(("arbitrary", "arbitrary")),
        name="swa_attention",
    )(sinks, qkv, qkv, qkv, qkv, qkv, g_attn.reshape(1, ATTN_WIDTH))


def _rope_tables(positions):
    half = ROT_DIM // 2
    inv_freq = ROPE_THETA ** (-jnp.arange(0, ROT_DIM, 2, dtype=F32) / ROT_DIM)
    ang = positions.astype(F32)[..., None] * inv_freq
    cos, sin = jnp.cos(ang), jnp.sin(ang)
    shp = cos.shape[:-1] + (HEAD_DIM - ROT_DIM,)
    c = jnp.concatenate([cos, cos, jnp.ones(shp, F32)], axis=-1)
    z8 = jnp.zeros_like(sin)
    s_lo = jnp.concatenate([-sin, z8, jnp.zeros(shp, F32)], axis=-1)
    s_hi = jnp.concatenate([z8, sin, jnp.zeros(shp, F32)], axis=-1)
    rep = LANES // HEAD_DIM
    return jnp.concatenate([jnp.tile(c, rep), jnp.tile(s_lo, rep), jnp.tile(s_hi, rep)], axis=-1)


def _s5_scan(buf, ar, ai, hr, hi, ti, store):
    def step(i, carry):
        hr, hi = carry
        r0 = pl.multiple_of(i * SUBLANES, SUBLANES)
        row = buf[pl.ds(r0, SUBLANES), :]
        nhr = ar * hr - ai * hi + row[:, :SSM_BLK_ST]
        nhi = ar * hi + ai * hr + row[:, SSM_BLK_ST:]
        if store:
            buf[pl.ds(r0, SUBLANES), :] = jnp.concatenate([nhr, nhi], axis=1)
        return nhr, nhi
    return lax.fori_loop(0, ti, step, (hr, hi), unroll=4)


def _time_major(u_ref, ti):
    u = pltpu.einshape("jid->ijd", u_ref[...].astype(F32))
    return u.reshape(ti * N_SUBSEQ, u.shape[-1])


def _s5_pass1_kernel(u_ref, bdb_ref, a_ref, f_ref, buf, hst, *, ti):
    ic = pl.program_id(2)

    @pl.when(ic == 0)
    def _():
        hst[...] = jnp.zeros_like(hst)

    u = _time_major(u_ref, ti)
    buf[...] = jnp.dot(u.astype(BF16), bdb_ref[...], preferred_element_type=F32)
    ar = jnp.broadcast_to(a_ref[0:1, :], (SUBLANES, SSM_BLK_ST))
    ai = jnp.broadcast_to(a_ref[1:2, :], (SUBLANES, SSM_BLK_ST))
    hr, hi = _s5_scan(buf, ar, ai, hst[:, :SSM_BLK_ST], hst[:, SSM_BLK_ST:], ti, store=False)
    hst[...] = jnp.concatenate([hr, hi], axis=1)

    @pl.when(ic == pl.num_programs(2) - 1)
    def _():
        f_ref[...] = hst[...]


def _s5_pass2_kernel(u_ref, f_ref, bdb_ref, a_ref, bdc_ref, glu_ref, vec_ref, o_ref, buf, hst, *, ti):
    ic = pl.program_id(2)

    @pl.when(ic == 0)
    def _():
        fr, fi = f_ref[:, :SSM_BLK_ST], f_ref[:, SSM_BLK_ST:]
        pr = jnp.broadcast_to(a_ref[2:3, :], (SUBLANES, SSM_BLK_ST))
        pi = jnp.broadcast_to(a_ref[3:4, :], (SUBLANES, SSM_BLK_ST))
        row = lax.broadcasted_iota(jnp.int32, (SUBLANES, SSM_BLK_ST), 0)
        hr = jnp.zeros((SUBLANES, SSM_BLK_ST), F32)
        hi = jnp.zeros((SUBLANES, SSM_BLK_ST), F32)
        for _ in range(N_SUBSEQ - 1):
            nr = pr * hr - pi * hi + fr
            ni = pr * hi + pi * hr + fi
            hr = jnp.where(row == 0, 0.0, pltpu.roll(nr, 1, 0))
            hi = jnp.where(row == 0, 0.0, pltpu.roll(ni, 1, 0))
        hst[...] = jnp.concatenate([hr, hi], axis=1)

    u = _time_major(u_ref, ti)
    buf[...] = jnp.dot(u.astype(BF16), bdb_ref[...], preferred_element_type=F32)
    ar = jnp.broadcast_to(a_ref[0:1, :], (SUBLANES, SSM_BLK_ST))
    ai = jnp.broadcast_to(a_ref[1:2, :], (SUBLANES, SSM_BLK_ST))
    hr, hi = _s5_scan(buf, ar, ai, hst[:, :SSM_BLK_ST], hst[:, SSM_BLK_ST:], ti, store=True)
    hst[...] = jnp.concatenate([hr, hi], axis=1)

    y = jnp.dot(buf[...].astype(BF16), bdc_ref[...], preferred_element_type=F32)
    y = jax.nn.gelu(y + vec_ref[0:1, :SSM_BLK_IN] * u)
    z = jnp.dot(y.astype(BF16), glu_ref[...], preferred_element_type=F32) + vec_ref[1:2, :]
    out = z[:, :SSM_BLK_IN] * jax.nn.sigmoid(z[:, SSM_BLK_IN:])
    out = pltpu.einshape("ijd->jid", out.reshape(ti, N_SUBSEQ, SSM_BLK_IN))
    o_ref[...] = out.astype(BF16)


def _s5_params(a_re, a_im, b_re, b_im, c_re, c_im, d_skip, log_dt, w_glu, b_glu, lsub):
    g = a_re.shape[0]
    nf = g // GROUPS_PER_BLK
    a = lax.complex(a_re.astype(F32), a_im.astype(F32))
    dt = jnp.exp(log_dt.astype(F32))[:, None]
    a_bar = jnp.exp(a * dt)
    a_pow = jnp.exp(a * dt * lsub)
    b_bar = ((a_bar - 1.0) / a)[..., None] * lax.complex(b_re.astype(F32), b_im.astype(F32))
    def block_diag(m, inner):
        rows = m.shape[1]
        tile = jnp.tile(jnp.eye(inner, dtype=F32), (1, GROUPS_PER_BLK))
        wide = jnp.einsum('frk,kn->frn', m, tile, precision=lax.Precision.HIGHEST)
        rg = jnp.arange(rows, dtype=jnp.int32)[:, None] // (rows // GROUPS_PER_BLK)
        cg = jnp.arange(GROUPS_PER_BLK * inner, dtype=jnp.int32)[None, :] // inner
        return jnp.where(rg == cg, wide, 0.0)

    def bd_in(m):
        return block_diag(jnp.swapaxes(m, 1, 2).reshape(nf, SSM_BLK_IN, STATE), STATE)

    def bd_out(m):
        return block_diag(jnp.swapaxes(m, 1, 2).reshape(nf, SSM_BLK_ST, GROUP_CH), GROUP_CH)

    def bd_glu(m):
        return block_diag(m.reshape(nf, SSM_BLK_IN, GROUP_CH), GROUP_CH)

    bdb = jnp.concatenate([bd_in(jnp.real(b_bar)), bd_in(jnp.imag(b_bar))], axis=2).astype(BF16)
    bdc = jnp.concatenate([bd_out(c_re.astype(F32)), bd_out(-c_im.astype(F32))], axis=1).astype(BF16)
    wg = w_glu.astype(F32)
    glu = jnp.concatenate([bd_glu(wg[..., :GROUP_CH]), bd_glu(wg[..., GROUP_CH:])], axis=2).astype(BF16)
    flat = lambda m: m.reshape(nf, 1, SSM_BLK_ST)
    avec = jnp.concatenate([flat(jnp.real(a_bar)), flat(jnp.imag(a_bar)),
                            flat(jnp.real(a_pow)), flat(jnp.imag(a_pow))], axis=1)
    bg = b_glu.astype(F32).reshape(nf, GROUPS_PER_BLK, 2 * GROUP_CH)
    bvec = jnp.concatenate([bg[..., :GROUP_CH].reshape(nf, 1, SSM_BLK_IN),
                            bg[..., GROUP_CH:].reshape(nf, 1, SSM_BLK_IN)], axis=2)
    dvec = jnp.concatenate([d_skip.astype(F32).reshape(nf, 1, SSM_BLK_IN),
                            jnp.zeros((nf, 1, SSM_BLK_IN), F32)], axis=2)
    vec = jnp.concatenate([dvec, bvec], axis=1)
    return bdb, bdc, glu, avec, vec


def _s5(proj, params, ti=S5_STEPS):
    bdb, bdc, glu, avec, vec = params
    bsz, seqlen, n_in = proj.shape
    width = n_in - QKV_WIDTH
    nf = width // SSM_BLK_IN
    lsub = seqlen // N_SUBSEQ
    ti = min(ti, lsub)
    u_col0 = QKV_WIDTH // SSM_BLK_IN
    p4 = proj.reshape(bsz, N_SUBSEQ, lsub, n_in)
    grid = (bsz, nf, lsub // ti)
    u_spec = pl.BlockSpec((None, N_SUBSEQ, ti, SSM_BLK_IN), lambda b, f, i: (b, 0, i, u_col0 + f))
    o_spec = pl.BlockSpec((None, N_SUBSEQ, ti, SSM_BLK_IN), lambda b, f, i: (b, 0, i, f))
    blk = lambda r, c: pl.BlockSpec((None, r, c), lambda b, f, i: (f, 0, 0))
    f_spec = pl.BlockSpec((None, None, N_SUBSEQ, 2 * SSM_BLK_ST), lambda b, f, i: (b, f, 0, 0))
    scratch = [pltpu.VMEM((ti * SUBLANES, 2 * SSM_BLK_ST), F32),
               pltpu.VMEM((SUBLANES, 2 * SSM_BLK_ST), F32)]
    sem = ("arbitrary", "arbitrary", "arbitrary")
    fin = pl.pallas_call(
        functools.partial(_s5_pass1_kernel, ti=ti),
        grid=grid,
        in_specs=[u_spec, blk(SSM_BLK_IN, 2 * SSM_BLK_ST), blk(4, SSM_BLK_ST)],
        out_specs=f_spec,
        out_shape=jax.ShapeDtypeStruct((bsz, nf, N_SUBSEQ, 2 * SSM_BLK_ST), F32),
        scratch_shapes=scratch,
        compiler_params=_cparams(sem),
        name="s5_pass1",
    )(p4, bdb, avec)
    out = pl.pallas_call(
        functools.partial(_s5_pass2_kernel, ti=ti),
        grid=grid,
        in_specs=[u_spec, f_spec, blk(SSM_BLK_IN, 2 * SSM_BLK_ST), blk(4, SSM_BLK_ST),
                  blk(2 * SSM_BLK_ST, SSM_BLK_IN), blk(SSM_BLK_IN, 2 * SSM_BLK_IN),
                  blk(2, 2 * SSM_BLK_IN)],
        out_specs=o_spec,
        out_shape=jax.ShapeDtypeStruct((bsz, N_SUBSEQ, lsub, width), BF16),
        scratch_shapes=scratch,
        compiler_params=_cparams(sem),
        name="s5_pass2",
    )(p4, fin, bdb, avec, bdc, glu, vec)
    return out.reshape(bsz, seqlen, width)


def _layer_norm_rows(y, g, b):
    mu = jnp.mean(y, axis=-1, keepdims=True)
    yc = y - mu
    var = jnp.mean(yc * yc, axis=-1, keepdims=True)
    return yc * lax.rsqrt(var + EPS) * g + b


def _pack_bf16_pairs(lo, hi):
    lo_bits = lax.bitcast_convert_type(lo.astype(BF16).astype(F32), jnp.uint32)
    hi_bits = lax.bitcast_convert_type(hi.astype(BF16).astype(F32), jnp.uint32)
    return (lo_bits >> 16) | (hi_bits & jnp.uint32(0xFFFF0000))


def _unpack_bf16_pairs(w):
    lo = lax.bitcast_convert_type(w << 16, F32)
    hi = lax.bitcast_convert_type(w & jnp.uint32(0xFFFF0000), F32)
    return lo, hi


def _outproj_kernel(attn_ref, ssm_ref, gs_ref, w_ref, x_ref, mod_ref, ln_ref, wr_ref, br_ref,
                    x1_ref, hp_ref, logit_ref, ybuf, *, nc):
    m = mod_ref[...]
    ssm = ssm_ref[...].astype(F32)
    ms = jnp.mean(ssm * ssm, axis=-1, keepdims=True)
    ssm_n = (ssm * lax.rsqrt(ms + EPS) * gs_ref[...]).astype(BF16)
    attn = attn_ref[...]
    ka = attn.shape[1]
    d = x_ref.shape[1]
    for n0 in range(0, d, nc):
        mix = (jnp.dot(attn, w_ref[:ka, n0:n0 + nc], preferred_element_type=F32)
               + jnp.dot(ssm_n, w_ref[ka:, n0:n0 + nc], preferred_element_type=F32))
        ybuf[:, n0:n0 + nc] = DN_ALPHA * x_ref[:, n0:n0 + nc] + (1.0 + m[2:3, n0:n0 + nc]) * mix
    x1 = _layer_norm_rows(ybuf[...], ln_ref[0:1, :], ln_ref[1:2, :])
    x1_ref[...] = x1
    h2 = x1 * (1.0 + m[4:5, :]) + m[3:4, :]
    hp_ref[...] = _pack_bf16_pairs(h2[:, :d // 2], h2[:, d // 2:])
    hi = h2.astype(BF16)
    lo = (h2 - hi.astype(F32)).astype(BF16)
    tm = h2.shape[0]
    r = jnp.dot(jnp.concatenate([hi, lo], axis=0), wr_ref[...], preferred_element_type=F32)
    logit_ref[...] = r[:tm, :N_EXPERTS] + r[:tm, N_EXPERTS:] + r[tm:, :N_EXPERTS] + br_ref[...]


def _route_kernel(logit_ref, idx_ref, gate_ref, rank_ref, cnt_ref):
    logits = logit_ref[...]
    tm = logits.shape[0]
    lane = lax.broadcasted_iota(jnp.int32, logits.shape, 1)
    vals, idxs = [], []
    for _ in range(TOP_K):
        mx = jnp.max(logits, axis=-1, keepdims=True)
        ix = jnp.min(jnp.where(logits == mx, lane, N_EXPERTS), axis=-1, keepdims=True)
        vals.append(mx)
        idxs.append(ix)
        logits = jnp.where(lane == ix, -jnp.inf, logits)
    tv = jnp.concatenate(vals, axis=1)
    e = jnp.exp(tv - vals[0])
    gate_ref[...] = e / jnp.sum(e, axis=-1, keepdims=True)
    idx_ref[...] = jnp.concatenate(idxs, axis=1)
    tri = (lax.broadcasted_iota(jnp.int32, (tm, tm), 0) > lax.broadcasted_iota(jnp.int32, (tm, tm), 1)).astype(BF16)
    run = jnp.zeros((1, N_EXPERTS), F32)
    ranks = []
    for ix in idxs:
        onehot = (lane == ix).astype(F32)
        before = jnp.dot(tri, onehot.astype(BF16), preferred_element_type=F32) + run
        ranks.append(jnp.sum(onehot * before, axis=-1, keepdims=True))
        run = run + jnp.sum(onehot, axis=0, keepdims=True)
    rank_ref[...] = jnp.concatenate(ranks, axis=1).astype(jnp.int32)
    cnt_ref[...] = jnp.zeros_like(cnt_ref)
    cnt_ref[0:1, 0:N_EXPERTS] = run


def _out_proj(attn_n, ssm, g_ssm, w_out_bf, x, mod3, ln1, wr, br, tm=128):
    bsz, seqlen, d = x.shape
    ka = attn_n.shape[-1]
    ks = w_out_bf.shape[0] - ka
    row = lambda b, i: (b, i, 0)
    const = lambda b, i: (0, 0)
    return pl.pallas_call(
        functools.partial(_outproj_kernel, nc=512),
        grid=(bsz, seqlen // tm),
        in_specs=[pl.BlockSpec((None, tm, ka), row),
                  pl.BlockSpec((None, tm, ks), row),
                  pl.BlockSpec((1, ks), const),
                  _resident((ka + ks, d), const),
                  pl.BlockSpec((None, tm, d), row),
                  pl.BlockSpec((None, 6, d), lambda b, i: (b, 0, 0)),
                  pl.BlockSpec((2, d), const),
                  pl.BlockSpec((d, 2 * N_EXPERTS), const),
                  pl.BlockSpec((1, N_EXPERTS), const)],
        out_specs=[pl.BlockSpec((None, tm, d), row),
                   pl.BlockSpec((None, tm, d // 2), row),
                   pl.BlockSpec((None, tm, N_EXPERTS), row)],
        out_shape=[jax.ShapeDtypeStruct((bsz, seqlen, d), F32),
                   jax.ShapeDtypeStruct((bsz, seqlen, d // 2), jnp.uint32),
                   jax.ShapeDtypeStruct((bsz, seqlen, N_EXPERTS), F32)],
        scratch_shapes=[pltpu.VMEM((tm, d), F32)],
        compiler_params=_cparams(("arbitrary", "arbitrary")),
        name="out_proj_ln_router",
    )(attn_n, ssm, g_ssm.reshape(1, ks), w_out_bf, x, mod3, ln1, wr, br)


def _route(logits, tm=ROUTE_TILE):
    n_tok = logits.shape[0]
    nt = n_tok // tm
    row = lambda i: (i, 0)
    k_shape = lambda dt: jax.ShapeDtypeStruct((n_tok, TOP_K), dt)
    return pl.pallas_call(
        _route_kernel,
        grid=(nt,),
        in_specs=[pl.BlockSpec((tm, N_EXPERTS), row)],
        out_specs=[pl.BlockSpec((tm, TOP_K), row), pl.BlockSpec((tm, TOP_K), row), pl.BlockSpec((tm, TOP_K), row),
                   pl.BlockSpec((None, SUBLANES, LANES), lambda i: (i, 0, 0))],
        out_shape=[k_shape(jnp.int32), k_shape(F32), k_shape(jnp.int32),
                   jax.ShapeDtypeStruct((nt, SUBLANES, LANES), F32)],
        compiler_params=_cparams(("arbitrary",)),
        name="route_topk",
    )(logits)


def _stream_expert_weights(blk_e, first, nxt, last, w_hbm, stage, wbf, sem):
    ct = pl.program_id(0)
    rb = pl.program_id(1)
    width = wbf[0].shape[1]

    def copies(e, col_tile):
        c0 = pl.multiple_of(col_tile * width, width)
        return [pltpu.make_async_copy(w.at[e, :, pl.ds(c0, width)], s, sem.at[j])
                for j, (w, s) in enumerate(zip(w_hbm, stage))]

    @pl.when((ct == 0) & (rb == 0))
    def _():
        for cp in copies(blk_e[0], 0):
            cp.start(priority=1)

    @pl.when(first[rb] == 1)
    def _():
        for cp in copies(blk_e[rb], ct):
            cp.wait()
        def convert(c, carry):
            r0 = pl.multiple_of(c * CAST_ROWS, CAST_ROWS)
            for s, w in zip(stage, wbf):
                w[pl.ds(r0, CAST_ROWS), :] = s[pl.ds(r0, CAST_ROWS), :].astype(BF16)
            return carry
        lax.fori_loop(0, stage[0].shape[0] // CAST_ROWS, convert, 0)

        @pl.when(last[rb] == 0)
        def _():
            for cp in copies(nxt[rb], ct):
                cp.start(priority=1)

        @pl.when((last[rb] == 1) & (ct + 1 < pl.num_programs(0)))
        def _():
            for cp in copies(nxt[rb], ct + 1):
                cp.start(priority=1)


def _expert_up_kernel(blk_e, first, nreal, nxt, last, xs_ref, wg_hbm, wu_hbm, bg_ref, bu_ref, act_ref,
                      stg_g, stg_u, wgb, wub, sem):
    rb = pl.program_id(1)
    _stream_expert_weights(blk_e, first, nxt, last, (wg_hbm, wu_hbm), (stg_g, stg_u), (wgb, wub), sem)

    @pl.when(rb < nreal[0])
    def _():
        x = jnp.concatenate(_unpack_bf16_pairs(xs_ref[...]), axis=1).astype(BF16)
        g = jnp.dot(x, wgb[...], preferred_element_type=F32) + bg_ref[...]
        up = jnp.dot(x, wub[...], preferred_element_type=F32) + bu_ref[...]
        g = jnp.minimum(g, SWIGLU_LIMIT)
        up = jnp.clip(up, -SWIGLU_LIMIT, SWIGLU_LIMIT)
        act_ref[...] = (g * jax.nn.sigmoid(SWIGLU_ALPHA * g) * (up + 1.0)).astype(BF16)

    @pl.when(rb >= nreal[0])
    def _():
        act_ref[...] = jnp.zeros_like(act_ref)


def _expert_down_kernel(blk_e, first, nreal, nxt, last, act_ref, wd_hbm, bd_ref, y_ref, stg, wdb, sem):
    rb = pl.program_id(1)
    _stream_expert_weights(blk_e, first, nxt, last, (wd_hbm,), (stg,), (wdb,), sem)

    @pl.when(rb < nreal[0])
    def _():
        y = jnp.dot(act_ref[...], wdb[...], preferred_element_type=F32) + bd_ref[...]
        half = y.shape[1] // 2
        y_ref[...] = _pack_bf16_pairs(y[:, :half], y[:, half:])

    @pl.when(rb >= nreal[0])
    def _():
        y_ref[...] = jnp.zeros_like(y_ref)


def _experts(xs, sched, w_gate, b_gate, w_up, b_up, w_down, b_down, tf=512, tn=DOWN_TILE):
    cap = xs.shape[0]
    n_e, d, dff = w_gate.shape
    nblk = cap // EXPERT_BLK
    hbm = pl.BlockSpec(memory_space=pl.ANY)
    act = pl.pallas_call(
        _expert_up_kernel,
        grid_spec=pltpu.PrefetchScalarGridSpec(
            num_scalar_prefetch=5,
            grid=(dff // tf, nblk),
            in_specs=[pl.BlockSpec((EXPERT_BLK, d // 2), lambda f, r, be, *_: (r, 0)),
                      hbm, hbm,
                      pl.BlockSpec((None, 1, tf), lambda f, r, be, *_: (be[r], 0, f)),
                      pl.BlockSpec((None, 1, tf), lambda f, r, be, *_: (be[r], 0, f))],
            out_specs=pl.BlockSpec((EXPERT_BLK, tf), lambda f, r, be, *_: (r, f)),
            scratch_shapes=[pltpu.VMEM((d, tf), F32), pltpu.VMEM((d, tf), F32),
                            pltpu.VMEM((d, tf), BF16), pltpu.VMEM((d, tf), BF16),
                            pltpu.SemaphoreType.DMA((2,))]),
        out_shape=jax.ShapeDtypeStruct((cap, dff), BF16),
        compiler_params=_cparams(("arbitrary", "arbitrary")),
        name="expert_gate_up",
    )(*sched, xs, w_gate, w_up, b_gate.reshape(n_e, 1, dff), b_up.reshape(n_e, 1, dff))
    ys = pl.pallas_call(
        _expert_down_kernel,
        grid_spec=pltpu.PrefetchScalarGridSpec(
            num_scalar_prefetch=5,
            grid=(d // tn, nblk),
            in_specs=[pl.BlockSpec((EXPERT_BLK, dff), lambda n, r, be, *_: (r, 0)),
                      hbm,
                      pl.BlockSpec((None, 1, tn), lambda n, r, be, *_: (be[r], 0, n))],
            out_specs=pl.BlockSpec((EXPERT_BLK, tn // 2), lambda n, r, be, *_: (r, n)),
            scratch_shapes=[pltpu.VMEM((dff, tn), F32), pltpu.VMEM((dff, tn), BF16),
                            pltpu.SemaphoreType.DMA((1,))]),
        out_shape=jax.ShapeDtypeStruct((cap, d // 2), jnp.uint32),
        compiler_params=_cparams(("arbitrary", "arbitrary")),
        name="expert_down",
    )(*sched, act, w_down, b_down.reshape(n_e, 1, d))
    return ys


def _route_tables(cnt, top_idx, rank):
    n_assign = top_idx.size
    counts = cnt[:, 0, :N_EXPERTS].astype(jnp.int32)
    tot = jnp.sum(counts, axis=0)
    padded = (tot + EXPERT_BLK - 1) // EXPERT_BLK * EXPERT_BLK
    pend = jnp.cumsum(padded)
    pstart = pend - padded
    base = pstart[None, :] + jnp.cumsum(counts, axis=0) - counts
    experts = jnp.arange(N_EXPERTS, dtype=jnp.int32)
    idx_t = top_idx.reshape(-1, ROUTE_TILE * TOP_K)
    dest = jnp.sum(jnp.where(idx_t[..., None] == experts, base[:, None, :], 0), axis=-1)
    dest = (dest.reshape(-1) + rank.reshape(-1)).astype(jnp.int32)
    cap = ((n_assign + EXPERT_BLK - 1) // EXPERT_BLK) * EXPERT_BLK + N_EXPERTS * EXPERT_BLK
    nblk = cap // EXPERT_BLK
    blk = jnp.arange(nblk, dtype=jnp.int32)
    blk_e = jnp.sum((pend[None, :] <= blk[:, None] * EXPERT_BLK).astype(jnp.int32), axis=1)
    blk_e = jnp.minimum(blk_e, N_EXPERTS - 1).astype(jnp.int32)
    nreal = (pend[-1:] // EXPERT_BLK).astype(jnp.int32)
    real = blk < nreal[0]
    first = real & jnp.concatenate([jnp.ones((1,), bool), blk_e[1:] != blk_e[:-1]])
    starts = jnp.where(first, blk, nblk)
    nxt_blk = jnp.concatenate([lax.cummin(starts, reverse=True)[1:], jnp.full((1,), nblk, jnp.int32)])
    last = nxt_blk >= nblk
    nxt = jnp.where(last, blk_e[0], blk_e[jnp.minimum(nxt_blk, nblk - 1)]).astype(jnp.int32)
    sched = (blk_e, first.astype(jnp.int32), nreal, nxt, last.astype(jnp.int32))
    shift = base % GATHER_CHUNK
    span = jnp.where(counts > 0, (counts + shift + GATHER_CHUNK - 1) // GATHER_CHUNK * GATHER_CHUNK, 0)
    off = jnp.cumsum(span, axis=1) - span
    flat = lambda a: a.reshape(-1).astype(jnp.int32)
    runs = (flat(span // GATHER_CHUNK), flat(base - shift), flat(off), (off + shift).astype(jnp.int32)[:, None, :])
    return dest, cap, sched, (pstart + tot).astype(jnp.int32), (padded - tot).astype(jnp.int32), runs


def _row_copy(src, src_row, dst, dst_row, sem):
    return pltpu.make_async_copy(src.at[pl.ds(src_row, 1)], dst.at[pl.ds(dst_row, 1)], sem)


def _dispatch_kernel(pad0_ref, padn_ref, nreal_ref, hp_ref, dest_ref, xs_ref, zblk, sem, zsem, bsem):
    i = pl.program_id(0)
    tm = hp_ref.shape[0]
    nblk = xs_ref.shape[0] // EXPERT_BLK

    def for_each_pad_row(fn):
        def per_expert(e, c):
            def per_row(r, c2):
                fn(_row_copy(zblk, 0, xs_ref, pad0_ref[e] + r, zsem))
                return c2
            return lax.fori_loop(0, padn_ref[e], per_row, c)
        lax.fori_loop(0, N_EXPERTS, per_expert, 0)

    def for_each_unused_block(fn):
        def per_block(b, c):
            r0 = pl.multiple_of(b * EXPERT_BLK, EXPERT_BLK)
            fn(pltpu.make_async_copy(zblk, xs_ref.at[pl.ds(r0, EXPERT_BLK)], bsem))
            return c
        lax.fori_loop(nreal_ref[0], nblk, per_block, 0)

    @pl.when(i == 0)
    def _():
        zblk[...] = jnp.zeros_like(zblk)
        for_each_pad_row(lambda cp: cp.start())
        for_each_unused_block(lambda cp: cp.start())

    def issue(r, c):
        for k in range(TOP_K):
            _row_copy(hp_ref, r, xs_ref, dest_ref[r * TOP_K + k], sem).start(priority=k % 2)
        return c
    lax.fori_loop(0, tm, issue, 0, unroll=2)

    def drain(r, c):
        for k in range(TOP_K):
            _row_copy(hp_ref, 0, xs_ref, 0, sem).wait()
        return c
    lax.fori_loop(0, tm, drain, 0)

    @pl.when(i == 0)
    def _():
        for_each_pad_row(lambda cp: cp.wait())
        for_each_unused_block(lambda cp: cp.wait())


def _dispatch(hp, dest, pad0, padn, nreal, cap, tm=MOVE_TILE):
    n_tok, half = hp.shape
    return pl.pallas_call(
        _dispatch_kernel,
        grid_spec=pltpu.PrefetchScalarGridSpec(
            num_scalar_prefetch=3,
            grid=(n_tok // tm,),
            in_specs=[pl.BlockSpec((tm, half), lambda i, *_: (i, 0)),
                      pl.BlockSpec((tm * TOP_K,), lambda i, *_: (i,), memory_space=pltpu.SMEM)],
            out_specs=pl.BlockSpec(memory_space=pl.ANY),
            scratch_shapes=[pltpu.VMEM((EXPERT_BLK, half), jnp.uint32),
                            pltpu.SemaphoreType.DMA, pltpu.SemaphoreType.DMA, pltpu.SemaphoreType.DMA]),
        out_shape=jax.ShapeDtypeStruct((cap, half), jnp.uint32),
        compiler_params=_cparams(("arbitrary",)),
        name="moe_dispatch",
    )(pad0, padn, nreal, hp, dest)


def _combine_kernel(npiece_ref, src_ref, off_ref, idx_ref, rank_ref, gate_ref, runrow_ref, x1_ref, mod_ref, ln_ref,
                    ys_ref, o_ref, buf, sem, *, tn):
    i = pl.program_id(0)
    tm = x1_ref.shape[0]
    rows = buf.shape[1]

    def for_each_piece(tile, fn):
        slot = tile % 2

        def per_expert(e, c):
            t = tile * N_EXPERTS + e

            def per_piece(j, c2):
                src = pl.multiple_of(src_ref[t] + j * GATHER_CHUNK, GATHER_CHUNK)
                dst = pl.multiple_of(off_ref[t] + j * GATHER_CHUNK, GATHER_CHUNK)
                fn(pltpu.make_async_copy(ys_ref.at[pl.ds(src, GATHER_CHUNK)],
                                         buf.at[slot, pl.ds(dst, GATHER_CHUNK)], sem.at[slot]))
                return c2
            return lax.fori_loop(0, npiece_ref[t], per_piece, c)
        lax.fori_loop(0, N_EXPERTS, per_expert, 0)

    @pl.when(i == 0)
    def _():
        buf[...] = jnp.zeros_like(buf)
        for_each_piece(0, lambda cp: cp.start())

    @pl.when(i + 1 < pl.num_programs(0))
    def _():
        for_each_piece(i + 1, lambda cp: cp.start())

    for_each_piece(i, lambda cp: cp.wait())

    lane = lax.broadcasted_iota(jnp.int32, (tm, N_EXPERTS), 1)
    col = lax.broadcasted_iota(jnp.int32, (tm, rows), 1)
    sel = jnp.zeros((tm, rows), F32)
    for k in range(TOP_K):
        run_row = jnp.sum(jnp.where(lane == idx_ref[:, k:k + 1], runrow_ref[...], 0), axis=-1, keepdims=True)
        sel = sel + jnp.where(col == run_row + rank_ref[:, k:k + 1], gate_ref[:, k:k + 1], 0.0)
    sel = sel.astype(BF16)

    cur = buf.at[i % 2]
    m = mod_ref[...]
    d = x1_ref.shape[1]
    hw = tn // 2
    cw = 512
    for w0 in range(0, d // 2, cw):
        wl, wh = _unpack_bf16_pairs(cur[:, w0:w0 + cw])
        lo = jnp.dot(sel, wl.astype(BF16), preferred_element_type=F32)
        hi = jnp.dot(sel, wh.astype(BF16), preferred_element_type=F32)
        for half, moe in ((0, lo), (1, hi)):
            c0 = (w0 // hw) * tn + half * hw + w0 % hw
            o_ref[:, c0:c0 + cw] = DN_ALPHA * x1_ref[:, c0:c0 + cw] + (1.0 + m[5:6, c0:c0 + cw]) * moe
    o_ref[...] = _layer_norm_rows(o_ref[...], ln_ref[0:1, :], ln_ref[1:2, :])


def _combine(ys, runs, top_idx, rank, gates, x1, mod3, ln2, tm=ROUTE_TILE, tn=DOWN_TILE):
    bsz, seqlen, d = x1.shape
    n_tok = bsz * seqlen
    per_b = seqlen // tm
    n_piece, src0, off, run_row = runs
    buf_rows = tm * TOP_K + N_EXPERTS * 2 * GATHER_CHUNK
    tok = lambda i, *_: (i, 0)
    out = pl.pallas_call(
        functools.partial(_combine_kernel, tn=tn),
        grid_spec=pltpu.PrefetchScalarGridSpec(
            num_scalar_prefetch=3,
            grid=(n_tok // tm,),
            in_specs=[pl.BlockSpec((tm, TOP_K), tok), pl.BlockSpec((tm, TOP_K), tok), pl.BlockSpec((tm, TOP_K), tok),
                      pl.BlockSpec((None, 1, N_EXPERTS), lambda i, *_: (i, 0, 0)),
                      pl.BlockSpec((tm, d), tok),
                      pl.BlockSpec((None, 6, d), lambda i, *_: (i // per_b, 0, 0)),
                      pl.BlockSpec((2, d), lambda i, *_: (0, 0)),
                      pl.BlockSpec(memory_space=pl.ANY)],
            out_specs=pl.BlockSpec((tm, d), tok),
            scratch_shapes=[pltpu.VMEM((2, buf_rows, d // 2), jnp.uint32), pltpu.SemaphoreType.DMA((2,))]),
        out_shape=jax.ShapeDtypeStruct((n_tok, d), F32),
        compiler_params=_cparams(("arbitrary",)),
        name="moe_combine_ln",
    )(n_piece, src0, off, top_idx, rank, gates, run_row, x1.reshape(n_tok, d), mod3, ln2, ys)
    return out.reshape(bsz, seqlen, d)


def kernel(x, c, positions, w_ada, b_ada, w_in, attn_sinks, ssm_a_re, ssm_a_im, ssm_b_re, ssm_b_im,
           ssm_c_re, ssm_c_im, ssm_d, ssm_log_dt, ssm_w_glu, ssm_b_glu, g_attn_out, g_ssm_out, w_out,
           ln1_g, ln1_b, w_router, b_router, w_gate, b_gate, w_up, b_up, w_down, b_down, ln2_g, ln2_b):
    bsz, seqlen, d = x.shape
    lsub = seqlen // N_SUBSEQ
    n_tok = bsz * seqlen
    rope_tab = _rope_tables(positions)
    for l in range(w_ada.shape[0]):
        mod3 = _ada_mod(c, w_ada[l], b_ada[l]).reshape(bsz, 6, d)
        proj = _in_proj(x, mod3, rope_tab, w_in[l].astype(BF16))
        attn_n = _attention(proj, attn_sinks[l].astype(F32), g_attn_out[l].astype(F32))
        s5p = _s5_params(ssm_a_re[l], ssm_a_im[l], ssm_b_re[l], ssm_b_im[l], ssm_c_re[l], ssm_c_im[l],
                         ssm_d[l], ssm_log_dt[l], ssm_w_glu[l], ssm_b_glu[l], lsub)
        ssm = _s5(proj, s5p)
        wr_hi = w_router[l].astype(BF16)
        wr_lo = (w_router[l] - wr_hi.astype(F32)).astype(BF16)
        x1, hp, logits = _out_proj(
            attn_n, ssm, g_ssm_out[l].astype(F32), w_out[l].astype(BF16), x, mod3,
            jnp.stack([ln1_g[l], ln1_b[l]]).astype(F32),
            jnp.concatenate([wr_hi, wr_lo], axis=1), b_router[l].reshape(1, N_EXPERTS).astype(F32))
        top_idx, gates, rank, cnt = _route(logits.reshape(n_tok, N_EXPERTS))
        dest, cap, sched, pad0, padn, runs = _route_tables(cnt, top_idx, rank)
        xs = _dispatch(hp.reshape(n_tok, d // 2), dest, pad0, padn, sched[2], cap)
        ys = _experts(xs, sched, w_gate[l], b_gate[l], w_up[l], b_up[l], w_down[l], b_down[l])
        x = _combine(ys, runs, top_idx, rank, gates, x1, mod3, jnp.stack([ln2_g[l], ln2_b[l]]).astype(F32))
    return x
```

```python
import functools
import math

import jax
import jax.numpy as jnp
from jax import lax
from jax.experimental import pallas as pl
from jax.experimental.pallas import tpu as pltpu

F32 = jnp.float32
BF16 = jnp.bfloat16

HEAD_DIM = 64
N_Q_HEADS = 32
N_KV_HEADS = 4
GQ = N_Q_HEADS // N_KV_HEADS
ATTN_WIDTH = N_Q_HEADS * HEAD_DIM
KV_WIDTH = N_KV_HEADS * HEAD_DIM
QKV_WIDTH = ATTN_WIDTH + 2 * KV_WIDTH
BLK = 128
ROT_DIM = HEAD_DIM // 4
ROPE_THETA = 500000.0
GROUP_CH = 16
STATE = 64
N_EXPERTS = 32
TOP_K = 4
SWIGLU_LIMIT = 7.0
SWIGLU_ALPHA = 1.702
EXPERT_BLK = 256
ROUTE_TILE = 256
GATHER_CHUNK = 8
MOVE_TILE = 256
DOWN_TILE = 4096
S5_STEPS = 512
CAST_ROWS = 128
DEPTH = 1
DN_ALPHA = (2.0 * DEPTH) ** 0.25
EPS = 1e-5

LANES = 128
SUBLANES = 8
N_SUBSEQ = SUBLANES
GROUPS_PER_BLK = 16
SSM_BLK_IN = GROUPS_PER_BLK * GROUP_CH
SSM_BLK_ST = GROUPS_PER_BLK * STATE
VMEM_LIMIT = 56 * 1024 * 1024


def _cparams(sem, vmem=VMEM_LIMIT):
    return pltpu.CompilerParams(dimension_semantics=sem, vmem_limit_bytes=vmem)


def _resident(shape, index_map):
    return pl.BlockSpec(shape, index_map, pipeline_mode=pl.Buffered(1))


def _ada_kernel(c_ref, w_ref, b_ref, o_ref):
    c = c_ref[...]
    ca = c * jax.nn.sigmoid(c)
    o_ref[...] = jnp.dot(ca.astype(BF16), w_ref[...].astype(BF16),
                         preferred_element_type=F32) + b_ref[...]


def _ada_mod(c, w_ada, b_ada, tn=1024):
    bsz, d = c.shape
    n = w_ada.shape[1]
    c8 = jnp.zeros((SUBLANES, d), F32).at[:bsz].set(c)
    out = pl.pallas_call(
        _ada_kernel,
        grid=(n // tn,),
        in_specs=[pl.BlockSpec((SUBLANES, d), lambda j: (0, 0)),
                  pl.BlockSpec((d, tn), lambda j: (0, j)),
                  pl.BlockSpec((1, tn), lambda j: (0, j))],
        out_specs=pl.BlockSpec((SUBLANES, tn), lambda j: (0, j)),
        out_shape=jax.ShapeDtypeStruct((SUBLANES, n), F32),
        compiler_params=_cparams(("arbitrary",)),
        name="ada_mod",
    )(c8, w_ada, b_ada.reshape(1, n))
    return out[:bsz]


def _rope(t, tab):
    c, s_lo, s_hi = tab[:, :LANES], tab[:, LANES:2 * LANES], tab[:, 2 * LANES:]
    half = ROT_DIM // 2
    out = []
    for j in range(t.shape[1] // LANES):
        tj = t[:, j * LANES:(j + 1) * LANES]
        out.append(tj * c + pltpu.roll(tj, LANES - half, 1) * s_lo + pltpu.roll(tj, half, 1) * s_hi)
    return jnp.concatenate(out, axis=1)


def _inproj_kernel(x_ref, mod_ref, tab_ref, w_ref, o_ref, *, nc):
    m = mod_ref[...]
    h = (x_ref[...] * (1.0 + m[1:2, :]) + m[0:1, :]).astype(BF16)
    for n0 in range(0, o_ref.shape[-1], nc):
        p = jnp.dot(h, w_ref[:, n0:n0 + nc], preferred_element_type=F32)
        if n0 < ATTN_WIDTH:
            p = _rope(p * (HEAD_DIM ** -0.5), tab_ref[...])
        elif n0 == ATTN_WIDTH:
            p = jnp.concatenate([_rope(p[:, :KV_WIDTH], tab_ref[...]), p[:, KV_WIDTH:]], axis=1)
        o_ref[:, n0:n0 + nc] = p.astype(BF16)


def _in_proj(x, mod3, rope_tab, w_in_bf, tm=128, nc=512):
    bsz, seqlen, d = x.shape
    n_in = w_in_bf.shape[1]
    assert ATTN_WIDTH % nc == 0 and nc >= 2 * KV_WIDTH
    return pl.pallas_call(
        functools.partial(_inproj_kernel, nc=nc),
        grid=(bsz, seqlen // tm),
        in_specs=[pl.BlockSpec((None, tm, d), lambda b, i: (b, i, 0)),
                  pl.BlockSpec((None, 6, d), lambda b, i: (b, 0, 0)),
                  pl.BlockSpec((None, tm, 3 * LANES), lambda b, i: (b, i, 0)),
                  _resident((d, n_in), lambda b, i: (0, 0))],
        out_specs=pl.BlockSpec((None, tm, n_in), lambda b, i: (b, i, 0)),
        out_shape=jax.ShapeDtypeStruct((bsz, seqlen, n_in), BF16),
        compiler_params=_cparams(("arbitrary", "arbitrary")),
        name="in_proj",
    )(x, mod3, rope_tab, w_in_bf)


def _attn_kernel(sink_ref, q_ref, kc_ref, kp_ref, vc_ref, vp_ref, g_ref, o_ref):
    n = pl.program_id(1)
    low = lax.broadcasted_iota(jnp.int32, (2 * BLK, LANES), 1) < HEAD_DIM
    low_q = lax.broadcasted_iota(jnp.int32, (BLK, LANES), 1) < HEAD_DIM

    k_raw = jnp.concatenate([kp_ref[...], kc_ref[...]], axis=0).astype(F32)
    v_raw = jnp.concatenate([vp_ref[...], vc_ref[...]], axis=0).astype(F32)

    qi = lax.broadcasted_iota(jnp.int32, (BLK, BLK), 0)
    kj = lax.broadcasted_iota(jnp.int32, (BLK, BLK), 1)
    own = kj <= qi
    prev_ok = kj >= jnp.where(n > 0, 0, BLK)

    o_chunks = []
    for hk in range(N_KV_HEADS):
        kc = k_raw[:, (hk // 2) * LANES:(hk // 2 + 1) * LANES]
        vc = v_raw[:, (hk // 2) * LANES:(hk // 2 + 1) * LANES]
        k_sw = pltpu.roll(kc, HEAD_DIM, 1)
        v_sw = pltpu.roll(vc, HEAD_DIM, 1)
        if hk % 2 == 0:
            kk2 = jnp.where(low, kc, k_sw)
            v_lo = jnp.where(low, vc, 0.0)
            v_hi = jnp.where(low, 0.0, v_sw)
        else:
            kk2 = jnp.where(low, k_sw, kc)
            v_lo = jnp.where(low, v_sw, 0.0)
            v_hi = jnp.where(low, 0.0, vc)
        kk2 = kk2.astype(BF16)
        v_lo = v_lo.astype(BF16)
        v_hi = v_hi.astype(BF16)
        lhs = []
        for j in range(GQ // 2):
            c0 = (hk * (GQ // 2) + j) * LANES
            q2 = q_ref[:, c0:c0 + LANES].astype(F32)
            lhs.append(jnp.where(low_q, q2, 0.0).astype(BF16))
            lhs.append(jnp.where(low_q, 0.0, q2).astype(BF16))
        s_all = lax.dot_general(jnp.concatenate(lhs, axis=0), kk2,
                                (((1,), (1,)), ((), ())), preferred_element_type=F32)
        for j in range(GQ // 2):
            acc = None
            for side, vv in ((0, v_lo), (1, v_hi)):
                i = 2 * j + side
                s_prev = jnp.where(prev_ok, s_all[i * BLK:(i + 1) * BLK, :BLK], -1e30)
                s = jnp.where(own, s_all[i * BLK:(i + 1) * BLK, BLK:], s_prev)
                sink = sink_ref[hk * GQ + i]
                m = jnp.maximum(jnp.max(s, axis=-1, keepdims=True), sink)
                p = jnp.exp(s - m)
                denom = jnp.sum(p, axis=-1, keepdims=True) + jnp.exp(sink - m)
                p = p * (1.0 / denom)
                p = jnp.concatenate([jnp.where(own, 0.0, p), jnp.where(own, p, 0.0)], axis=1).astype(BF16)
                o = jnp.dot(p, vv, preferred_element_type=F32)
                acc = o if acc is None else acc + o
            o_chunks.append(acc)

    ssq = None
    for oc in o_chunks:
        t = jnp.sum(oc * oc, axis=-1, keepdims=True)
        ssq = t if ssq is None else ssq + t
    inv = lax.rsqrt(ssq * (1.0 / ATTN_WIDTH) + EPS)
    for j, oc in enumerate(o_chunks):
        o_ref[:, j * LANES:(j + 1) * LANES] = (oc * inv * g_ref[:, j * LANES:(j + 1) * LANES]).astype(BF16)


def _attention(qkv, sinks, g_attn):
    bsz, seqlen, _ = qkv.shape
    nb = seqlen // BLK
    kcol = ATTN_WIDTH // KV_WIDTH
    cur = lambda b, n: (b, n, 0)
    return pl.pallas_call(
        _attn_kernel,
        grid=(bsz, nb),
        in_specs=[pl.BlockSpec(memory_space=pltpu.SMEM),
                  pl.BlockSpec((None, BLK, ATTN_WIDTH), cur),
                  pl.BlockSpec((None, BLK, KV_WIDTH), lambda b, n: (b, n, kcol)),
                  pl.BlockSpec((None, BLK, KV_WIDTH), lambda b, n: (b, jnp.maximum(n - 1, 0), kcol)),
                  pl.BlockSpec((None, BLK, KV_WIDTH), lambda b, n: (b, n, kcol + 1)),
                  pl.BlockSpec((None, BLK, KV_WIDTH), lambda b, n: (b, jnp.maximum(n - 1, 0), kcol + 1)),
                  pl.BlockSpec((1, ATTN_WIDTH), lambda b, n: (0, 0))],
        out_specs=pl.BlockSpec((None, BLK, ATTN_WIDTH), cur),
        out_shape=jax.ShapeDtypeStruct((bsz, seqlen, ATTN_WIDTH), BF16),
        compiler_params=_cparams(("arbitrary", "arbitrary")),
        name="swa_attention",
    )(sinks, qkv, qkv, qkv, qkv, qkv, g_attn.reshape(1, ATTN_WIDTH))


def _rope_tables(positions):
    half = ROT_DIM // 2
    inv_freq = ROPE_THETA ** (-jnp.arange(0, ROT_DIM, 2, dtype=F32) / ROT_DIM)
    ang = positions.astype(F32)[..., None] * inv_freq
    cos, sin = jnp.cos(ang), jnp.sin(ang)
    shp = cos.shape[:-1] + (HEAD_DIM - ROT_DIM,)
    c = jnp.concatenate([cos, cos, jnp.ones(shp, F32)], axis=-1)
    z8 = jnp.zeros_like(sin)
    s_lo = jnp.concatenate([-sin, z8, jnp.zeros(shp, F32)], axis=-1)
    s_hi = jnp.concatenate([z8, sin, jnp.zeros(shp, F32)], axis=-1)
    rep = LANES // HEAD_DIM
    return jnp.concatenate([jnp.tile(c, rep), jnp.tile(s_lo, rep), jnp.tile(s_hi, rep)], axis=-1)


def _s5_scan(buf, ar, ai, hr, hi, ti, store):
    def step(i, carry):
        hr, hi = carry
        r0 = pl.multiple_of(i * SUBLANES, SUBLANES)
        row = buf[pl.ds(r0, SUBLANES), :]
        nhr = ar * hr - ai * hi + row[:, :SSM_BLK_ST]
        nhi = ar * hi + ai * hr + row[:, SSM_BLK_ST:]
        if store:
            buf[pl.ds(r0, SUBLANES), :] = jnp.concatenate([nhr, nhi], axis=1)
        return nhr, nhi
    return lax.fori_loop(0, ti, step, (hr, hi), unroll=4)


def _time_major(u_ref, ti):
    u = pltpu.einshape("jid->ijd", u_ref[...].astype(F32))
    return u.reshape(ti * N_SUBSEQ, u.shape[-1])


def _s5_pass1_kernel(u_ref, bdb_ref, a_ref, f_ref, buf, hst, *, ti):
    ic = pl.program_id(2)

    @pl.when(ic == 0)
    def _():
        hst[...] = jnp.zeros_like(hst)

    u = _time_major(u_ref, ti)
    buf[...] = jnp.dot(u.astype(BF16), bdb_ref[...], preferred_element_type=F32)
    ar = jnp.broadcast_to(a_ref[0:1, :], (SUBLANES, SSM_BLK_ST))
    ai = jnp.broadcast_to(a_ref[1:2, :], (SUBLANES, SSM_BLK_ST))
    hr, hi = _s5_scan(buf, ar, ai, hst[:, :SSM_BLK_ST], hst[:, SSM_BLK_ST:], ti, store=False)
    hst[...] = jnp.concatenate([hr, hi], axis=1)

    @pl.when(ic == pl.num_programs(2) - 1)
    def _():
        f_ref[...] = hst[...]


def _s5_pass2_kernel(u_ref, f_ref, bdb_ref, a_ref, bdc_ref, glu_ref, vec_ref, o_ref, buf, hst, *, ti):
    ic = pl.program_id(2)

    @pl.when(ic == 0)
    def _():
        fr, fi = f_ref[:, :SSM_BLK_ST], f_ref[:, SSM_BLK_ST:]
        pr = jnp.broadcast_to(a_ref[2:3, :], (SUBLANES, SSM_BLK_ST))
        pi = jnp.broadcast_to(a_ref[3:4, :], (SUBLANES, SSM_BLK_ST))
        row = lax.broadcasted_iota(jnp.int32, (SUBLANES, SSM_BLK_ST), 0)
        hr = jnp.zeros((SUBLANES, SSM_BLK_ST), F32)
        hi = jnp.zeros((SUBLANES, SSM_BLK_ST), F32)
        for _ in range(N_SUBSEQ - 1):
            nr = pr * hr - pi * hi + fr
            ni = pr * hi + pi * hr + fi
            hr = jnp.where(row == 0, 0.0, pltpu.roll(nr, 1, 0))
            hi = jnp.where(row == 0, 0.0, pltpu.roll(ni, 1, 0))
        hst[...] = jnp.concatenate([hr, hi], axis=1)

    u = _time_major(u_ref, ti)
    buf[...] = jnp.dot(u.astype(BF16), bdb_ref[...], preferred_element_type=F32)
    ar = jnp.broadcast_to(a_ref[0:1, :], (SUBLANES, SSM_BLK_ST))
    ai = jnp.broadcast_to(a_ref[1:2, :], (SUBLANES, SSM_BLK_ST))
    hr, hi = _s5_scan(buf, ar, ai, hst[:, :SSM_BLK_ST], hst[:, SSM_BLK_ST:], ti, store=True)
    hst[...] = jnp.concatenate([hr, hi], axis=1)

    y = jnp.dot(buf[...].astype(BF16), bdc_ref[...], preferred_element_type=F32)
    y = jax.nn.gelu(y + vec_ref[0:1, :SSM_BLK_IN] * u)
    z = jnp.dot(y.astype(BF16), glu_ref[...], preferred_element_type=F32) + vec_ref[1:2, :]
    out = z[:, :SSM_BLK_IN] * jax.nn.sigmoid(z[:, SSM_BLK_IN:])
    out = pltpu.einshape("ijd->jid", out.reshape(ti, N_SUBSEQ, SSM_BLK_IN))
    o_ref[...] = out.astype(BF16)


def _s5_params(a_re, a_im, b_re, b_im, c_re, c_im, d_skip, log_dt, w_glu, b_glu, lsub):
    g = a_re.shape[0]
    nf = g // GROUPS_PER_BLK
    a = lax.complex(a_re.astype(F32), a_im.astype(F32))
    dt = jnp.exp(log_dt.astype(F32))[:, None]
    a_bar = jnp.exp(a * dt)
    a_pow = jnp.exp(a * dt * lsub)
    b_bar = ((a_bar - 1.0) / a)[..., None] * lax.complex(b_re.astype(F32), b_im.astype(F32))
    def block_diag(m, inner):
        rows = m.shape[1]
        tile = jnp.tile(jnp.eye(inner, dtype=F32), (1, GROUPS_PER_BLK))
        wide = jnp.einsum('frk,kn->frn', m, tile, precision=lax.Precision.HIGHEST)
        rg = jnp.arange(rows, dtype=jnp.int32)[:, None] // (rows // GROUPS_PER_BLK)
        cg = jnp.arange(GROUPS_PER_BLK * inner, dtype=jnp.int32)[None, :] // inner
        return jnp.where(rg == cg, wide, 0.0)

    def bd_in(m):
        return block_diag(jnp.swapaxes(m, 1, 2).reshape(nf, SSM_BLK_IN, STATE), STATE)

    def bd_out(m):
        return block_diag(jnp.swapaxes(m, 1, 2).reshape(nf, SSM_BLK_ST, GROUP_CH), GROUP_CH)

    def bd_glu(m):
        return block_diag(m.reshape(nf, SSM_BLK_IN, GROUP_CH), GROUP_CH)

    bdb = jnp.concatenate([bd_in(jnp.real(b_bar)), bd_in(jnp.imag(b_bar))], axis=2).astype(BF16)
    bdc = jnp.concatenate([bd_out(c_re.astype(F32)), bd_out(-c_im.astype(F32))], axis=1).astype(BF16)
    wg = w_glu.astype(F32)
    glu = jnp.concatenate([bd_glu(wg[..., :GROUP_CH]), bd_glu(wg[..., GROUP_CH:])], axis=2).astype(BF16)
    flat = lambda m: m.reshape(nf, 1, SSM_BLK_ST)
    avec = jnp.concatenate([flat(jnp.real(a_bar)), flat(jnp.imag(a_bar)),
                            flat(jnp.real(a_pow)), flat(jnp.imag(a_pow))], axis=1)
    bg = b_glu.astype(F32).reshape(nf, GROUPS_PER_BLK, 2 * GROUP_CH)
    bvec = jnp.concatenate([bg[..., :GROUP_CH].reshape(nf, 1, SSM_BLK_IN),
                            bg[..., GROUP_CH:].reshape(nf, 1, SSM_BLK_IN)], axis=2)
    dvec = jnp.concatenate([d_skip.astype(F32).reshape(nf, 1, SSM_BLK_IN),
                            jnp.zeros((nf, 1, SSM_BLK_IN), F32)], axis=2)
    vec = jnp.concatenate([dvec, bvec], axis=1)
    return bdb, bdc, glu, avec, vec


def _s5(proj, params, ti=S5_STEPS):
    bdb, bdc, glu, avec, vec = params
    bsz, seqlen, n_in = proj.shape
    width = n_in - QKV_WIDTH
    nf = width // SSM_BLK_IN
    lsub = seqlen // N_SUBSEQ
    ti = min(ti, lsub)
    u_col0 = QKV_WIDTH // SSM_BLK_IN
    p4 = proj.reshape(bsz, N_SUBSEQ, lsub, n_in)
    grid = (bsz, nf, lsub // ti)
    u_spec = pl.BlockSpec((None, N_SUBSEQ, ti, SSM_BLK_IN), lambda b, f, i: (b, 0, i, u_col0 + f))
    o_spec = pl.BlockSpec((None, N_SUBSEQ, ti, SSM_BLK_IN), lambda b, f, i: (b, 0, i, f))
    blk = lambda r, c: pl.BlockSpec((None, r, c), lambda b, f, i: (f, 0, 0))
    f_spec = pl.BlockSpec((None, None, N_SUBSEQ, 2 * SSM_BLK_ST), lambda b, f, i: (b, f, 0, 0))
    scratch = [pltpu.VMEM((ti * SUBLANES, 2 * SSM_BLK_ST), F32),
               pltpu.VMEM((SUBLANES, 2 * SSM_BLK_ST), F32)]
    sem = ("arbitrary", "arbitrary", "arbitrary")
    fin = pl.pallas_call(
        functools.partial(_s5_pass1_kernel, ti=ti),
        grid=grid,
        in_specs=[u_spec, blk(SSM_BLK_IN, 2 * SSM_BLK_ST), blk(4, SSM_BLK_ST)],
        out_specs=f_spec,
        out_shape=jax.ShapeDtypeStruct((bsz, nf, N_SUBSEQ, 2 * SSM_BLK_ST), F32),
        scratch_shapes=scratch,
        compiler_params=_cparams(sem),
        name="s5_pass1",
    )(p4, bdb, avec)
    out = pl.pallas_call(
        functools.partial(_s5_pass2_kernel, ti=ti),
        grid=grid,
        in_specs=[u_spec, f_spec, blk(SSM_BLK_IN, 2 * SSM_BLK_ST), blk(4, SSM_BLK_ST),
                  blk(2 * SSM_BLK_ST, SSM_BLK_IN), blk(SSM_BLK_IN, 2 * SSM_BLK_IN),
                  blk(2, 2 * SSM_BLK_IN)],
        out_specs=o_spec,
        out_shape=jax.ShapeDtypeStruct((bsz, N_SUBSEQ, lsub, width), BF16),
        scratch_shapes=scratch,
        compiler_params=_cparams(sem),
        name="s5_pass2",
    )(p4, fin, bdb, avec, bdc, glu, vec)
    return out.reshape(bsz, seqlen, width)


def _layer_norm_rows(y, g, b):
    mu = jnp.mean(y, axis=-1, keepdims=True)
    yc = y - mu
    var = jnp.mean(yc * yc, axis=-1, keepdims=True)
    return yc * lax.rsqrt(var + EPS) * g + b


def _pack_bf16_pairs(lo, hi):
    lo_bits = lax.bitcast_convert_type(lo.astype(BF16).astype(F32), jnp.uint32)
    hi_bits = lax.bitcast_convert_type(hi.astype(BF16).astype(F32), jnp.uint32)
    return (lo_bits >> 16) | (hi_bits & jnp.uint32(0xFFFF0000))


def _unpack_bf16_pairs(w):
    lo = lax.bitcast_convert_type(w << 16, F32)
    hi = lax.bitcast_convert_type(w & jnp.uint32(0xFFFF0000), F32)
    return lo, hi


def _outproj_kernel(attn_ref, ssm_ref, gs_ref, w_ref, x_ref, mod_ref, ln_ref, wr_ref, br_ref,
                    x1_ref, hp_ref, logit_ref, ybuf, *, nc):
    m = mod_ref[...]
    ssm = ssm_ref[...].astype(F32)
    ms = jnp.mean(ssm * ssm, axis=-1, keepdims=True)
    ssm_n = (ssm * lax.rsqrt(ms + EPS) * gs_ref[...]).astype(BF16)
    attn = attn_ref[...]
    ka = attn.shape[1]
    d = x_ref.shape[1]
    for n0 in range(0, d, nc):
        mix = (jnp.dot(attn, w_ref[:ka, n0:n0 + nc], preferred_element_type=F32)
               + jnp.dot(ssm_n, w_ref[ka:, n0:n0 + nc], preferred_element_type=F32))
        ybuf[:, n0:n0 + nc] = DN_ALPHA * x_ref[:, n0:n0 + nc] + (1.0 + m[2:3, n0:n0 + nc]) * mix
    x1 = _layer_norm_rows(ybuf[...], ln_ref[0:1, :], ln_ref[1:2, :])
    x1_ref[...] = x1
    h2 = x1 * (1.0 + m[4:5, :]) + m[3:4, :]
    hp_ref[...] = _pack_bf16_pairs(h2[:, :d // 2], h2[:, d // 2:])
    hi = h2.astype(BF16)
    lo = (h2 - hi.astype(F32)).astype(BF16)
    tm = h2.shape[0]
    r = jnp.dot(jnp.concatenate([hi, lo], axis=0), wr_ref[...], preferred_element_type=F32)
    logit_ref[...] = r[:tm, :N_EXPERTS] + r[:tm, N_EXPERTS:] + r[tm:, :N_EXPERTS] + br_ref[...]


def _route_kernel(logit_ref, idx_ref, gate_ref, rank_ref, cnt_ref):
    logits = logit_ref[...]
    tm = logits.shape[0]
    lane = lax.broadcasted_iota(jnp.int32, logits.shape, 1)
    vals, idxs = [], []
    for _ in range(TOP_K):
        mx = jnp.max(logits, axis=-1, keepdims=True)
        ix = jnp.min(jnp.where(logits == mx, lane, N_EXPERTS), axis=-1, keepdims=True)
        vals.append(mx)
        idxs.append(ix)
        logits = jnp.where(lane == ix, -jnp.inf, logits)
    tv = jnp.concatenate(vals, axis=1)
    e = jnp.exp(tv - vals[0])
    gate_ref[...] = e / jnp.sum(e, axis=-1, keepdims=True)
    idx_ref[...] = jnp.concatenate(idxs, axis=1)
    tri = (lax.broadcasted_iota(jnp.int32, (tm, tm), 0) > lax.broadcasted_iota(jnp.int32, (tm, tm), 1)).astype(BF16)
    run = jnp.zeros((1, N_EXPERTS), F32)
    ranks = []
    for ix in idxs:
        onehot = (lane == ix).astype(F32)
        before = jnp.dot(tri, onehot.astype(BF16), preferred_element_type=F32) + run
        ranks.append(jnp.sum(onehot * before, axis=-1, keepdims=True))
        run = run + jnp.sum(onehot, axis=0, keepdims=True)
    rank_ref[...] = jnp.concatenate(ranks, axis=1).astype(jnp.int32)
    cnt_ref[...] = jnp.zeros_like(cnt_ref)
    cnt_ref[0:1, 0:N_EXPERTS] = run


def _out_proj(attn_n, ssm, g_ssm, w_out_bf, x, mod3, ln1, wr, br, tm=128):
    bsz, seqlen, d = x.shape
    ka = attn_n.shape[-1]
    ks = w_out_bf.shape[0] - ka
    row = lambda b, i: (b, i, 0)
    const = lambda b, i: (0, 0)
    return pl.pallas_call(
        functools.partial(_outproj_kernel, nc=512),
        grid=(bsz, seqlen // tm),
        in_specs=[pl.BlockSpec((None, tm, ka), row),
                  pl.BlockSpec((None, tm, ks), row),
                  pl.BlockSpec((1, ks), const),
                  _resident((ka + ks, d), const),
                  pl.BlockSpec((None, tm, d), row),
                  pl.BlockSpec((None, 6, d), lambda b, i: (b, 0, 0)),
                  pl.BlockSpec((2, d), const),
                  pl.BlockSpec((d, 2 * N_EXPERTS), const),
                  pl.BlockSpec((1, N_EXPERTS), const)],
        out_specs=[pl.BlockSpec((None, tm, d), row),
                   pl.BlockSpec((None, tm, d // 2), row),
                   pl.BlockSpec((None, tm, N_EXPERTS), row)],
        out_shape=[jax.ShapeDtypeStruct((bsz, seqlen, d), F32),
                   jax.ShapeDtypeStruct((bsz, seqlen, d // 2), jnp.uint32),
                   jax.ShapeDtypeStruct((bsz, seqlen, N_EXPERTS), F32)],
        scratch_shapes=[pltpu.VMEM((tm, d), F32)],
        compiler_params=_cparams(("arbitrary", "arbitrary")),
        name="out_proj_ln_router",
    )(attn_n, ssm, g_ssm.reshape(1, ks), w_out_bf, x, mod3, ln1, wr, br)


def _route(logits, tm=ROUTE_TILE):
    n_tok = logits.shape[0]
    nt = n_tok // tm
    row = lambda i: (i, 0)
    k_shape = lambda dt: jax.ShapeDtypeStruct((n_tok, TOP_K), dt)
    return pl.pallas_call(
        _route_kernel,
        grid=(nt,),
        in_specs=[pl.BlockSpec((tm, N_EXPERTS), row)],
        out_specs=[pl.BlockSpec((tm, TOP_K), row), pl.BlockSpec((tm, TOP_K), row), pl.BlockSpec((tm, TOP_K), row),
                   pl.BlockSpec((None, SUBLANES, LANES), lambda i: (i, 0, 0))],
        out_shape=[k_shape(jnp.int32), k_shape(F32), k_shape(jnp.int32),
                   jax.ShapeDtypeStruct((nt, SUBLANES, LANES), F32)],
        compiler_params=_cparams(("arbitrary",)),
        name="route_topk",
    )(logits)


def _stream_expert_weights(blk_e, first, nxt, last, w_hbm, stage, wbf, sem):
    ct = pl.program_id(0)
    rb = pl.program_id(1)
    width = wbf[0].shape[1]

    def copies(e, col_tile):
        c0 = pl.multiple_of(col_tile * width, width)
        return [pltpu.make_async_copy(w.at[e, :, pl.ds(c0, width)], s, sem.at[j])
                for j, (w, s) in enumerate(zip(w_hbm, stage))]

    @pl.when((ct == 0) & (rb == 0))
    def _():
        for cp in copies(blk_e[0], 0):
            cp.start(priority=1)

    @pl.when(first[rb] == 1)
    def _():
        for cp in copies(blk_e[rb], ct):
            cp.wait()
        def convert(c, carry):
            r0 = pl.multiple_of(c * CAST_ROWS, CAST_ROWS)
            for s, w in zip(stage, wbf):
                w[pl.ds(r0, CAST_ROWS), :] = s[pl.ds(r0, CAST_ROWS), :].astype(BF16)
            return carry
        lax.fori_loop(0, stage[0].shape[0] // CAST_ROWS, convert, 0)

        @pl.when(last[rb] == 0)
        def _():
            for cp in copies(nxt[rb], ct):
                cp.start(priority=1)

        @pl.when((last[rb] == 1) & (ct + 1 < pl.num_programs(0)))
        def _():
            for cp in copies(nxt[rb], ct + 1):
                cp.start(priority=1)


def _expert_up_kernel(blk_e, first, nreal, nxt, last, xs_ref, wg_hbm, wu_hbm, bg_ref, bu_ref, act_ref,
                      stg_g, stg_u, wgb, wub, sem):
    rb = pl.program_id(1)
    _stream_expert_weights(blk_e, first, nxt, last, (wg_hbm, wu_hbm), (stg_g, stg_u), (wgb, wub), sem)

    @pl.when(rb < nreal[0])
    def _():
        x = jnp.concatenate(_unpack_bf16_pairs(xs_ref[...]), axis=1).astype(BF16)
        g = jnp.dot(x, wgb[...], preferred_element_type=F32) + bg_ref[...]
        up = jnp.dot(x, wub[...], preferred_element_type=F32) + bu_ref[...]
        g = jnp.minimum(g, SWIGLU_LIMIT)
        up = jnp.clip(up, -SWIGLU_LIMIT, SWIGLU_LIMIT)
        act_ref[...] = (g * jax.nn.sigmoid(SWIGLU_ALPHA * g) * (up + 1.0)).astype(BF16)

    @pl.when(rb >= nreal[0])
    def _():
        act_ref[...] = jnp.zeros_like(act_ref)


def _expert_down_kernel(blk_e, first, nreal, nxt, last, act_ref, wd_hbm, bd_ref, y_ref, stg, wdb, sem):
    rb = pl.program_id(1)
    _stream_expert_weights(blk_e, first, nxt, last, (wd_hbm,), (stg,), (wdb,), sem)

    @pl.when(rb < nreal[0])
    def _():
        y = jnp.dot(act_ref[...], wdb[...], preferred_element_type=F32) + bd_ref[...]
        half = y.shape[1] // 2
        y_ref[...] = _pack_bf16_pairs(y[:, :half], y[:, half:])

    @pl.when(rb >= nreal[0])
    def _():
        y_ref[...] = jnp.zeros_like(y_ref)


def _experts(xs, sched, w_gate, b_gate, w_up, b_up, w_down, b_down, tf=512, tn=DOWN_TILE):
    cap = xs.shape[0]
    n_e, d, dff = w_gate.shape
    nblk = cap // EXPERT_BLK
    hbm = pl.BlockSpec(memory_space=pl.ANY)
    act = pl.pallas_call(
        _expert_up_kernel,
        grid_spec=pltpu.PrefetchScalarGridSpec(
            num_scalar_prefetch=5,
            grid=(dff // tf, nblk),
            in_specs=[pl.BlockSpec((EXPERT_BLK, d // 2), lambda f, r, be, *_: (r, 0)),
                      hbm, hbm,
                      pl.BlockSpec((None, 1, tf), lambda f, r, be, *_: (be[r], 0, f)),
                      pl.BlockSpec((None, 1, tf), lambda f, r, be, *_: (be[r], 0, f))],
            out_specs=pl.BlockSpec((EXPERT_BLK, tf), lambda f, r, be, *_: (r, f)),
            scratch_shapes=[pltpu.VMEM((d, tf), F32), pltpu.VMEM((d, tf), F32),
                            pltpu.VMEM((d, tf), BF16), pltpu.VMEM((d, tf), BF16),
                            pltpu.SemaphoreType.DMA((2,))]),
        out_shape=jax.ShapeDtypeStruct((cap, dff), BF16),
        compiler_params=_cparams(("arbitrary", "arbitrary")),
        name="expert_gate_up",
    )(*sched, xs, w_gate, w_up, b_gate.reshape(n_e, 1, dff), b_up.reshape(n_e, 1, dff))
    ys = pl.pallas_call(
        _expert_down_kernel,
        grid_spec=pltpu.PrefetchScalarGridSpec(
            num_scalar_prefetch=5,
            grid=(d // tn, nblk),
            in_specs=[pl.BlockSpec((EXPERT_BLK, dff), lambda n, r, be, *_: (r, 0)),
                      hbm,
                      pl.BlockSpec((None, 1, tn), lambda n, r, be, *_: (be[r], 0, n))],
            out_specs=pl.BlockSpec((EXPERT_BLK, tn // 2), lambda n, r, be, *_: (r, n)),
            scratch_shapes=[pltpu.VMEM((dff, tn), F32), pltpu.VMEM((dff, tn), BF16),
                            pltpu.SemaphoreType.DMA((1,))]),
        out_shape=jax.ShapeDtypeStruct((cap, d // 2), jnp.uint32),
        compiler_params=_cparams(("arbitrary", "arbitrary")),
        name="expert_down",
    )(*sched, act, w_down, b_down.reshape(n_e, 1, d))
    return ys


def _route_tables(cnt, top_idx, rank):
    n_assign = top_idx.size
    counts = cnt[:, 0, :N_EXPERTS].astype(jnp.int32)
    tot = jnp.sum(counts, axis=0)
    padded = (tot + EXPERT_BLK - 1) // EXPERT_BLK * EXPERT_BLK
    pend = jnp.cumsum(padded)
    pstart = pend - padded
    base = pstart[None, :] + jnp.cumsum(counts, axis=0) - counts
    experts = jnp.arange(N_EXPERTS, dtype=jnp.int32)
    idx_t = top_idx.reshape(-1, ROUTE_TILE * TOP_K)
    dest = jnp.sum(jnp.where(idx_t[..., None] == experts, base[:, None, :], 0), axis=-1)
    dest = (dest.reshape(-1) + rank.reshape(-1)).astype(jnp.int32)
    cap = ((n_assign + EXPERT_BLK - 1) // EXPERT_BLK) * EXPERT_BLK + N_EXPERTS * EXPERT_BLK
    nblk = cap // EXPERT_BLK
    blk = jnp.arange(nblk, dtype=jnp.int32)
    blk_e = jnp.sum((pend[None, :] <= blk[:, None] * EXPERT_BLK).astype(jnp.int32), axis=1)
    blk_e = jnp.minimum(blk_e, N_EXPERTS - 1).astype(jnp.int32)
    nreal = (pend[-1:] // EXPERT_BLK).astype(jnp.int32)
    real = blk < nreal[0]
    first = real & jnp.concatenate([jnp.ones((1,), bool), blk_e[1:] != blk_e[:-1]])
    starts = jnp.where(first, blk, nblk)
    nxt_blk = jnp.concatenate([lax.cummin(starts, reverse=True)[1:], jnp.full((1,), nblk, jnp.int32)])
    last = nxt_blk >= nblk
    nxt = jnp.where(last, blk_e[0], blk_e[jnp.minimum(nxt_blk, nblk - 1)]).astype(jnp.int32)
    sched = (blk_e, first.astype(jnp.int32), nreal, nxt, last.astype(jnp.int32))
    shift = base % GATHER_CHUNK
    span = jnp.where(counts > 0, (counts + shift + GATHER_CHUNK - 1) // GATHER_CHUNK * GATHER_CHUNK, 0)
    off = jnp.cumsum(span, axis=1) - span
    max_pieces = ROUTE_TILE * TOP_K // GATHER_CHUNK + 2 * N_EXPERTS
    piece = jnp.arange(max_pieces, dtype=jnp.int32)[None, :, None] * GATHER_CHUNK
    in_run = (piece >= off[:, None, :]) & (piece < (off + span)[:, None, :])
    piece_src = jnp.sum(jnp.where(in_run, (base - shift - off)[:, None, :] + piece, 0), axis=-1)
    n_piece = jnp.sum(span, axis=1) // GATHER_CHUNK
    runs = (n_piece.astype(jnp.int32), piece_src.reshape(-1).astype(jnp.int32),
            (off + shift).astype(jnp.int32)[:, None, :])
    return dest, cap, sched, (pstart + tot).astype(jnp.int32), (padded - tot).astype(jnp.int32), runs


def _row_copy(src, src_row, dst, dst_row, sem):
    return pltpu.make_async_copy(src.at[pl.ds(src_row, 1)], dst.at[pl.ds(dst_row, 1)], sem)


def _dispatch_kernel(pad0_ref, padn_ref, nreal_ref, hp_ref, dest_ref, xs_ref, zblk, sem, zsem, bsem):
    i = pl.program_id(0)
    tm = hp_ref.shape[0]
    nblk = xs_ref.shape[0] // EXPERT_BLK

    def for_each_pad_row(fn):
        def per_expert(e, c):
            def per_row(r, c2):
                fn(_row_copy(zblk, 0, xs_ref, pad0_ref[e] + r, zsem))
                return c2
            return lax.fori_loop(0, padn_ref[e], per_row, c)
        lax.fori_loop(0, N_EXPERTS, per_expert, 0)

    def for_each_unused_block(fn):
        def per_block(b, c):
            r0 = pl.multiple_of(b * EXPERT_BLK, EXPERT_BLK)
            fn(pltpu.make_async_copy(zblk, xs_ref.at[pl.ds(r0, EXPERT_BLK)], bsem))
            return c
        lax.fori_loop(nreal_ref[0], nblk, per_block, 0)

    @pl.when(i == 0)
    def _():
        zblk[...] = jnp.zeros_like(zblk)
        for_each_pad_row(lambda cp: cp.start())
        for_each_unused_block(lambda cp: cp.start())

    def issue(r, c):
        for k in range(TOP_K):
            _row_copy(hp_ref, r, xs_ref, dest_ref[r * TOP_K + k], sem).start(priority=k % 2)
        return c
    lax.fori_loop(0, tm, issue, 0, unroll=2)

    def drain(r, c):
        for k in range(TOP_K):
            _row_copy(hp_ref, 0, xs_ref, 0, sem).wait()
        return c
    lax.fori_loop(0, tm, drain, 0)

    @pl.when(i == 0)
    def _():
        for_each_pad_row(lambda cp: cp.wait())
        for_each_unused_block(lambda cp: cp.wait())


def _dispatch(hp, dest, pad0, padn, nreal, cap, tm=MOVE_TILE):
    n_tok, half = hp.shape
    return pl.pallas_call(
        _dispatch_kernel,
        grid_spec=pltpu.PrefetchScalarGridSpec(
            num_scalar_prefetch=3,
            grid=(n_tok // tm,),
            in_specs=[pl.BlockSpec((tm, half), lambda i, *_: (i, 0)),
                      pl.BlockSpec((tm * TOP_K,), lambda i, *_: (i,), memory_space=pltpu.SMEM)],
            out_specs=pl.BlockSpec(memory_space=pl.ANY),
            scratch_shapes=[pltpu.VMEM((EXPERT_BLK, half), jnp.uint32),
                            pltpu.SemaphoreType.DMA, pltpu.SemaphoreType.DMA, pltpu.SemaphoreType.DMA]),
        out_shape=jax.ShapeDtypeStruct((cap, half), jnp.uint32),
        compiler_params=_cparams(("arbitrary",)),
        name="moe_dispatch",
    )(pad0, padn, nreal, hp, dest)


def _combine_kernel(npiece_ref, src_ref, idx_ref, rank_ref, gate_ref, runrow_ref, x1_ref, mod_ref, ln_ref,
                    ys_ref, o_ref, buf, sem, *, tn):
    i = pl.program_id(0)
    tm = x1_ref.shape[0]
    rows = buf.shape[1]

    max_pieces = rows // GATHER_CHUNK

    def for_each_piece(tile, fn):
        slot = tile % 2

        def per_piece(p, c):
            src = pl.multiple_of(src_ref[tile * max_pieces + p], GATHER_CHUNK)
            dst = pl.multiple_of(p * GATHER_CHUNK, GATHER_CHUNK)
            fn(pltpu.make_async_copy(ys_ref.at[pl.ds(src, GATHER_CHUNK)],
                                     buf.at[slot, pl.ds(dst, GATHER_CHUNK)], sem.at[slot]))
            return c
        lax.fori_loop(0, npiece_ref[tile], per_piece, 0)

    @pl.when(i == 0)
    def _():
        buf[...] = jnp.zeros_like(buf)
        for_each_piece(0, lambda cp: cp.start())

    @pl.when(i + 1 < pl.num_programs(0))
    def _():
        for_each_piece(i + 1, lambda cp: cp.start())

    for_each_piece(i, lambda cp: cp.wait())

    lane = lax.broadcasted_iota(jnp.int32, (tm, N_EXPERTS), 1)
    col = lax.broadcasted_iota(jnp.int32, (tm, rows), 1)
    sel = jnp.zeros((tm, rows), F32)
    for k in range(TOP_K):
        run_row = jnp.sum(jnp.where(lane == idx_ref[:, k:k + 1], runrow_ref[...], 0), axis=-1, keepdims=True)
        sel = sel + jnp.where(col == run_row + rank_ref[:, k:k + 1], gate_ref[:, k:k + 1], 0.0)
    sel = sel.astype(BF16)

    cur = buf.at[i % 2]
    m = mod_ref[...]
    d = x1_ref.shape[1]
    hw = tn // 2
    cw = 512
    for w0 in range(0, d // 2, cw):
        wl, wh = _unpack_bf16_pairs(cur[:, w0:w0 + cw])
        lo = jnp.dot(sel, wl.astype(BF16), preferred_element_type=F32)
        hi = jnp.dot(sel, wh.astype(BF16), preferred_element_type=F32)
        for half, moe in ((0, lo), (1, hi)):
            c0 = (w0 // hw) * tn + half * hw + w0 % hw
            o_ref[:, c0:c0 + cw] = DN_ALPHA * x1_ref[:, c0:c0 + cw] + (1.0 + m[5:6, c0:c0 + cw]) * moe
    o_ref[...] = _layer_norm_rows(o_ref[...], ln_ref[0:1, :], ln_ref[1:2, :])


def _combine(ys, runs, top_idx, rank, gates, x1, mod3, ln2, tm=ROUTE_TILE, tn=DOWN_TILE):
    bsz, seqlen, d = x1.shape
    n_tok = bsz * seqlen
    per_b = seqlen // tm
    n_piece, piece_src, run_row = runs
    buf_rows = tm * TOP_K + N_EXPERTS * 2 * GATHER_CHUNK
    assert piece_src.shape[0] * GATHER_CHUNK == (n_tok // tm) * buf_rows
    tok = lambda i, *_: (i, 0)
    out = pl.pallas_call(
        functools.partial(_combine_kernel, tn=tn),
        grid_spec=pltpu.PrefetchScalarGridSpec(
            num_scalar_prefetch=2,
            grid=(n_tok // tm,),
            in_specs=[pl.BlockSpec((tm, TOP_K), tok), pl.BlockSpec((tm, TOP_K), tok), pl.BlockSpec((tm, TOP_K), tok),
                      pl.BlockSpec((None, 1, N_EXPERTS), lambda i, *_: (i, 0, 0)),
                      pl.BlockSpec((tm, d), tok),
                      pl.BlockSpec((None, 6, d), lambda i, *_: (i // per_b, 0, 0)),
                      pl.BlockSpec((2, d), lambda i, *_: (0, 0)),
                      pl.BlockSpec(memory_space=pl.ANY)],
            out_specs=pl.BlockSpec((tm, d), tok),
            scratch_shapes=[pltpu.VMEM((2, buf_rows, d // 2), jnp.uint32), pltpu.SemaphoreType.DMA((2,))]),
        out_shape=jax.ShapeDtypeStruct((n_tok, d), F32),
        compiler_params=_cparams(("arbitrary",)),
        name="moe_combine_ln",
    )(n_piece, piece_src, top_idx, rank, gates, run_row, x1.reshape(n_tok, d), mod3, ln2, ys)
    return out.reshape(bsz, seqlen, d)


def kernel(x, c, positions, w_ada, b_ada, w_in, attn_sinks, ssm_a_re, ssm_a_im, ssm_b_re, ssm_b_im,
           ssm_c_re, ssm_c_im, ssm_d, ssm_log_dt, ssm_w_glu, ssm_b_glu, g_attn_out, g_ssm_out, w_out,
           ln1_g, ln1_b, w_router, b_router, w_gate, b_gate, w_up, b_up, w_down, b_down, ln2_g, ln2_b):
    bsz, seqlen, d = x.shape
    lsub = seqlen // N_SUBSEQ
    n_tok = bsz * seqlen
    rope_tab = _rope_tables(positions)
    for l in range(w_ada.shape[0]):
        mod3 = _ada_mod(c, w_ada[l], b_ada[l]).reshape(bsz, 6, d)
        proj = _in_proj(x, mod3, rope_tab, w_in[l].astype(BF16))
        attn_n = _attention(proj, attn_sinks[l].astype(F32), g_attn_out[l].astype(F32))
        s5p = _s5_params(ssm_a_re[l], ssm_a_im[l], ssm_b_re[l], ssm_b_im[l], ssm_c_re[l], ssm_c_im[l],
                         ssm_d[l], ssm_log_dt[l], ssm_w_glu[l], ssm_b_glu[l], lsub)
        ssm = _s5(proj, s5p)
        wr_hi = w_router[l].astype(BF16)
        wr_lo = (w_router[l] - wr_hi.astype(F32)).astype(BF16)
        x1, hp, logits = _out_proj(
            attn_n, ssm, g_ssm_out[l].astype(F32), w_out[l].astype(BF16), x, mod3,
            jnp.stack([ln1_g[l], ln1_b[l]]).astype(F32),
            jnp.concatenate([wr_hi, wr_lo], axis=1), b_router[l].reshape(1, N_EXPERTS).astype(F32))
        top_idx, gates, rank, cnt = _route(logits.reshape(n_tok, N_EXPERTS))
        dest, cap, sched, pad0, padn, runs = _route_tables(cnt, top_idx, rank)
        xs = _dispatch(hp.reshape(n_tok, d // 2), dest, pad0, padn, sched[2], cap)
        ys = _experts(xs, sched, w_gate[l], b_gate[l], w_up[l], b_up[l], w_down[l], b_down[l])
        x = _combine(ys, runs, top_idx, rank, gates, x1, mod3, jnp.stack([ln2_g[l], ln2_b[l]]).astype(F32))
    return x
```

```python
import functools
import math

import jax
import jax.numpy as jnp
from jax import lax
from jax.experimental import pallas as pl
from jax.experimental.pallas import tpu as pltpu

F32 = jnp.float32
BF16 = jnp.bfloat16

HEAD_DIM = 64
N_Q_HEADS = 32
N_KV_HEADS = 4
GQ = N_Q_HEADS // N_KV_HEADS
ATTN_WIDTH = N_Q_HEADS * HEAD_DIM
KV_WIDTH = N_KV_HEADS * HEAD_DIM
QKV_WIDTH = ATTN_WIDTH + 2 * KV_WIDTH
BLK = 128
ROT_DIM = HEAD_DIM // 4
ROPE_THETA = 500000.0
GROUP_CH = 16
STATE = 64
N_EXPERTS = 32
TOP_K = 4
SWIGLU_LIMIT = 7.0
SWIGLU_ALPHA = 1.702
EXPERT_BLK = 256
ROUTE_TILE = 256
GATHER_CHUNK = 8
MOVE_TILE = 256
DOWN_TILE = 4096
S5_STEPS = 512
CAST_ROWS = 128
DEPTH = 1
DN_ALPHA = (2.0 * DEPTH) ** 0.25
EPS = 1e-5

LANES = 128
SUBLANES = 8
N_SUBSEQ = SUBLANES
GROUPS_PER_BLK = 16
SSM_BLK_IN = GROUPS_PER_BLK * GROUP_CH
SSM_BLK_ST = GROUPS_PER_BLK * STATE
VMEM_LIMIT = 56 * 1024 * 1024


def _cparams(sem, vmem=VMEM_LIMIT):
    return pltpu.CompilerParams(dimension_semantics=sem, vmem_limit_bytes=vmem)


def _resident(shape, index_map):
    return pl.BlockSpec(shape, index_map, pipeline_mode=pl.Buffered(1))


def _ada_kernel(c_ref, w_ref, b_ref, o_ref):
    c = c_ref[...]
    ca = c * jax.nn.sigmoid(c)
    o_ref[...] = jnp.dot(ca.astype(BF16), w_ref[...].astype(BF16),
                         preferred_element_type=F32) + b_ref[...]


def _ada_mod(c, w_ada, b_ada, tn=1024):
    bsz, d = c.shape
    n = w_ada.shape[1]
    c8 = jnp.zeros((SUBLANES, d), F32).at[:bsz].set(c)
    out = pl.pallas_call(
        _ada_kernel,
        grid=(n // tn,),
        in_specs=[pl.BlockSpec((SUBLANES, d), lambda j: (0, 0)),
                  pl.BlockSpec((d, tn), lambda j: (0, j)),
                  pl.BlockSpec((1, tn), lambda j: (0, j))],
        out_specs=pl.BlockSpec((SUBLANES, tn), lambda j: (0, j)),
        out_shape=jax.ShapeDtypeStruct((SUBLANES, n), F32),
        compiler_params=_cparams(("arbitrary",)),
        name="ada_mod",
    )(c8, w_ada, b_ada.reshape(1, n))
    return out[:bsz]


def _rope(t, tab):
    c, s_lo, s_hi = tab[:, :LANES], tab[:, LANES:2 * LANES], tab[:, 2 * LANES:]
    half = ROT_DIM // 2
    out = []
    for j in range(t.shape[1] // LANES):
        tj = t[:, j * LANES:(j + 1) * LANES]
        out.append(tj * c + pltpu.roll(tj, LANES - half, 1) * s_lo + pltpu.roll(tj, half, 1) * s_hi)
    return jnp.concatenate(out, axis=1)


def _inproj_kernel(x_ref, mod_ref, tab_ref, w_ref, o_ref, *, nc):
    m = mod_ref[...]
    h = (x_ref[...] * (1.0 + m[1:2, :]) + m[0:1, :]).astype(BF16)
    for n0 in range(0, o_ref.shape[-1], nc):
        p = jnp.dot(h, w_ref[:, n0:n0 + nc], preferred_element_type=F32)
        if n0 < ATTN_WIDTH:
            p = _rope(p * (HEAD_DIM ** -0.5), tab_ref[...])
        elif n0 == ATTN_WIDTH:
            p = jnp.concatenate([_rope(p[:, :KV_WIDTH], tab_ref[...]), p[:, KV_WIDTH:]], axis=1)
        o_ref[:, n0:n0 + nc] = p.astype(BF16)


def _in_proj(x, mod3, rope_tab, w_in_bf, tm=128, nc=512):
    bsz, seqlen, d = x.shape
    n_in = w_in_bf.shape[1]
    assert ATTN_WIDTH % nc == 0 and nc >= 2 * KV_WIDTH
    return pl.pallas_call(
        functools.partial(_inproj_kernel, nc=nc),
        grid=(bsz, seqlen // tm),
        in_specs=[pl.BlockSpec((None, tm, d), lambda b, i: (b, i, 0)),
                  pl.BlockSpec((None, 6, d), lambda b, i: (b, 0, 0)),
                  pl.BlockSpec((None, tm, 3 * LANES), lambda b, i: (b, i, 0)),
                  _resident((d, n_in), lambda b, i: (0, 0))],
        out_specs=pl.BlockSpec((None, tm, n_in), lambda b, i: (b, i, 0)),
        out_shape=jax.ShapeDtypeStruct((bsz, seqlen, n_in), BF16),
        compiler_params=_cparams(("arbitrary", "arbitrary")),
        name="in_proj",
    )(x, mod3, rope_tab, w_in_bf)


def _attn_kernel(sink_ref, q_ref, kc_ref, kp_ref, vc_ref, vp_ref, g_ref, o_ref):
    n = pl.program_id(1)
    low = lax.broadcasted_iota(jnp.int32, (2 * BLK, LANES), 1) < HEAD_DIM
    low_q = lax.broadcasted_iota(jnp.int32, (BLK, LANES), 1) < HEAD_DIM

    k_raw = jnp.concatenate([kp_ref[...], kc_ref[...]], axis=0).astype(F32)
    v_raw = jnp.concatenate([vp_ref[...], vc_ref[...]], axis=0).astype(F32)

    qi = lax.broadcasted_iota(jnp.int32, (BLK, BLK), 0)
    kj = lax.broadcasted_iota(jnp.int32, (BLK, BLK), 1)
    own = kj <= qi
    prev_ok = kj >= jnp.where(n > 0, 0, BLK)

    o_chunks = []
    for hk in range(N_KV_HEADS):
        kc = k_raw[:, (hk // 2) * LANES:(hk // 2 + 1) * LANES]
        vc = v_raw[:, (hk // 2) * LANES:(hk // 2 + 1) * LANES]
        k_sw = pltpu.roll(kc, HEAD_DIM, 1)
        v_sw = pltpu.roll(vc, HEAD_DIM, 1)
        if hk % 2 == 0:
            kk2 = jnp.where(low, kc, k_sw)
            v_lo = jnp.where(low, vc, 0.0)
            v_hi = jnp.where(low, 0.0, v_sw)
        else:
            kk2 = jnp.where(low, k_sw, kc)
            v_lo = jnp.where(low, v_sw, 0.0)
            v_hi = jnp.where(low, 0.0, vc)
        kk2 = kk2.astype(BF16)
        v_lo = v_lo.astype(BF16)
        v_hi = v_hi.astype(BF16)
        lhs = []
        for j in range(GQ // 2):
            c0 = (hk * (GQ // 2) + j) * LANES
            q2 = q_ref[:, c0:c0 + LANES].astype(F32)
            lhs.append(jnp.where(low_q, q2, 0.0).astype(BF16))
            lhs.append(jnp.where(low_q, 0.0, q2).astype(BF16))
        s_all = lax.dot_general(jnp.concatenate(lhs, axis=0), kk2,
                                (((1,), (1,)), ((), ())), preferred_element_type=F32)
        for j in range(GQ // 2):
            acc = None
            for side, vv in ((0, v_lo), (1, v_hi)):
                i = 2 * j + side
                s_prev = jnp.where(prev_ok, s_all[i * BLK:(i + 1) * BLK, :BLK], -1e30)
                s = jnp.where(own, s_all[i * BLK:(i + 1) * BLK, BLK:], s_prev)
                sink = sink_ref[hk * GQ + i]
                m = jnp.maximum(jnp.max(s, axis=-1, keepdims=True), sink)
                p = jnp.exp(s - m)
                denom = jnp.sum(p, axis=-1, keepdims=True) + jnp.exp(sink - m)
                p = p * (1.0 / denom)
                p = jnp.concatenate([jnp.where(own, 0.0, p), jnp.where(own, p, 0.0)], axis=1).astype(BF16)
                o = jnp.dot(p, vv, preferred_element_type=F32)
                acc = o if acc is None else acc + o
            o_chunks.append(acc)

    ssq = None
    for oc in o_chunks:
        t = jnp.sum(oc * oc, axis=-1, keepdims=True)
        ssq = t if ssq is None else ssq + t
    inv = lax.rsqrt(ssq * (1.0 / ATTN_WIDTH) + EPS)
    for j, oc in enumerate(o_chunks):
        o_ref[:, j * LANES:(j + 1) * LANES] = (oc * inv * g_ref[:, j * LANES:(j + 1) * LANES]).astype(BF16)


def _attention(qkv, sinks, g_attn):
    bsz, seqlen, _ = qkv.shape
    nb = seqlen // BLK
    kcol = ATTN_WIDTH // KV_WIDTH
    cur = lambda b, n: (b, n, 0)
    return pl.pallas_call(
        _attn_kernel,
        grid=(bsz, nb),
        in_specs=[pl.BlockSpec(memory_space=pltpu.SMEM),
                  pl.BlockSpec((None, BLK, ATTN_WIDTH), cur),
                  pl.BlockSpec((None, BLK, KV_WIDTH), lambda b, n: (b, n, kcol)),
                  pl.BlockSpec((None, BLK, KV_WIDTH), lambda b, n: (b, jnp.maximum(n - 1, 0), kcol)),
                  pl.BlockSpec((None, BLK, KV_WIDTH), lambda b, n: (b, n, kcol + 1)),
                  pl.BlockSpec((None, BLK, KV_WIDTH), lambda b, n: (b, jnp.maximum(n - 1, 0), kcol + 1)),
                  pl.BlockSpec((1, ATTN_WIDTH), lambda b, n: (0, 0))],
        out_specs=pl.BlockSpec((None, BLK, ATTN_WIDTH), cur),
        out_shape=jax.ShapeDtypeStruct((bsz, seqlen, ATTN_WIDTH), BF16),
        compiler_params=_cparams(("arbitrary", "arbitrary")),
        name="swa_attention",
    )(sinks, qkv, qkv, qkv, qkv, qkv, g_attn.reshape(1, ATTN_WIDTH))


def _rope_tables(positions):
    half = ROT_DIM // 2
    inv_freq = ROPE_THETA ** (-jnp.arange(0, ROT_DIM, 2, dtype=F32) / ROT_DIM)
    ang = positions.astype(F32)[..., None] * inv_freq
    cos, sin = jnp.cos(ang), jnp.sin(ang)
    shp = cos.shape[:-1] + (HEAD_DIM - ROT_DIM,)
    c = jnp.concatenate([cos, cos, jnp.ones(shp, F32)], axis=-1)
    z8 = jnp.zeros_like(sin)
    s_lo = jnp.concatenate([-sin, z8, jnp.zeros(shp, F32)], axis=-1)
    s_hi = jnp.concatenate([z8, sin, jnp.zeros(shp, F32)], axis=-1)
    rep = LANES // HEAD_DIM
    return jnp.concatenate([jnp.tile(c, rep), jnp.tile(s_lo, rep), jnp.tile(s_hi, rep)], axis=-1)


def _s5_scan(buf, ar, ai, hr, hi, ti, store):
    def step(i, carry):
        hr, hi = carry
        r0 = pl.multiple_of(i * SUBLANES, SUBLANES)
        row = buf[pl.ds(r0, SUBLANES), :]
        nhr = ar * hr - ai * hi + row[:, :SSM_BLK_ST]
        nhi = ar * hi + ai * hr + row[:, SSM_BLK_ST:]
        if store:
            buf[pl.ds(r0, SUBLANES), :] = jnp.concatenate([nhr, nhi], axis=1)
        return nhr, nhi
    return lax.fori_loop(0, ti, step, (hr, hi), unroll=4)


def _time_major(u_ref, ti):
    u = pltpu.einshape("jid->ijd", u_ref[...].astype(F32))
    return u.reshape(ti * N_SUBSEQ, u.shape[-1])


def _s5_pass1_kernel(u_ref, bdb_ref, a_ref, f_ref, buf, hst, *, ti):
    ic = pl.program_id(2)

    @pl.when(ic == 0)
    def _():
        hst[...] = jnp.zeros_like(hst)

    u = _time_major(u_ref, ti)
    buf[...] = jnp.dot(u.astype(BF16), bdb_ref[...], preferred_element_type=F32)
    ar = jnp.broadcast_to(a_ref[0:1, :], (SUBLANES, SSM_BLK_ST))
    ai = jnp.broadcast_to(a_ref[1:2, :], (SUBLANES, SSM_BLK_ST))
    hr, hi = _s5_scan(buf, ar, ai, hst[:, :SSM_BLK_ST], hst[:, SSM_BLK_ST:], ti, store=False)
    hst[...] = jnp.concatenate([hr, hi], axis=1)

    @pl.when(ic == pl.num_programs(2) - 1)
    def _():
        f_ref[...] = hst[...]


def _s5_pass2_kernel(u_ref, f_ref, bdb_ref, a_ref, bdc_ref, glu_ref, vec_ref, o_ref, buf, hst, *, ti):
    ic = pl.program_id(2)

    @pl.when(ic == 0)
    def _():
        fr, fi = f_ref[:, :SSM_BLK_ST], f_ref[:, SSM_BLK_ST:]
        pr = jnp.broadcast_to(a_ref[2:3, :], (SUBLANES, SSM_BLK_ST))
        pi = jnp.broadcast_to(a_ref[3:4, :], (SUBLANES, SSM_BLK_ST))
        row = lax.broadcasted_iota(jnp.int32, (SUBLANES, SSM_BLK_ST), 0)
        hr = jnp.zeros((SUBLANES, SSM_BLK_ST), F32)
        hi = jnp.zeros((SUBLANES, SSM_BLK_ST), F32)
        for _ in range(N_SUBSEQ - 1):
            nr = pr * hr - pi * hi + fr
            ni = pr * hi + pi * hr + fi
            hr = jnp.where(row == 0, 0.0, pltpu.roll(nr, 1, 0))
            hi = jnp.where(row == 0, 0.0, pltpu.roll(ni, 1, 0))
        hst[...] = jnp.concatenate([hr, hi], axis=1)

    u = _time_major(u_ref, ti)
    buf[...] = jnp.dot(u.astype(BF16), bdb_ref[...], preferred_element_type=F32)
    ar = jnp.broadcast_to(a_ref[0:1, :], (SUBLANES, SSM_BLK_ST))
    ai = jnp.broadcast_to(a_ref[1:2, :], (SUBLANES, SSM_BLK_ST))
    hr, hi = _s5_scan(buf, ar, ai, hst[:, :SSM_BLK_ST], hst[:, SSM_BLK_ST:], ti, store=True)
    hst[...] = jnp.concatenate([hr, hi], axis=1)

    y = jnp.dot(buf[...].astype(BF16), bdc_ref[...], preferred_element_type=F32)
    y = jax.nn.gelu(y + vec_ref[0:1, :SSM_BLK_IN] * u)
    z = jnp.dot(y.astype(BF16), glu_ref[...], preferred_element_type=F32) + vec_ref[1:2, :]
    out = z[:, :SSM_BLK_IN] * jax.nn.sigmoid(z[:, SSM_BLK_IN:])
    out = pltpu.einshape("ijd->jid", out.reshape(ti, N_SUBSEQ, SSM_BLK_IN))
    o_ref[...] = out.astype(BF16)


def _s5_params(a_re, a_im, b_re, b_im, c_re, c_im, d_skip, log_dt, w_glu, b_glu, lsub):
    g = a_re.shape[0]
    nf = g // GROUPS_PER_BLK
    a = lax.complex(a_re.astype(F32), a_im.astype(F32))
    dt = jnp.exp(log_dt.astype(F32))[:, None]
    a_bar = jnp.exp(a * dt)
    a_pow = jnp.exp(a * dt * lsub)
    b_bar = ((a_bar - 1.0) / a)[..., None] * lax.complex(b_re.astype(F32), b_im.astype(F32))
    def block_diag(m, inner):
        rows = m.shape[1]
        tile = jnp.tile(jnp.eye(inner, dtype=F32), (1, GROUPS_PER_BLK))
        wide = jnp.einsum('frk,kn->frn', m, tile, precision=lax.Precision.HIGHEST)
        rg = jnp.arange(rows, dtype=jnp.int32)[:, None] // (rows // GROUPS_PER_BLK)
        cg = jnp.arange(GROUPS_PER_BLK * inner, dtype=jnp.int32)[None, :] // inner
        return jnp.where(rg == cg, wide, 0.0)

    def bd_in(m):
        return block_diag(jnp.swapaxes(m, 1, 2).reshape(nf, SSM_BLK_IN, STATE), STATE)

    def bd_out(m):
        return block_diag(jnp.swapaxes(m, 1, 2).reshape(nf, SSM_BLK_ST, GROUP_CH), GROUP_CH)

    def bd_glu(m):
        return block_diag(m.reshape(nf, SSM_BLK_IN, GROUP_CH), GROUP_CH)

    bdb = jnp.concatenate([bd_in(jnp.real(b_bar)), bd_in(jnp.imag(b_bar))], axis=2).astype(BF16)
    bdc = jnp.concatenate([bd_out(c_re.astype(F32)), bd_out(-c_im.astype(F32))], axis=1).astype(BF16)
    wg = w_glu.astype(F32)
    glu = jnp.concatenate([bd_glu(wg[..., :GROUP_CH]), bd_glu(wg[..., GROUP_CH:])], axis=2).astype(BF16)
    flat = lambda m: m.reshape(nf, 1, SSM_BLK_ST)
    avec = jnp.concatenate([flat(jnp.real(a_bar)), flat(jnp.imag(a_bar)),
                            flat(jnp.real(a_pow)), flat(jnp.imag(a_pow))], axis=1)
    bg = b_glu.astype(F32).reshape(nf, GROUPS_PER_BLK, 2 * GROUP_CH)
    bvec = jnp.concatenate([bg[..., :GROUP_CH].reshape(nf, 1, SSM_BLK_IN),
                            bg[..., GROUP_CH:].reshape(nf, 1, SSM_BLK_IN)], axis=2)
    dvec = jnp.concatenate([d_skip.astype(F32).reshape(nf, 1, SSM_BLK_IN),
                            jnp.zeros((nf, 1, SSM_BLK_IN), F32)], axis=2)
    vec = jnp.concatenate([dvec, bvec], axis=1)
    return bdb, bdc, glu, avec, vec


def _s5(proj, params, ti=S5_STEPS):
    bdb, bdc, glu, avec, vec = params
    bsz, seqlen, n_in = proj.shape
    width = n_in - QKV_WIDTH
    nf = width // SSM_BLK_IN
    lsub = seqlen // N_SUBSEQ
    ti = min(ti, lsub)
    u_col0 = QKV_WIDTH // SSM_BLK_IN
    p4 = proj.reshape(bsz, N_SUBSEQ, lsub, n_in)
    grid = (bsz, nf, lsub // ti)
    u_spec = pl.BlockSpec((None, N_SUBSEQ, ti, SSM_BLK_IN), lambda b, f, i: (b, 0, i, u_col0 + f))
    o_spec = pl.BlockSpec((None, N_SUBSEQ, ti, SSM_BLK_IN), lambda b, f, i: (b, 0, i, f))
    blk = lambda r, c: pl.BlockSpec((None, r, c), lambda b, f, i: (f, 0, 0))
    f_spec = pl.BlockSpec((None, None, N_SUBSEQ, 2 * SSM_BLK_ST), lambda b, f, i: (b, f, 0, 0))
    scratch = [pltpu.VMEM((ti * SUBLANES, 2 * SSM_BLK_ST), F32),
               pltpu.VMEM((SUBLANES, 2 * SSM_BLK_ST), F32)]
    sem = ("arbitrary", "arbitrary", "arbitrary")
    fin = pl.pallas_call(
        functools.partial(_s5_pass1_kernel, ti=ti),
        grid=grid,
        in_specs=[u_spec, blk(SSM_BLK_IN, 2 * SSM_BLK_ST), blk(4, SSM_BLK_ST)],
        out_specs=f_spec,
        out_shape=jax.ShapeDtypeStruct((bsz, nf, N_SUBSEQ, 2 * SSM_BLK_ST), F32),
        scratch_shapes=scratch,
        compiler_params=_cparams(sem),
        name="s5_pass1",
    )(p4, bdb, avec)
    out = pl.pallas_call(
        functools.partial(_s5_pass2_kernel, ti=ti),
        grid=grid,
        in_specs=[u_spec, f_spec, blk(SSM_BLK_IN, 2 * SSM_BLK_ST), blk(4, SSM_BLK_ST),
                  blk(2 * SSM_BLK_ST, SSM_BLK_IN), blk(SSM_BLK_IN, 2 * SSM_BLK_IN),
                  blk(2, 2 * SSM_BLK_IN)],
        out_specs=o_spec,
        out_shape=jax.ShapeDtypeStruct((bsz, N_SUBSEQ, lsub, width), BF16),
        scratch_shapes=scratch,
        compiler_params=_cparams(sem),
        name="s5_pass2",
    )(p4, fin, bdb, avec, bdc, glu, vec)
    return out.reshape(bsz, seqlen, width)


def _layer_norm_rows(y, g, b):
    mu = jnp.mean(y, axis=-1, keepdims=True)
    yc = y - mu
    var = jnp.mean(yc * yc, axis=-1, keepdims=True)
    return yc * lax.rsqrt(var + EPS) * g + b


def _pack_bf16_pairs(lo, hi):
    lo_bits = lax.bitcast_convert_type(lo.astype(BF16).astype(F32), jnp.uint32)
    hi_bits = lax.bitcast_convert_type(hi.astype(BF16).astype(F32), jnp.uint32)
    return (lo_bits >> 16) | (hi_bits & jnp.uint32(0xFFFF0000))


def _unpack_bf16_pairs(w):
    lo = lax.bitcast_convert_type(w << 16, F32)
    hi = lax.bitcast_convert_type(w & jnp.uint32(0xFFFF0000), F32)
    return lo, hi


def _outproj_kernel(attn_ref, ssm_ref, gs_ref, w_ref, x_ref, mod_ref, ln_ref, wr_ref, br_ref,
                    x1_ref, hp_ref, logit_ref, ybuf, *, nc):
    m = mod_ref[...]
    ssm = ssm_ref[...].astype(F32)
    ms = jnp.mean(ssm * ssm, axis=-1, keepdims=True)
    ssm_n = (ssm * lax.rsqrt(ms + EPS) * gs_ref[...]).astype(BF16)
    attn = attn_ref[...]
    ka = attn.shape[1]
    d = x_ref.shape[1]
    for n0 in range(0, d, nc):
        mix = (jnp.dot(attn, w_ref[:ka, n0:n0 + nc], preferred_element_type=F32)
               + jnp.dot(ssm_n, w_ref[ka:, n0:n0 + nc], preferred_element_type=F32))
        ybuf[:, n0:n0 + nc] = DN_ALPHA * x_ref[:, n0:n0 + nc] + (1.0 + m[2:3, n0:n0 + nc]) * mix
    x1 = _layer_norm_rows(ybuf[...], ln_ref[0:1, :], ln_ref[1:2, :])
    x1_ref[...] = x1
    h2 = x1 * (1.0 + m[4:5, :]) + m[3:4, :]
    hp_ref[...] = _pack_bf16_pairs(h2[:, :d // 2], h2[:, d // 2:])
    hi = h2.astype(BF16)
    lo = (h2 - hi.astype(F32)).astype(BF16)
    tm = h2.shape[0]
    r = jnp.dot(jnp.concatenate([hi, lo], axis=0), wr_ref[...], preferred_element_type=F32)
    logit_ref[...] = r[:tm, :N_EXPERTS] + r[:tm, N_EXPERTS:] + r[tm:, :N_EXPERTS] + br_ref[...]


def _route_kernel(logit_ref, idx_ref, gate_ref, rank_ref, cnt_ref):
    logits = logit_ref[...]
    tm = logits.shape[0]
    lane = lax.broadcasted_iota(jnp.int32, logits.shape, 1)
    vals, idxs = [], []
    for _ in range(TOP_K):
        mx = jnp.max(logits, axis=-1, keepdims=True)
        ix = jnp.min(jnp.where(logits == mx, lane, N_EXPERTS), axis=-1, keepdims=True)
        vals.append(mx)
        idxs.append(ix)
        logits = jnp.where(lane == ix, -jnp.inf, logits)
    tv = jnp.concatenate(vals, axis=1)
    e = jnp.exp(tv - vals[0])
    gate_ref[...] = e / jnp.sum(e, axis=-1, keepdims=True)
    idx_ref[...] = jnp.concatenate(idxs, axis=1)
    tri = (lax.broadcasted_iota(jnp.int32, (tm, tm), 0) > lax.broadcasted_iota(jnp.int32, (tm, tm), 1)).astype(BF16)
    run = jnp.zeros((1, N_EXPERTS), F32)
    ranks = []
    for ix in idxs:
        onehot = (lane == ix).astype(F32)
        before = jnp.dot(tri, onehot.astype(BF16), preferred_element_type=F32) + run
        ranks.append(jnp.sum(onehot * before, axis=-1, keepdims=True))
        run = run + jnp.sum(onehot, axis=0, keepdims=True)
    rank_ref[...] = jnp.concatenate(ranks, axis=1).astype(jnp.int32)
    cnt_ref[...] = jnp.zeros_like(cnt_ref)
    cnt_ref[0:1, 0:N_EXPERTS] = run


def _out_proj(attn_n, ssm, g_ssm, w_out_bf, x, mod3, ln1, wr, br, tm=128):
    bsz, seqlen, d = x.shape
    ka = attn_n.shape[-1]
    ks = w_out_bf.shape[0] - ka
    row = lambda b, i: (b, i, 0)
    const = lambda b, i: (0, 0)
    return pl.pallas_call(
        functools.partial(_outproj_kernel, nc=512),
        grid=(bsz, seqlen // tm),
        in_specs=[pl.BlockSpec((None, tm, ka), row),
                  pl.BlockSpec((None, tm, ks), row),
                  pl.BlockSpec((1, ks), const),
                  _resident((ka + ks, d), const),
                  pl.BlockSpec((None, tm, d), row),
                  pl.BlockSpec((None, 6, d), lambda b, i: (b, 0, 0)),
                  pl.BlockSpec((2, d), const),
                  pl.BlockSpec((d, 2 * N_EXPERTS), const),
                  pl.BlockSpec((1, N_EXPERTS), const)],
        out_specs=[pl.BlockSpec((None, tm, d), row),
                   pl.BlockSpec((None, tm, d // 2), row),
                   pl.BlockSpec((None, tm, N_EXPERTS), row)],
        out_shape=[jax.ShapeDtypeStruct((bsz, seqlen, d), F32),
                   jax.ShapeDtypeStruct((bsz, seqlen, d // 2), jnp.uint32),
                   jax.ShapeDtypeStruct((bsz, seqlen, N_EXPERTS), F32)],
        scratch_shapes=[pltpu.VMEM((tm, d), F32)],
        compiler_params=_cparams(("arbitrary", "arbitrary")),
        name="out_proj_ln_router",
    )(attn_n, ssm, g_ssm.reshape(1, ks), w_out_bf, x, mod3, ln1, wr, br)


def _route(logits, tm=ROUTE_TILE):
    n_tok = logits.shape[0]
    nt = n_tok // tm
    row = lambda i: (i, 0)
    k_shape = lambda dt: jax.ShapeDtypeStruct((n_tok, TOP_K), dt)
    return pl.pallas_call(
        _route_kernel,
        grid=(nt,),
        in_specs=[pl.BlockSpec((tm, N_EXPERTS), row)],
        out_specs=[pl.BlockSpec((tm, TOP_K), row), pl.BlockSpec((tm, TOP_K), row), pl.BlockSpec((tm, TOP_K), row),
                   pl.BlockSpec((None, SUBLANES, LANES), lambda i: (i, 0, 0))],
        out_shape=[k_shape(jnp.int32), k_shape(F32), k_shape(jnp.int32),
                   jax.ShapeDtypeStruct((nt, SUBLANES, LANES), F32)],
        compiler_params=_cparams(("arbitrary",)),
        name="route_topk",
    )(logits)


def _stream_expert_weights(blk_e, first, nxt, last, w_hbm, stage, wbf, sem):
    ct = pl.program_id(0)
    rb = pl.program_id(1)
    width = wbf[0].shape[1]

    def copies(e, col_tile):
        c0 = pl.multiple_of(col_tile * width, width)
        return [pltpu.make_async_copy(w.at[e, :, pl.ds(c0, width)], s, sem.at[j])
                for j, (w, s) in enumerate(zip(w_hbm, stage))]

    @pl.when((ct == 0) & (rb == 0))
    def _():
        for cp in copies(blk_e[0], 0):
            cp.start(priority=1)

    @pl.when(first[rb] == 1)
    def _():
        for cp in copies(blk_e[rb], ct):
            cp.wait()
        def convert(c, carry):
            r0 = pl.multiple_of(c * CAST_ROWS, CAST_ROWS)
            for s, w in zip(stage, wbf):
                w[pl.ds(r0, CAST_ROWS), :] = s[pl.ds(r0, CAST_ROWS), :].astype(BF16)
            return carry
        lax.fori_loop(0, stage[0].shape[0] // CAST_ROWS, convert, 0)

        @pl.when(last[rb] == 0)
        def _():
            for cp in copies(nxt[rb], ct):
                cp.start(priority=1)

        @pl.when((last[rb] == 1) & (ct + 1 < pl.num_programs(0)))
        def _():
            for cp in copies(nxt[rb], ct + 1):
                cp.start(priority=1)


def _for_live_rows(rb, nreal, short, out_ref, compute):
    full = out_ref.shape[0]
    for rows, is_short in ((full, 0), (full // 2, 1)):
        @pl.when((rb < nreal[0]) & (short[rb] == is_short))
        def _(rows=rows):
            compute(rows)
            if rows < full:
                out_ref[rows:, :] = jnp.zeros((full - rows, out_ref.shape[1]), out_ref.dtype)

    @pl.when(rb >= nreal[0])
    def _():
        out_ref[...] = jnp.zeros_like(out_ref)


def _expert_up_kernel(blk_e, first, nreal, nxt, last, short, xs_ref, wg_hbm, wu_hbm, bg_ref, bu_ref, act_ref,
                      stg_g, stg_u, wgb, wub, sem):
    rb = pl.program_id(1)
    _stream_expert_weights(blk_e, first, nxt, last, (wg_hbm, wu_hbm), (stg_g, stg_u), (wgb, wub), sem)

    def compute(rows):
        x = jnp.concatenate(_unpack_bf16_pairs(xs_ref[:rows, :]), axis=1).astype(BF16)
        g = jnp.dot(x, wgb[...], preferred_element_type=F32) + bg_ref[...]
        up = jnp.dot(x, wub[...], preferred_element_type=F32) + bu_ref[...]
        g = jnp.minimum(g, SWIGLU_LIMIT)
        up = jnp.clip(up, -SWIGLU_LIMIT, SWIGLU_LIMIT)
        act_ref[:rows, :] = (g * jax.nn.sigmoid(SWIGLU_ALPHA * g) * (up + 1.0)).astype(BF16)

    _for_live_rows(rb, nreal, short, act_ref, compute)


def _expert_down_kernel(blk_e, first, nreal, nxt, last, short, act_ref, wd_hbm, bd_ref, y_ref, stg, wdb, sem):
    rb = pl.program_id(1)
    _stream_expert_weights(blk_e, first, nxt, last, (wd_hbm,), (stg,), (wdb,), sem)

    def compute(rows):
        y = jnp.dot(act_ref[:rows, :], wdb[...], preferred_element_type=F32) + bd_ref[...]
        half = y.shape[1] // 2
        y_ref[:rows, :] = _pack_bf16_pairs(y[:, :half], y[:, half:])

    _for_live_rows(rb, nreal, short, y_ref, compute)


def _experts(xs, sched, w_gate, b_gate, w_up, b_up, w_down, b_down, tf=512, tn=DOWN_TILE):
    cap = xs.shape[0]
    n_e, d, dff = w_gate.shape
    nblk = cap // EXPERT_BLK
    hbm = pl.BlockSpec(memory_space=pl.ANY)
    act = pl.pallas_call(
        _expert_up_kernel,
        grid_spec=pltpu.PrefetchScalarGridSpec(
            num_scalar_prefetch=6,
            grid=(dff // tf, nblk),
            in_specs=[pl.BlockSpec((EXPERT_BLK, d // 2), lambda f, r, be, *_: (r, 0)),
                      hbm, hbm,
                      pl.BlockSpec((None, 1, tf), lambda f, r, be, *_: (be[r], 0, f)),
                      pl.BlockSpec((None, 1, tf), lambda f, r, be, *_: (be[r], 0, f))],
            out_specs=pl.BlockSpec((EXPERT_BLK, tf), lambda f, r, be, *_: (r, f)),
            scratch_shapes=[pltpu.VMEM((d, tf), F32), pltpu.VMEM((d, tf), F32),
                            pltpu.VMEM((d, tf), BF16), pltpu.VMEM((d, tf), BF16),
                            pltpu.SemaphoreType.DMA((2,))]),
        out_shape=jax.ShapeDtypeStruct((cap, dff), BF16),
        compiler_params=_cparams(("arbitrary", "arbitrary")),
        name="expert_gate_up",
    )(*sched, xs, w_gate, w_up, b_gate.reshape(n_e, 1, dff), b_up.reshape(n_e, 1, dff))
    ys = pl.pallas_call(
        _expert_down_kernel,
        grid_spec=pltpu.PrefetchScalarGridSpec(
            num_scalar_prefetch=6,
            grid=(d // tn, nblk),
            in_specs=[pl.BlockSpec((EXPERT_BLK, dff), lambda n, r, be, *_: (r, 0)),
                      hbm,
                      pl.BlockSpec((None, 1, tn), lambda n, r, be, *_: (be[r], 0, n))],
            out_specs=pl.BlockSpec((EXPERT_BLK, tn // 2), lambda n, r, be, *_: (r, n)),
            scratch_shapes=[pltpu.VMEM((dff, tn), F32), pltpu.VMEM((dff, tn), BF16),
                            pltpu.SemaphoreType.DMA((1,))]),
        out_shape=jax.ShapeDtypeStruct((cap, d // 2), jnp.uint32),
        compiler_params=_cparams(("arbitrary", "arbitrary")),
        name="expert_down",
    )(*sched, act, w_down, b_down.reshape(n_e, 1, d))
    return ys


def _route_tables(cnt, top_idx, rank):
    n_assign = top_idx.size
    counts = cnt[:, 0, :N_EXPERTS].astype(jnp.int32)
    tot = jnp.sum(counts, axis=0)
    padded = (tot + EXPERT_BLK - 1) // EXPERT_BLK * EXPERT_BLK
    pend = jnp.cumsum(padded)
    pstart = pend - padded
    base = pstart[None, :] + jnp.cumsum(counts, axis=0) - counts
    experts = jnp.arange(N_EXPERTS, dtype=jnp.int32)
    idx_t = top_idx.reshape(-1, ROUTE_TILE * TOP_K)
    dest = jnp.sum(jnp.where(idx_t[..., None] == experts, base[:, None, :], 0), axis=-1)
    dest = (dest.reshape(-1) + rank.reshape(-1)).astype(jnp.int32)
    cap = ((n_assign + EXPERT_BLK - 1) // EXPERT_BLK) * EXPERT_BLK + N_EXPERTS * EXPERT_BLK
    nblk = cap // EXPERT_BLK
    blk = jnp.arange(nblk, dtype=jnp.int32)
    blk_e = jnp.sum((pend[None, :] <= blk[:, None] * EXPERT_BLK).astype(jnp.int32), axis=1)
    blk_e = jnp.minimum(blk_e, N_EXPERTS - 1).astype(jnp.int32)
    nreal = (pend[-1:] // EXPERT_BLK).astype(jnp.int32)
    real = blk < nreal[0]
    first = real & jnp.concatenate([jnp.ones((1,), bool), blk_e[1:] != blk_e[:-1]])
    starts = jnp.where(first, blk, nblk)
    nxt_blk = jnp.concatenate([lax.cummin(starts, reverse=True)[1:], jnp.full((1,), nblk, jnp.int32)])
    last = nxt_blk >= nblk
    nxt = jnp.where(last, blk_e[0], blk_e[jnp.minimum(nxt_blk, nblk - 1)]).astype(jnp.int32)
    live = jnp.sum(jnp.where(blk_e[:, None] == experts[None, :], (pstart + tot)[None, :], 0), axis=1) - blk * EXPERT_BLK
    short = real & (live <= EXPERT_BLK // 2)
    sched = (blk_e, first.astype(jnp.int32), nreal, nxt, last.astype(jnp.int32), short.astype(jnp.int32))
    shift = base % GATHER_CHUNK
    span = jnp.where(counts > 0, (counts + shift + GATHER_CHUNK - 1) // GATHER_CHUNK * GATHER_CHUNK, 0)
    off = jnp.cumsum(span, axis=1) - span
    max_pieces = ROUTE_TILE * TOP_K // GATHER_CHUNK + 2 * N_EXPERTS
    piece = jnp.arange(max_pieces, dtype=jnp.int32)[None, :, None] * GATHER_CHUNK
    in_run = (piece >= off[:, None, :]) & (piece < (off + span)[:, None, :])
    piece_src = jnp.sum(jnp.where(in_run, (base - shift - off)[:, None, :] + piece, 0), axis=-1)
    n_piece = jnp.sum(span, axis=1) // GATHER_CHUNK
    runs = (n_piece.astype(jnp.int32), piece_src.reshape(-1).astype(jnp.int32),
            (off + shift).astype(jnp.int32)[:, None, :])
    return dest, cap, sched, (pstart + tot).astype(jnp.int32), (padded - tot).astype(jnp.int32), runs


def _row_copy(src, src_row, dst, dst_row, sem):
    return pltpu.make_async_copy(src.at[pl.ds(src_row, 1)], dst.at[pl.ds(dst_row, 1)], sem)


def _dispatch_kernel(pad0_ref, padn_ref, nreal_ref, hp_ref, dest_ref, xs_ref, zblk, sem, zsem, bsem):
    i = pl.program_id(0)
    tm = hp_ref.shape[0]
    nblk = xs_ref.shape[0] // EXPERT_BLK

    def for_each_pad_row(fn):
        def per_expert(e, c):
            def per_row(r, c2):
                fn(_row_copy(zblk, 0, xs_ref, pad0_ref[e] + r, zsem))
                return c2
            return lax.fori_loop(0, padn_ref[e], per_row, c)
        lax.fori_loop(0, N_EXPERTS, per_expert, 0)

    def for_each_unused_block(fn):
        def per_block(b, c):
            r0 = pl.multiple_of(b * EXPERT_BLK, EXPERT_BLK)
            fn(pltpu.make_async_copy(zblk, xs_ref.at[pl.ds(r0, EXPERT_BLK)], bsem))
            return c
        lax.fori_loop(nreal_ref[0], nblk, per_block, 0)

    @pl.when(i == 0)
    def _():
        zblk[...] = jnp.zeros_like(zblk)
        for_each_pad_row(lambda cp: cp.start())
        for_each_unused_block(lambda cp: cp.start())

    def issue(r, c):
        for k in range(TOP_K):
            _row_copy(hp_ref, r, xs_ref, dest_ref[r * TOP_K + k], sem).start(priority=k % 2)
        return c
    lax.fori_loop(0, tm, issue, 0, unroll=2)

    def drain(r, c):
        for k in range(TOP_K):
            _row_copy(hp_ref, 0, xs_ref, 0, sem).wait()
        return c
    lax.fori_loop(0, tm, drain, 0)

    @pl.when(i == 0)
    def _():
        for_each_pad_row(lambda cp: cp.wait())
        for_each_unused_block(lambda cp: cp.wait())


def _dispatch(hp, dest, pad0, padn, nreal, cap, tm=MOVE_TILE):
    n_tok, half = hp.shape
    return pl.pallas_call(
        _dispatch_kernel,
        grid_spec=pltpu.PrefetchScalarGridSpec(
            num_scalar_prefetch=3,
            grid=(n_tok // tm,),
            in_specs=[pl.BlockSpec((tm, half), lambda i, *_: (i, 0)),
                      pl.BlockSpec((tm * TOP_K,), lambda i, *_: (i,), memory_space=pltpu.SMEM)],
            out_specs=pl.BlockSpec(memory_space=pl.ANY),
            scratch_shapes=[pltpu.VMEM((EXPERT_BLK, half), jnp.uint32),
                            pltpu.SemaphoreType.DMA, pltpu.SemaphoreType.DMA, pltpu.SemaphoreType.DMA]),
        out_shape=jax.ShapeDtypeStruct((cap, half), jnp.uint32),
        compiler_params=_cparams(("arbitrary",)),
        name="moe_dispatch",
    )(pad0, padn, nreal, hp, dest)


def _combine_kernel(npiece_ref, src_ref, idx_ref, rank_ref, gate_ref, runrow_ref, x1_ref, mod_ref, ln_ref,
                    ys_ref, o_ref, buf, sem, *, tn):
    i = pl.program_id(0)
    tm = x1_ref.shape[0]
    rows = buf.shape[1]

    max_pieces = rows // GATHER_CHUNK

    def for_each_piece(tile, fn):
        slot = tile % 2

        def per_piece(p, c):
            src = pl.multiple_of(src_ref[tile * max_pieces + p], GATHER_CHUNK)
            dst = pl.multiple_of(p * GATHER_CHUNK, GATHER_CHUNK)
            fn(pltpu.make_async_copy(ys_ref.at[pl.ds(src, GATHER_CHUNK)],
                                     buf.at[slot, pl.ds(dst, GATHER_CHUNK)], sem.at[slot]))
            return c
        lax.fori_loop(0, npiece_ref[tile], per_piece, 0)

    @pl.when(i == 0)
    def _():
        buf[...] = jnp.zeros_like(buf)
        for_each_piece(0, lambda cp: cp.start())

    @pl.when(i + 1 < pl.num_programs(0))
    def _():
        for_each_piece(i + 1, lambda cp: cp.start())

    for_each_piece(i, lambda cp: cp.wait())

    lane = lax.broadcasted_iota(jnp.int32, (tm, N_EXPERTS), 1)
    col = lax.broadcasted_iota(jnp.int32, (tm, rows), 1)
    sel = jnp.zeros((tm, rows), F32)
    for k in range(TOP_K):
        run_row = jnp.sum(jnp.where(lane == idx_ref[:, k:k + 1], runrow_ref[...], 0), axis=-1, keepdims=True)
        sel = sel + jnp.where(col == run_row + rank_ref[:, k:k + 1], gate_ref[:, k:k + 1], 0.0)
    sel = sel.astype(BF16)

    cur = buf.at[i % 2]
    m = mod_ref[...]
    d = x1_ref.shape[1]
    hw = tn // 2
    cw = 512
    for w0 in range(0, d // 2, cw):
        wl, wh = _unpack_bf16_pairs(cur[:, w0:w0 + cw])
        lo = jnp.dot(sel, wl.astype(BF16), preferred_element_type=F32)
        hi = jnp.dot(sel, wh.astype(BF16), preferred_element_type=F32)
        for half, moe in ((0, lo), (1, hi)):
            c0 = (w0 // hw) * tn + half * hw + w0 % hw
            o_ref[:, c0:c0 + cw] = DN_ALPHA * x1_ref[:, c0:c0 + cw] + (1.0 + m[5:6, c0:c0 + cw]) * moe
    o_ref[...] = _layer_norm_rows(o_ref[...], ln_ref[0:1, :], ln_ref[1:2, :])


def _combine(ys, runs, top_idx, rank, gates, x1, mod3, ln2, tm=ROUTE_TILE, tn=DOWN_TILE):
    bsz, seqlen, d = x1.shape
    n_tok = bsz * seqlen
    per_b = seqlen // tm
    n_piece, piece_src, run_row = runs
    buf_rows = tm * TOP_K + N_EXPERTS * 2 * GATHER_CHUNK
    assert piece_src.shape[0] * GATHER_CHUNK == (n_tok // tm) * buf_rows
    tok = lambda i, *_: (i, 0)
    out = pl.pallas_call(
        functools.partial(_combine_kernel, tn=tn),
        grid_spec=pltpu.PrefetchScalarGridSpec(
            num_scalar_prefetch=2,
            grid=(n_tok // tm,),
            in_specs=[pl.BlockSpec((tm, TOP_K), tok), pl.BlockSpec((tm, TOP_K), tok), pl.BlockSpec((tm, TOP_K), tok),
                      pl.BlockSpec((None, 1, N_EXPERTS), lambda i, *_: (i, 0, 0)),
                      pl.BlockSpec((tm, d), tok),
                      pl.BlockSpec((None, 6, d), lambda i, *_: (i // per_b, 0, 0)),
                      pl.BlockSpec((2, d), lambda i, *_: (0, 0)),
                      pl.BlockSpec(memory_space=pl.ANY)],
            out_specs=pl.BlockSpec((tm, d), tok),
            scratch_shapes=[pltpu.VMEM((2, buf_rows, d // 2), jnp.uint32), pltpu.SemaphoreType.DMA((2,))]),
        out_shape=jax.ShapeDtypeStruct((n_tok, d), F32),
        compiler_params=_cparams(("arbitrary",)),
        name="moe_combine_ln",
    )(n_piece, piece_src, top_idx, rank, gates, run_row, x1.reshape(n_tok, d), mod3, ln2, ys)
    return out.reshape(bsz, seqlen, d)


def kernel(x, c, positions, w_ada, b_ada, w_in, attn_sinks, ssm_a_re, ssm_a_im, ssm_b_re, ssm_b_im,
           ssm_c_re, ssm_c_im, ssm_d, ssm_log_dt, ssm_w_glu, ssm_b_glu, g_attn_out, g_ssm_out, w_out,
           ln1_g, ln1_b, w_router, b_router, w_gate, b_gate, w_up, b_up, w_down, b_down, ln2_g, ln2_b):
    bsz, seqlen, d = x.shape
    lsub = seqlen // N_SUBSEQ
    n_tok = bsz * seqlen
    rope_tab = _rope_tables(positions)
    for l in range(w_ada.shape[0]):
        mod3 = _ada_mod(c, w_ada[l], b_ada[l]).reshape(bsz, 6, d)
        proj = _in_proj(x, mod3, rope_tab, w_in[l].astype(BF16))
        attn_n = _attention(proj, attn_sinks[l].astype(F32), g_attn_out[l].astype(F32))
        s5p = _s5_params(ssm_a_re[l], ssm_a_im[l], ssm_b_re[l], ssm_b_im[l], ssm_c_re[l], ssm_c_im[l],
                         ssm_d[l], ssm_log_dt[l], ssm_w_glu[l], ssm_b_glu[l], lsub)
        ssm = _s5(proj, s5p)
        wr_hi = w_router[l].astype(BF16)
        wr_lo = (w_router[l] - wr_hi.astype(F32)).astype(BF16)
        x1, hp, logits = _out_proj(
            attn_n, ssm, g_ssm_out[l].astype(F32), w_out[l].astype(BF16), x, mod3,
            jnp.stack([ln1_g[l], ln1_b[l]]).astype(F32),
            jnp.concatenate([wr_hi, wr_lo], axis=1), b_router[l].reshape(1, N_EXPERTS).astype(F32))
        top_idx, gates, rank, cnt = _route(logits.reshape(n_tok, N_EXPERTS))
        dest, cap, sched, pad0, padn, runs = _route_tables(cnt, top_idx, rank)
        xs = _dispatch(hp.reshape(n_tok, d // 2), dest, pad0, padn, sched[2], cap)
        ys = _experts(xs, sched, w_gate[l], b_gate[l], w_up[l], b_up[l], w_down[l], b_down[l])
        x = _combine(ys, runs, top_idx, rank, gates, x1, mod3, jnp.stack([ln2_g[l], ln2_b[l]]).astype(F32))
    return x
```

```python
import functools
import math

import jax
import jax.numpy as jnp
from jax import lax
from jax.experimental import pallas as pl
from jax.experimental.pallas import tpu as pltpu

F32 = jnp.float32
BF16 = jnp.bfloat16

HEAD_DIM = 64
N_Q_HEADS = 32
N_KV_HEADS = 4
GQ = N_Q_HEADS // N_KV_HEADS
ATTN_WIDTH = N_Q_HEADS * HEAD_DIM
KV_WIDTH = N_KV_HEADS * HEAD_DIM
QKV_WIDTH = ATTN_WIDTH + 2 * KV_WIDTH
BLK = 128
ROT_DIM = HEAD_DIM // 4
ROPE_THETA = 500000.0
GROUP_CH = 16
STATE = 64
N_EXPERTS = 32
TOP_K = 4
SWIGLU_LIMIT = 7.0
SWIGLU_ALPHA = 1.702
EXPERT_BLK = 256
ROUTE_TILE = 256
GATHER_CHUNK = 8
MOVE_TILE = 1024
DOWN_TILE = 4096
S5_STEPS = 512
CAST_ROWS = 128
DEPTH = 1
DN_ALPHA = (2.0 * DEPTH) ** 0.25
EPS = 1e-5

LANES = 128
SUBLANES = 8
N_SUBSEQ = SUBLANES
GROUPS_PER_BLK = 16
SSM_BLK_IN = GROUPS_PER_BLK * GROUP_CH
SSM_BLK_ST = GROUPS_PER_BLK * STATE
VMEM_LIMIT = 56 * 1024 * 1024


def _cparams(sem, vmem=VMEM_LIMIT):
    return pltpu.CompilerParams(dimension_semantics=sem, vmem_limit_bytes=vmem)


def _resident(shape, index_map):
    return pl.BlockSpec(shape, index_map, pipeline_mode=pl.Buffered(1))


def _ada_kernel(c_ref, w_ref, b_ref, o_ref):
    c = c_ref[...]
    ca = c * jax.nn.sigmoid(c)
    o_ref[...] = jnp.dot(ca.astype(BF16), w_ref[...].astype(BF16),
                         preferred_element_type=F32) + b_ref[...]


def _ada_mod(c, w_ada, b_ada, tn=1024):
    bsz, d = c.shape
    n = w_ada.shape[1]
    c8 = jnp.zeros((SUBLANES, d), F32).at[:bsz].set(c)
    out = pl.pallas_call(
        _ada_kernel,
        grid=(n // tn,),
        in_specs=[pl.BlockSpec((SUBLANES, d), lambda j: (0, 0)),
                  pl.BlockSpec((d, tn), lambda j: (0, j)),
                  pl.BlockSpec((1, tn), lambda j: (0, j))],
        out_specs=pl.BlockSpec((SUBLANES, tn), lambda j: (0, j)),
        out_shape=jax.ShapeDtypeStruct((SUBLANES, n), F32),
        compiler_params=_cparams(("arbitrary",)),
        name="ada_mod",
    )(c8, w_ada, b_ada.reshape(1, n))
    return out[:bsz]


def _rope(t, tab):
    c, s_lo, s_hi = tab[:, :LANES], tab[:, LANES:2 * LANES], tab[:, 2 * LANES:]
    half = ROT_DIM // 2
    out = []
    for j in range(t.shape[1] // LANES):
        tj = t[:, j * LANES:(j + 1) * LANES]
        out.append(tj * c + pltpu.roll(tj, LANES - half, 1) * s_lo + pltpu.roll(tj, half, 1) * s_hi)
    return jnp.concatenate(out, axis=1)


def _inproj_kernel(x_ref, mod_ref, tab_ref, w_ref, o_ref, *, nc):
    m = mod_ref[...]
    h = (x_ref[...] * (1.0 + m[1:2, :]) + m[0:1, :]).astype(BF16)
    for n0 in range(0, o_ref.shape[-1], nc):
        p = jnp.dot(h, w_ref[:, n0:n0 + nc], preferred_element_type=F32)
        if n0 < ATTN_WIDTH:
            p = _rope(p * (HEAD_DIM ** -0.5), tab_ref[...])
        elif n0 == ATTN_WIDTH:
            p = jnp.concatenate([_rope(p[:, :KV_WIDTH], tab_ref[...]), p[:, KV_WIDTH:]], axis=1)
        o_ref[:, n0:n0 + nc] = p.astype(BF16)


def _in_proj(x, mod3, rope_tab, w_in_bf, tm=256, nc=512):
    bsz, seqlen, d = x.shape
    n_in = w_in_bf.shape[1]
    assert ATTN_WIDTH % nc == 0 and nc >= 2 * KV_WIDTH
    return pl.pallas_call(
        functools.partial(_inproj_kernel, nc=nc),
        grid=(bsz, seqlen // tm),
        in_specs=[pl.BlockSpec((None, tm, d), lambda b, i: (b, i, 0)),
                  pl.BlockSpec((None, 6, d), lambda b, i: (b, 0, 0)),
                  pl.BlockSpec((None, tm, 3 * LANES), lambda b, i: (b, i, 0)),
                  _resident((d, n_in), lambda b, i: (0, 0))],
        out_specs=pl.BlockSpec((None, tm, n_in), lambda b, i: (b, i, 0)),
        out_shape=jax.ShapeDtypeStruct((bsz, seqlen, n_in), BF16),
        compiler_params=_cparams(("arbitrary", "arbitrary")),
        name="in_proj",
    )(x, mod3, rope_tab, w_in_bf)


def _attn_kernel(sink_ref, q_ref, kc_ref, kp_ref, vc_ref, vp_ref, g_ref, o_ref):
    n = pl.program_id(1)
    low = lax.broadcasted_iota(jnp.int32, (2 * BLK, LANES), 1) < HEAD_DIM
    low_q = lax.broadcasted_iota(jnp.int32, (BLK, LANES), 1) < HEAD_DIM

    k_raw = jnp.concatenate([kp_ref[...], kc_ref[...]], axis=0).astype(F32)
    v_raw = jnp.concatenate([vp_ref[...], vc_ref[...]], axis=0).astype(F32)

    qi = lax.broadcasted_iota(jnp.int32, (BLK, BLK), 0)
    kj = lax.broadcasted_iota(jnp.int32, (BLK, BLK), 1)
    own = kj <= qi
    prev_ok = kj >= jnp.where(n > 0, 0, BLK)

    o_chunks = []
    for hk in range(N_KV_HEADS):
        kc = k_raw[:, (hk // 2) * LANES:(hk // 2 + 1) * LANES]
        vc = v_raw[:, (hk // 2) * LANES:(hk // 2 + 1) * LANES]
        k_sw = pltpu.roll(kc, HEAD_DIM, 1)
        v_sw = pltpu.roll(vc, HEAD_DIM, 1)
        if hk % 2 == 0:
            kk2 = jnp.where(low, kc, k_sw)
            v_lo = jnp.where(low, vc, 0.0)
            v_hi = jnp.where(low, 0.0, v_sw)
        else:
            kk2 = jnp.where(low, k_sw, kc)
            v_lo = jnp.where(low, v_sw, 0.0)
            v_hi = jnp.where(low, 0.0, vc)
        kk2 = kk2.astype(BF16)
        v_lo = v_lo.astype(BF16)
        v_hi = v_hi.astype(BF16)
        lhs = []
        for j in range(GQ // 2):
            c0 = (hk * (GQ // 2) + j) * LANES
            q2 = q_ref[:, c0:c0 + LANES].astype(F32)
            lhs.append(jnp.where(low_q, q2, 0.0).astype(BF16))
            lhs.append(jnp.where(low_q, 0.0, q2).astype(BF16))
        s_all = lax.dot_general(jnp.concatenate(lhs, axis=0), kk2,
                                (((1,), (1,)), ((), ())), preferred_element_type=F32)
        for j in range(GQ // 2):
            acc = None
            for side, vv in ((0, v_lo), (1, v_hi)):
                i = 2 * j + side
                s_prev = jnp.where(prev_ok, s_all[i * BLK:(i + 1) * BLK, :BLK], -1e30)
                s = jnp.where(own, s_all[i * BLK:(i + 1) * BLK, BLK:], s_prev)
                sink = sink_ref[hk * GQ + i]
                m = jnp.maximum(jnp.max(s, axis=-1, keepdims=True), sink)
                p = jnp.exp(s - m)
                denom = jnp.sum(p, axis=-1, keepdims=True) + jnp.exp(sink - m)
                p = p * (1.0 / denom)
                p = jnp.concatenate([jnp.where(own, 0.0, p), jnp.where(own, p, 0.0)], axis=1).astype(BF16)
                o = jnp.dot(p, vv, preferred_element_type=F32)
                acc = o if acc is None else acc + o
            o_chunks.append(acc)

    ssq = None
    for oc in o_chunks:
        t = jnp.sum(oc * oc, axis=-1, keepdims=True)
        ssq = t if ssq is None else ssq + t
    inv = lax.rsqrt(ssq * (1.0 / ATTN_WIDTH) + EPS)
    for j, oc in enumerate(o_chunks):
        o_ref[:, j * LANES:(j + 1) * LANES] = (oc * inv * g_ref[:, j * LANES:(j + 1) * LANES]).astype(BF16)


def _attention(qkv, sinks, g_attn):
    bsz, seqlen, _ = qkv.shape
    nb = seqlen // BLK
    kcol = ATTN_WIDTH // KV_WIDTH
    cur = lambda b, n: (b, n, 0)
    return pl.pallas_call(
        _attn_kernel,
        grid=(bsz, nb),
        in_specs=[pl.BlockSpec(memory_space=pltpu.SMEM),
                  pl.BlockSpec((None, BLK, ATTN_WIDTH), cur),
                  pl.BlockSpec((None, BLK, KV_WIDTH), lambda b, n: (b, n, kcol)),
                  pl.BlockSpec((None, BLK, KV_WIDTH), lambda b, n: (b, jnp.maximum(n - 1, 0), kcol)),
                  pl.BlockSpec((None, BLK, KV_WIDTH), lambda b, n: (b, n, kcol + 1)),
                  pl.BlockSpec((None, BLK, KV_WIDTH), lambda b, n: (b, jnp.maximum(n - 1, 0), kcol + 1)),
                  pl.BlockSpec((1, ATTN_WIDTH), lambda b, n: (0, 0))],
        out_specs=pl.BlockSpec((None, BLK, ATTN_WIDTH), cur),
        out_shape=jax.ShapeDtypeStruct((bsz, seqlen, ATTN_WIDTH), BF16),
        compiler_params=_cparams(("arbitrary", "arbitrary")),
        name="swa_attention",
    )(sinks, qkv, qkv, qkv, qkv, qkv, g_attn.reshape(1, ATTN_WIDTH))


def _rope_tables(positions):
    half = ROT_DIM // 2
    inv_freq = ROPE_THETA ** (-jnp.arange(0, ROT_DIM, 2, dtype=F32) / ROT_DIM)
    ang = positions.astype(F32)[..., None] * inv_freq
    cos, sin = jnp.cos(ang), jnp.sin(ang)
    shp = cos.shape[:-1] + (HEAD_DIM - ROT_DIM,)
    c = jnp.concatenate([cos, cos, jnp.ones(shp, F32)], axis=-1)
    z8 = jnp.zeros_like(sin)
    s_lo = jnp.concatenate([-sin, z8, jnp.zeros(shp, F32)], axis=-1)
    s_hi = jnp.concatenate([z8, sin, jnp.zeros(shp, F32)], axis=-1)
    rep = LANES // HEAD_DIM
    return jnp.concatenate([jnp.tile(c, rep), jnp.tile(s_lo, rep), jnp.tile(s_hi, rep)], axis=-1)


def _s5_scan(buf, ar, ai, hr, hi, ti, store):
    def step(i, carry):
        hr, hi = carry
        r0 = pl.multiple_of(i * SUBLANES, SUBLANES)
        row = buf[pl.ds(r0, SUBLANES), :]
        nhr = ar * hr - ai * hi + row[:, :SSM_BLK_ST]
        nhi = ar * hi + ai * hr + row[:, SSM_BLK_ST:]
        if store:
            buf[pl.ds(r0, SUBLANES), :] = jnp.concatenate([nhr, nhi], axis=1)
        return nhr, nhi
    return lax.fori_loop(0, ti, step, (hr, hi), unroll=4)


def _time_major(u_ref, ti):
    u = pltpu.einshape("jid->ijd", u_ref[...].astype(F32))
    return u.reshape(ti * N_SUBSEQ, u.shape[-1])


def _s5_pass1_kernel(u_ref, bdb_ref, a_ref, f_ref, buf, hst, *, ti):
    ic = pl.program_id(2)

    @pl.when(ic == 0)
    def _():
        hst[...] = jnp.zeros_like(hst)

    u = _time_major(u_ref, ti)
    buf[...] = jnp.dot(u.astype(BF16), bdb_ref[...], preferred_element_type=F32)
    ar = jnp.broadcast_to(a_ref[0:1, :], (SUBLANES, SSM_BLK_ST))
    ai = jnp.broadcast_to(a_ref[1:2, :], (SUBLANES, SSM_BLK_ST))
    hr, hi = _s5_scan(buf, ar, ai, hst[:, :SSM_BLK_ST], hst[:, SSM_BLK_ST:], ti, store=False)
    hst[...] = jnp.concatenate([hr, hi], axis=1)

    @pl.when(ic == pl.num_programs(2) - 1)
    def _():
        f_ref[...] = hst[...]


def _s5_pass2_kernel(u_ref, f_ref, bdb_ref, a_ref, bdc_ref, glu_ref, vec_ref, o_ref, buf, hst, *, ti):
    ic = pl.program_id(2)

    @pl.when(ic == 0)
    def _():
        fr, fi = f_ref[:, :SSM_BLK_ST], f_ref[:, SSM_BLK_ST:]
        pr = jnp.broadcast_to(a_ref[2:3, :], (SUBLANES, SSM_BLK_ST))
        pi = jnp.broadcast_to(a_ref[3:4, :], (SUBLANES, SSM_BLK_ST))
        row = lax.broadcasted_iota(jnp.int32, (SUBLANES, SSM_BLK_ST), 0)
        hr = jnp.zeros((SUBLANES, SSM_BLK_ST), F32)
        hi = jnp.zeros((SUBLANES, SSM_BLK_ST), F32)
        for _ in range(N_SUBSEQ - 1):
            nr = pr * hr - pi * hi + fr
            ni = pr * hi + pi * hr + fi
            hr = jnp.where(row == 0, 0.0, pltpu.roll(nr, 1, 0))
            hi = jnp.where(row == 0, 0.0, pltpu.roll(ni, 1, 0))
        hst[...] = jnp.concatenate([hr, hi], axis=1)

    u = _time_major(u_ref, ti)
    buf[...] = jnp.dot(u.astype(BF16), bdb_ref[...], preferred_element_type=F32)
    ar = jnp.broadcast_to(a_ref[0:1, :], (SUBLANES, SSM_BLK_ST))
    ai = jnp.broadcast_to(a_ref[1:2, :], (SUBLANES, SSM_BLK_ST))
    hr, hi = _s5_scan(buf, ar, ai, hst[:, :SSM_BLK_ST], hst[:, SSM_BLK_ST:], ti, store=True)
    hst[...] = jnp.concatenate([hr, hi], axis=1)

    y = jnp.dot(buf[...].astype(BF16), bdc_ref[...], preferred_element_type=F32)
    y = jax.nn.gelu(y + vec_ref[0:1, :SSM_BLK_IN] * u)
    z = jnp.dot(y.astype(BF16), glu_ref[...], preferred_element_type=F32) + vec_ref[1:2, :]
    out = z[:, :SSM_BLK_IN] * jax.nn.sigmoid(z[:, SSM_BLK_IN:])
    out = pltpu.einshape("ijd->jid", out.reshape(ti, N_SUBSEQ, SSM_BLK_IN))
    o_ref[...] = out.astype(BF16)


def _s5_params(a_re, a_im, b_re, b_im, c_re, c_im, d_skip, log_dt, w_glu, b_glu, lsub):
    g = a_re.shape[0]
    nf = g // GROUPS_PER_BLK
    a = lax.complex(a_re.astype(F32), a_im.astype(F32))
    dt = jnp.exp(log_dt.astype(F32))[:, None]
    a_bar = jnp.exp(a * dt)
    a_pow = jnp.exp(a * dt * lsub)
    b_bar = ((a_bar - 1.0) / a)[..., None] * lax.complex(b_re.astype(F32), b_im.astype(F32))
    def block_diag(m, inner):
        rows = m.shape[1]
        tile = jnp.tile(jnp.eye(inner, dtype=F32), (1, GROUPS_PER_BLK))
        wide = jnp.einsum('frk,kn->frn', m, tile, precision=lax.Precision.HIGHEST)
        rg = jnp.arange(rows, dtype=jnp.int32)[:, None] // (rows // GROUPS_PER_BLK)
        cg = jnp.arange(GROUPS_PER_BLK * inner, dtype=jnp.int32)[None, :] // inner
        return jnp.where(rg == cg, wide, 0.0)

    def bd_in(m):
        return block_diag(jnp.swapaxes(m, 1, 2).reshape(nf, SSM_BLK_IN, STATE), STATE)

    def bd_out(m):
        return block_diag(jnp.swapaxes(m, 1, 2).reshape(nf, SSM_BLK_ST, GROUP_CH), GROUP_CH)

    def bd_glu(m):
        return block_diag(m.reshape(nf, SSM_BLK_IN, GROUP_CH), GROUP_CH)

    bdb = jnp.concatenate([bd_in(jnp.real(b_bar)), bd_in(jnp.imag(b_bar))], axis=2).astype(BF16)
    bdc = jnp.concatenate([bd_out(c_re.astype(F32)), bd_out(-c_im.astype(F32))], axis=1).astype(BF16)
    wg = w_glu.astype(F32)
    glu = jnp.concatenate([bd_glu(wg[..., :GROUP_CH]), bd_glu(wg[..., GROUP_CH:])], axis=2).astype(BF16)
    flat = lambda m: m.reshape(nf, 1, SSM_BLK_ST)
    avec = jnp.concatenate([flat(jnp.real(a_bar)), flat(jnp.imag(a_bar)),
                            flat(jnp.real(a_pow)), flat(jnp.imag(a_pow))], axis=1)
    bg = b_glu.astype(F32).reshape(nf, GROUPS_PER_BLK, 2 * GROUP_CH)
    bvec = jnp.concatenate([bg[..., :GROUP_CH].reshape(nf, 1, SSM_BLK_IN),
                            bg[..., GROUP_CH:].reshape(nf, 1, SSM_BLK_IN)], axis=2)
    dvec = jnp.concatenate([d_skip.astype(F32).reshape(nf, 1, SSM_BLK_IN),
                            jnp.zeros((nf, 1, SSM_BLK_IN), F32)], axis=2)
    vec = jnp.concatenate([dvec, bvec], axis=1)
    return bdb, bdc, glu, avec, vec


def _s5(proj, params, ti=S5_STEPS):
    bdb, bdc, glu, avec, vec = params
    bsz, seqlen, n_in = proj.shape
    width = n_in - QKV_WIDTH
    nf = width // SSM_BLK_IN
    lsub = seqlen // N_SUBSEQ
    ti = min(ti, lsub)
    u_col0 = QKV_WIDTH // SSM_BLK_IN
    p4 = proj.reshape(bsz, N_SUBSEQ, lsub, n_in)
    grid = (bsz, nf, lsub // ti)
    u_spec = pl.BlockSpec((None, N_SUBSEQ, ti, SSM_BLK_IN), lambda b, f, i: (b, 0, i, u_col0 + f))
    o_spec = pl.BlockSpec((None, N_SUBSEQ, ti, SSM_BLK_IN), lambda b, f, i: (b, 0, i, f))
    blk = lambda r, c: pl.BlockSpec((None, r, c), lambda b, f, i: (f, 0, 0))
    f_spec = pl.BlockSpec((None, None, N_SUBSEQ, 2 * SSM_BLK_ST), lambda b, f, i: (b, f, 0, 0))
    scratch = [pltpu.VMEM((ti * SUBLANES, 2 * SSM_BLK_ST), F32),
               pltpu.VMEM((SUBLANES, 2 * SSM_BLK_ST), F32)]
    sem = ("arbitrary", "arbitrary", "arbitrary")
    fin = pl.pallas_call(
        functools.partial(_s5_pass1_kernel, ti=ti),
        grid=grid,
        in_specs=[u_spec, blk(SSM_BLK_IN, 2 * SSM_BLK_ST), blk(4, SSM_BLK_ST)],
        out_specs=f_spec,
        out_shape=jax.ShapeDtypeStruct((bsz, nf, N_SUBSEQ, 2 * SSM_BLK_ST), F32),
        scratch_shapes=scratch,
        compiler_params=_cparams(sem),
        name="s5_pass1",
    )(p4, bdb, avec)
    out = pl.pallas_call(
        functools.partial(_s5_pass2_kernel, ti=ti),
        grid=grid,
        in_specs=[u_spec, f_spec, blk(SSM_BLK_IN, 2 * SSM_BLK_ST), blk(4, SSM_BLK_ST),
                  blk(2 * SSM_BLK_ST, SSM_BLK_IN), blk(SSM_BLK_IN, 2 * SSM_BLK_IN),
                  blk(2, 2 * SSM_BLK_IN)],
        out_specs=o_spec,
        out_shape=jax.ShapeDtypeStruct((bsz, N_SUBSEQ, lsub, width), BF16),
        scratch_shapes=scratch,
        compiler_params=_cparams(sem),
        name="s5_pass2",
    )(p4, fin, bdb, avec, bdc, glu, vec)
    return out.reshape(bsz, seqlen, width)


def _layer_norm_rows(y, g, b):
    mu = jnp.mean(y, axis=-1, keepdims=True)
    yc = y - mu
    var = jnp.mean(yc * yc, axis=-1, keepdims=True)
    return yc * lax.rsqrt(var + EPS) * g + b


def _pack_bf16_pairs(lo, hi):
    lo_bits = lax.bitcast_convert_type(lo.astype(BF16).astype(F32), jnp.uint32)
    hi_bits = lax.bitcast_convert_type(hi.astype(BF16).astype(F32), jnp.uint32)
    return (lo_bits >> 16) | (hi_bits & jnp.uint32(0xFFFF0000))


def _unpack_bf16_pairs(w):
    lo = lax.bitcast_convert_type(w << 16, F32)
    hi = lax.bitcast_convert_type(w & jnp.uint32(0xFFFF0000), F32)
    return lo, hi


def _outproj_kernel(attn_ref, ssm_ref, gs_ref, w_ref, x_ref, mod_ref, ln_ref, wr_ref, br_ref,
                    x1_ref, hp_ref, logit_ref, ybuf, *, nc):
    m = mod_ref[...]
    ssm = ssm_ref[...].astype(F32)
    ms = jnp.mean(ssm * ssm, axis=-1, keepdims=True)
    ssm_n = (ssm * lax.rsqrt(ms + EPS) * gs_ref[...]).astype(BF16)
    attn = attn_ref[...]
    ka = attn.shape[1]
    d = x_ref.shape[1]
    for n0 in range(0, d, nc):
        mix = (jnp.dot(attn, w_ref[:ka, n0:n0 + nc], preferred_element_type=F32)
               + jnp.dot(ssm_n, w_ref[ka:, n0:n0 + nc], preferred_element_type=F32))
        ybuf[:, n0:n0 + nc] = DN_ALPHA * x_ref[:, n0:n0 + nc] + (1.0 + m[2:3, n0:n0 + nc]) * mix
    x1 = _layer_norm_rows(ybuf[...], ln_ref[0:1, :], ln_ref[1:2, :])
    x1_ref[...] = x1
    h2 = x1 * (1.0 + m[4:5, :]) + m[3:4, :]
    hp_ref[...] = _pack_bf16_pairs(h2[:, :d // 2], h2[:, d // 2:])
    hi = h2.astype(BF16)
    lo = (h2 - hi.astype(F32)).astype(BF16)
    tm = h2.shape[0]
    r = jnp.dot(jnp.concatenate([hi, lo], axis=0), wr_ref[...], preferred_element_type=F32)
    logit_ref[...] = r[:tm, :N_EXPERTS] + r[:tm, N_EXPERTS:] + r[tm:, :N_EXPERTS] + br_ref[...]


def _route_kernel(logit_ref, idx_ref, gate_ref, rank_ref, cnt_ref):
    logits = logit_ref[...]
    tm = logits.shape[0]
    lane = lax.broadcasted_iota(jnp.int32, logits.shape, 1)
    vals, idxs = [], []
    for _ in range(TOP_K):
        mx = jnp.max(logits, axis=-1, keepdims=True)
        ix = jnp.min(jnp.where(logits == mx, lane, N_EXPERTS), axis=-1, keepdims=True)
        vals.append(mx)
        idxs.append(ix)
        logits = jnp.where(lane == ix, -jnp.inf, logits)
    tv = jnp.concatenate(vals, axis=1)
    e = jnp.exp(tv - vals[0])
    gate_ref[...] = e / jnp.sum(e, axis=-1, keepdims=True)
    idx_ref[...] = jnp.concatenate(idxs, axis=1)
    tri = (lax.broadcasted_iota(jnp.int32, (tm, tm), 0) > lax.broadcasted_iota(jnp.int32, (tm, tm), 1)).astype(BF16)
    run = jnp.zeros((1, N_EXPERTS), F32)
    ranks = []
    for ix in idxs:
        onehot = (lane == ix).astype(F32)
        before = jnp.dot(tri, onehot.astype(BF16), preferred_element_type=F32) + run
        ranks.append(jnp.sum(onehot * before, axis=-1, keepdims=True))
        run = run + jnp.sum(onehot, axis=0, keepdims=True)
    rank_ref[...] = jnp.concatenate(ranks, axis=1).astype(jnp.int32)
    cnt_ref[...] = jnp.zeros_like(cnt_ref)
    cnt_ref[0:1, 0:N_EXPERTS] = run


def _out_proj(attn_n, ssm, g_ssm, w_out_bf, x, mod3, ln1, wr, br, tm=128):
    bsz, seqlen, d = x.shape
    ka = attn_n.shape[-1]
    ks = w_out_bf.shape[0] - ka
    row = lambda b, i: (b, i, 0)
    const = lambda b, i: (0, 0)
    return pl.pallas_call(
        functools.partial(_outproj_kernel, nc=512),
        grid=(bsz, seqlen // tm),
        in_specs=[pl.BlockSpec((None, tm, ka), row),
                  pl.BlockSpec((None, tm, ks), row),
                  pl.BlockSpec((1, ks), const),
                  _resident((ka + ks, d), const),
                  pl.BlockSpec((None, tm, d), row),
                  pl.BlockSpec((None, 6, d), lambda b, i: (b, 0, 0)),
                  pl.BlockSpec((2, d), const),
                  pl.BlockSpec((d, 2 * N_EXPERTS), const),
                  pl.BlockSpec((1, N_EXPERTS), const)],
        out_specs=[pl.BlockSpec((None, tm, d), row),
                   pl.BlockSpec((None, tm, d // 2), row),
                   pl.BlockSpec((None, tm, N_EXPERTS), row)],
        out_shape=[jax.ShapeDtypeStruct((bsz, seqlen, d), F32),
                   jax.ShapeDtypeStruct((bsz, seqlen, d // 2), jnp.uint32),
                   jax.ShapeDtypeStruct((bsz, seqlen, N_EXPERTS), F32)],
        scratch_shapes=[pltpu.VMEM((tm, d), F32)],
        compiler_params=_cparams(("arbitrary", "arbitrary")),
        name="out_proj_ln_router",
    )(attn_n, ssm, g_ssm.reshape(1, ks), w_out_bf, x, mod3, ln1, wr, br)


def _route(logits, tm=ROUTE_TILE):
    n_tok = logits.shape[0]
    nt = n_tok // tm
    row = lambda i: (i, 0)
    k_shape = lambda dt: jax.ShapeDtypeStruct((n_tok, TOP_K), dt)
    return pl.pallas_call(
        _route_kernel,
        grid=(nt,),
        in_specs=[pl.BlockSpec((tm, N_EXPERTS), row)],
        out_specs=[pl.BlockSpec((tm, TOP_K), row), pl.BlockSpec((tm, TOP_K), row), pl.BlockSpec((tm, TOP_K), row),
                   pl.BlockSpec((None, SUBLANES, LANES), lambda i: (i, 0, 0))],
        out_shape=[k_shape(jnp.int32), k_shape(F32), k_shape(jnp.int32),
                   jax.ShapeDtypeStruct((nt, SUBLANES, LANES), F32)],
        compiler_params=_cparams(("arbitrary",)),
        name="route_topk",
    )(logits)


def _stream_expert_weights(blk_e, first, nxt, last, w_hbm, stage, wbf, sem):
    ct = pl.program_id(0)
    rb = pl.program_id(1)
    width = wbf[0].shape[1]

    def copies(e, col_tile):
        c0 = pl.multiple_of(col_tile * width, width)
        return [pltpu.make_async_copy(w.at[e, :, pl.ds(c0, width)], s, sem.at[j])
                for j, (w, s) in enumerate(zip(w_hbm, stage))]

    @pl.when((ct == 0) & (rb == 0))
    def _():
        for cp in copies(blk_e[0], 0):
            cp.start(priority=1)

    @pl.when(first[rb] == 1)
    def _():
        for cp in copies(blk_e[rb], ct):
            cp.wait()
        def convert(c, carry):
            r0 = pl.multiple_of(c * CAST_ROWS, CAST_ROWS)
            for s, w in zip(stage, wbf):
                w[pl.ds(r0, CAST_ROWS), :] = s[pl.ds(r0, CAST_ROWS), :].astype(BF16)
            return carry
        lax.fori_loop(0, stage[0].shape[0] // CAST_ROWS, convert, 0)

        @pl.when(last[rb] == 0)
        def _():
            for cp in copies(nxt[rb], ct):
                cp.start(priority=1)

        @pl.when((last[rb] == 1) & (ct + 1 < pl.num_programs(0)))
        def _():
            for cp in copies(nxt[rb], ct + 1):
                cp.start(priority=1)


def _for_live_rows(rb, nreal, short, out_ref, compute):
    full = out_ref.shape[0]
    for rows, is_short in ((full, 0), (full // 2, 1)):
        @pl.when((rb < nreal[0]) & (short[rb] == is_short))
        def _(rows=rows):
            compute(rows)
            if rows < full:
                out_ref[rows:, :] = jnp.zeros((full - rows, out_ref.shape[1]), out_ref.dtype)

    @pl.when(rb >= nreal[0])
    def _():
        out_ref[...] = jnp.zeros_like(out_ref)


def _expert_up_kernel(blk_e, first, nreal, nxt, last, short, xs_ref, wg_hbm, wu_hbm, bg_ref, bu_ref, act_ref,
                      stg_g, stg_u, wgb, wub, sem):
    rb = pl.program_id(1)
    _stream_expert_weights(blk_e, first, nxt, last, (wg_hbm, wu_hbm), (stg_g, stg_u), (wgb, wub), sem)

    def compute(rows):
        x = jnp.concatenate(_unpack_bf16_pairs(xs_ref[:rows, :]), axis=1).astype(BF16)
        g = jnp.dot(x, wgb[...], preferred_element_type=F32) + bg_ref[...]
        up = jnp.dot(x, wub[...], preferred_element_type=F32) + bu_ref[...]
        g = jnp.minimum(g, SWIGLU_LIMIT)
        up = jnp.clip(up, -SWIGLU_LIMIT, SWIGLU_LIMIT)
        act_ref[:rows, :] = (g * jax.nn.sigmoid(SWIGLU_ALPHA * g) * (up + 1.0)).astype(BF16)

    _for_live_rows(rb, nreal, short, act_ref, compute)


def _expert_down_kernel(blk_e, first, nreal, nxt, last, short, act_ref, wd_hbm, bd_ref, y_ref, stg, wdb, sem):
    rb = pl.program_id(1)
    _stream_expert_weights(blk_e, first, nxt, last, (wd_hbm,), (stg,), (wdb,), sem)

    def compute(rows):
        y = jnp.dot(act_ref[:rows, :], wdb[...], preferred_element_type=F32) + bd_ref[...]
        half = y.shape[1] // 2
        y_ref[:rows, :] = _pack_bf16_pairs(y[:, :half], y[:, half:])

    _for_live_rows(rb, nreal, short, y_ref, compute)


def _experts(xs, sched, w_gate, b_gate, w_up, b_up, w_down, b_down, tf=512, tn=DOWN_TILE):
    cap = xs.shape[0]
    n_e, d, dff = w_gate.shape
    nblk = cap // EXPERT_BLK
    hbm = pl.BlockSpec(memory_space=pl.ANY)
    act = pl.pallas_call(
        _expert_up_kernel,
        grid_spec=pltpu.PrefetchScalarGridSpec(
            num_scalar_prefetch=6,
            grid=(dff // tf, nblk),
            in_specs=[pl.BlockSpec((EXPERT_BLK, d // 2), lambda f, r, be, *_: (r, 0)),
                      hbm, hbm,
                      pl.BlockSpec((None, 1, tf), lambda f, r, be, *_: (be[r], 0, f)),
                      pl.BlockSpec((None, 1, tf), lambda f, r, be, *_: (be[r], 0, f))],
            out_specs=pl.BlockSpec((EXPERT_BLK, tf), lambda f, r, be, *_: (r, f)),
            scratch_shapes=[pltpu.VMEM((d, tf), F32), pltpu.VMEM((d, tf), F32),
                            pltpu.VMEM((d, tf), BF16), pltpu.VMEM((d, tf), BF16),
                            pltpu.SemaphoreType.DMA((2,))]),
        out_shape=jax.ShapeDtypeStruct((cap, dff), BF16),
        compiler_params=_cparams(("arbitrary", "arbitrary")),
        name="expert_gate_up",
    )(*sched, xs, w_gate, w_up, b_gate.reshape(n_e, 1, dff), b_up.reshape(n_e, 1, dff))
    ys = pl.pallas_call(
        _expert_down_kernel,
        grid_spec=pltpu.PrefetchScalarGridSpec(
            num_scalar_prefetch=6,
            grid=(d // tn, nblk),
            in_specs=[pl.BlockSpec((EXPERT_BLK, dff), lambda n, r, be, *_: (r, 0)),
                      hbm,
                      pl.BlockSpec((None, 1, tn), lambda n, r, be, *_: (be[r], 0, n))],
            out_specs=pl.BlockSpec((EXPERT_BLK, tn // 2), lambda n, r, be, *_: (r, n)),
            scratch_shapes=[pltpu.VMEM((dff, tn), F32), pltpu.VMEM((dff, tn), BF16),
                            pltpu.SemaphoreType.DMA((1,))]),
        out_shape=jax.ShapeDtypeStruct((cap, d // 2), jnp.uint32),
        compiler_params=_cparams(("arbitrary", "arbitrary")),
        name="expert_down",
    )(*sched, act, w_down, b_down.reshape(n_e, 1, d))
    return ys


def _route_tables(cnt, top_idx, rank):
    n_assign = top_idx.size
    counts = cnt[:, 0, :N_EXPERTS].astype(jnp.int32)
    tot = jnp.sum(counts, axis=0)
    padded = (tot + EXPERT_BLK - 1) // EXPERT_BLK * EXPERT_BLK
    pend = jnp.cumsum(padded)
    pstart = pend - padded
    base = pstart[None, :] + jnp.cumsum(counts, axis=0) - counts
    experts = jnp.arange(N_EXPERTS, dtype=jnp.int32)
    idx_t = top_idx.reshape(-1, ROUTE_TILE * TOP_K)
    dest = jnp.sum(jnp.where(idx_t[..., None] == experts, base[:, None, :], 0), axis=-1)
    dest = (dest.reshape(-1) + rank.reshape(-1)).astype(jnp.int32)
    cap = ((n_assign + EXPERT_BLK - 1) // EXPERT_BLK) * EXPERT_BLK + N_EXPERTS * EXPERT_BLK
    nblk = cap // EXPERT_BLK
    blk = jnp.arange(nblk, dtype=jnp.int32)
    blk_e = jnp.sum((pend[None, :] <= blk[:, None] * EXPERT_BLK).astype(jnp.int32), axis=1)
    blk_e = jnp.minimum(blk_e, N_EXPERTS - 1).astype(jnp.int32)
    nreal = (pend[-1:] // EXPERT_BLK).astype(jnp.int32)
    real = blk < nreal[0]
    first = real & jnp.concatenate([jnp.ones((1,), bool), blk_e[1:] != blk_e[:-1]])
    starts = jnp.where(first, blk, nblk)
    nxt_blk = jnp.concatenate([lax.cummin(starts, reverse=True)[1:], jnp.full((1,), nblk, jnp.int32)])
    last = nxt_blk >= nblk
    nxt = jnp.where(last, blk_e[0], blk_e[jnp.minimum(nxt_blk, nblk - 1)]).astype(jnp.int32)
    live = jnp.sum(jnp.where(blk_e[:, None] == experts[None, :], (pstart + tot)[None, :], 0), axis=1) - blk * EXPERT_BLK
    short = real & (live <= EXPERT_BLK // 2)
    sched = (blk_e, first.astype(jnp.int32), nreal, nxt, last.astype(jnp.int32), short.astype(jnp.int32))
    shift = base % GATHER_CHUNK
    span = jnp.where(counts > 0, (counts + shift + GATHER_CHUNK - 1) // GATHER_CHUNK * GATHER_CHUNK, 0)
    off = jnp.cumsum(span, axis=1) - span
    max_pieces = ROUTE_TILE * TOP_K // GATHER_CHUNK + 2 * N_EXPERTS
    piece = jnp.arange(max_pieces, dtype=jnp.int32)[None, :, None] * GATHER_CHUNK
    in_run = (piece >= off[:, None, :]) & (piece < (off + span)[:, None, :])
    piece_src = jnp.sum(jnp.where(in_run, (base - shift - off)[:, None, :] + piece, 0), axis=-1)
    n_piece = jnp.sum(span, axis=1) // GATHER_CHUNK
    runs = (n_piece.astype(jnp.int32), piece_src.reshape(-1).astype(jnp.int32),
            (off + shift).astype(jnp.int32)[:, None, :])
    return dest, cap, sched, (pstart + tot).astype(jnp.int32), (padded - tot).astype(jnp.int32), runs


def _row_copy(src, src_row, dst, dst_row, sem):
    return pltpu.make_async_copy(src.at[pl.ds(src_row, 1)], dst.at[pl.ds(dst_row, 1)], sem)


def _dispatch_kernel(pad0_ref, padn_ref, nreal_ref, hp_ref, dest_ref, xs_ref, zblk, sem, zsem, bsem):
    i = pl.program_id(0)
    tm = hp_ref.shape[0]
    nblk = xs_ref.shape[0] // EXPERT_BLK

    def for_each_pad_row(fn):
        def per_expert(e, c):
            def per_row(r, c2):
                fn(_row_copy(zblk, 0, xs_ref, pad0_ref[e] + r, zsem))
                return c2
            return lax.fori_loop(0, padn_ref[e], per_row, c)
        lax.fori_loop(0, N_EXPERTS, per_expert, 0)

    def for_each_unused_block(fn):
        def per_block(b, c):
            r0 = pl.multiple_of(b * EXPERT_BLK, EXPERT_BLK)
            fn(pltpu.make_async_copy(zblk, xs_ref.at[pl.ds(r0, EXPERT_BLK)], bsem))
            return c
        lax.fori_loop(nreal_ref[0], nblk, per_block, 0)

    @pl.when(i == 0)
    def _():
        zblk[...] = jnp.zeros_like(zblk)
        for_each_pad_row(lambda cp: cp.start())
        for_each_unused_block(lambda cp: cp.start())

    def issue(r, c):
        for k in range(TOP_K):
            _row_copy(hp_ref, r, xs_ref, dest_ref[r * TOP_K + k], sem).start(priority=k % 2)
        return c
    lax.fori_loop(0, tm, issue, 0, unroll=2)

    def drain(r, c):
        for k in range(TOP_K):
            _row_copy(hp_ref, 0, xs_ref, 0, sem).wait()
        return c
    lax.fori_loop(0, tm, drain, 0)

    @pl.when(i == 0)
    def _():
        for_each_pad_row(lambda cp: cp.wait())
        for_each_unused_block(lambda cp: cp.wait())


def _dispatch(hp, dest, pad0, padn, nreal, cap, tm=MOVE_TILE):
    n_tok, half = hp.shape
    return pl.pallas_call(
        _dispatch_kernel,
        grid_spec=pltpu.PrefetchScalarGridSpec(
            num_scalar_prefetch=3,
            grid=(n_tok // tm,),
            in_specs=[pl.BlockSpec((tm, half), lambda i, *_: (i, 0)),
                      pl.BlockSpec((tm * TOP_K,), lambda i, *_: (i,), memory_space=pltpu.SMEM)],
            out_specs=pl.BlockSpec(memory_space=pl.ANY),
            scratch_shapes=[pltpu.VMEM((EXPERT_BLK, half), jnp.uint32),
                            pltpu.SemaphoreType.DMA, pltpu.SemaphoreType.DMA, pltpu.SemaphoreType.DMA]),
        out_shape=jax.ShapeDtypeStruct((cap, half), jnp.uint32),
        compiler_params=_cparams(("arbitrary",)),
        name="moe_dispatch",
    )(pad0, padn, nreal, hp, dest)


def _combine_kernel(npiece_ref, src_ref, idx_ref, rank_ref, gate_ref, runrow_ref, x1_ref, mod_ref, ln_ref,
                    ys_ref, o_ref, buf, sem, *, tn):
    i = pl.program_id(0)
    tm = x1_ref.shape[0]
    rows = buf.shape[1]

    max_pieces = rows // GATHER_CHUNK

    def for_each_piece(tile, fn):
        slot = tile % 2

        def per_piece(p, c):
            src = pl.multiple_of(src_ref[tile * max_pieces + p], GATHER_CHUNK)
            dst = pl.multiple_of(p * GATHER_CHUNK, GATHER_CHUNK)
            fn(pltpu.make_async_copy(ys_ref.at[pl.ds(src, GATHER_CHUNK)],
                                     buf.at[slot, pl.ds(dst, GATHER_CHUNK)], sem.at[slot]))
            return c
        lax.fori_loop(0, npiece_ref[tile], per_piece, 0)

    @pl.when(i == 0)
    def _():
        buf[...] = jnp.zeros_like(buf)
        for_each_piece(0, lambda cp: cp.start())

    @pl.when(i + 1 < pl.num_programs(0))
    def _():
        for_each_piece(i + 1, lambda cp: cp.start())

    for_each_piece(i, lambda cp: cp.wait())

    lane = lax.broadcasted_iota(jnp.int32, (tm, N_EXPERTS), 1)
    col = lax.broadcasted_iota(jnp.int32, (tm, rows), 1)
    sel = jnp.zeros((tm, rows), F32)
    for k in range(TOP_K):
        run_row = jnp.sum(jnp.where(lane == idx_ref[:, k:k + 1], runrow_ref[...], 0), axis=-1, keepdims=True)
        sel = sel + jnp.where(col == run_row + rank_ref[:, k:k + 1], gate_ref[:, k:k + 1], 0.0)
    sel = sel.astype(BF16)

    cur = buf.at[i % 2]
    m = mod_ref[...]
    d = x1_ref.shape[1]
    hw = tn // 2
    cw = 512
    for w0 in range(0, d // 2, cw):
        wl, wh = _unpack_bf16_pairs(cur[:, w0:w0 + cw])
        lo = jnp.dot(sel, wl.astype(BF16), preferred_element_type=F32)
        hi = jnp.dot(sel, wh.astype(BF16), preferred_element_type=F32)
        for half, moe in ((0, lo), (1, hi)):
            c0 = (w0 // hw) * tn + half * hw + w0 % hw
            o_ref[:, c0:c0 + cw] = DN_ALPHA * x1_ref[:, c0:c0 + cw] + (1.0 + m[5:6, c0:c0 + cw]) * moe
    o_ref[...] = _layer_norm_rows(o_ref[...], ln_ref[0:1, :], ln_ref[1:2, :])


def _combine(ys, runs, top_idx, rank, gates, x1, mod3, ln2, tm=ROUTE_TILE, tn=DOWN_TILE):
    bsz, seqlen, d = x1.shape
    n_tok = bsz * seqlen
    per_b = seqlen // tm
    n_piece, piece_src, run_row = runs
    buf_rows = tm * TOP_K + N_EXPERTS * 2 * GATHER_CHUNK
    assert piece_src.shape[0] * GATHER_CHUNK == (n_tok // tm) * buf_rows
    tok = lambda i, *_: (i, 0)
    out = pl.pallas_call(
        functools.partial(_combine_kernel, tn=tn),
        grid_spec=pltpu.PrefetchScalarGridSpec(
            num_scalar_prefetch=2,
            grid=(n_tok // tm,),
            in_specs=[pl.BlockSpec((tm, TOP_K), tok), pl.BlockSpec((tm, TOP_K), tok), pl.BlockSpec((tm, TOP_K), tok),
                      pl.BlockSpec((None, 1, N_EXPERTS), lambda i, *_: (i, 0, 0)),
                      pl.BlockSpec((tm, d), tok),
                      pl.BlockSpec((None, 6, d), lambda i, *_: (i // per_b, 0, 0)),
                      pl.BlockSpec((2, d), lambda i, *_: (0, 0)),
                      pl.BlockSpec(memory_space=pl.ANY)],
            out_specs=pl.BlockSpec((tm, d), tok),
            scratch_shapes=[pltpu.VMEM((2, buf_rows, d // 2), jnp.uint32), pltpu.SemaphoreType.DMA((2,))]),
        out_shape=jax.ShapeDtypeStruct((n_tok, d), F32),
        compiler_params=_cparams(("arbitrary",)),
        name="moe_combine_ln",
    )(n_piece, piece_src, top_idx, rank, gates, run_row, x1.reshape(n_tok, d), mod3, ln2, ys)
    return out.reshape(bsz, seqlen, d)


def kernel(x, c, positions, w_ada, b_ada, w_in, attn_sinks, ssm_a_re, ssm_a_im, ssm_b_re, ssm_b_im,
           ssm_c_re, ssm_c_im, ssm_d, ssm_log_dt, ssm_w_glu, ssm_b_glu, g_attn_out, g_ssm_out, w_out,
           ln1_g, ln1_b, w_router, b_router, w_gate, b_gate, w_up, b_up, w_down, b_down, ln2_g, ln2_b):
    bsz, seqlen, d = x.shape
    lsub = seqlen // N_SUBSEQ
    n_tok = bsz * seqlen
    rope_tab = _rope_tables(positions)
    for l in range(w_ada.shape[0]):
        mod3 = _ada_mod(c, w_ada[l], b_ada[l]).reshape(bsz, 6, d)
        proj = _in_proj(x, mod3, rope_tab, w_in[l].astype(BF16))
        attn_n = _attention(proj, attn_sinks[l].astype(F32), g_attn_out[l].astype(F32))
        s5p = _s5_params(ssm_a_re[l], ssm_a_im[l], ssm_b_re[l], ssm_b_im[l], ssm_c_re[l], ssm_c_im[l],
                         ssm_d[l], ssm_log_dt[l], ssm_w_glu[l], ssm_b_glu[l], lsub)
        ssm = _s5(proj, s5p)
        wr_hi = w_router[l].astype(BF16)
        wr_lo = (w_router[l] - wr_hi.astype(F32)).astype(BF16)
        x1, hp, logits = _out_proj(
            attn_n, ssm, g_ssm_out[l].astype(F32), w_out[l].astype(BF16), x, mod3,
            jnp.stack([ln1_g[l], ln1_b[l]]).astype(F32),
            jnp.concatenate([wr_hi, wr_lo], axis=1), b_router[l].reshape(1, N_EXPERTS).astype(F32))
        top_idx, gates, rank, cnt = _route(logits.reshape(n_tok, N_EXPERTS))
        dest, cap, sched, pad0, padn, runs = _route_tables(cnt, top_idx, rank)
        xs = _dispatch(hp.reshape(n_tok, d // 2), dest, pad0, padn, sched[2], cap)
        ys = _experts(xs, sched, w_gate[l], b_gate[l], w_up[l], b_up[l], w_down[l], b_down[l])
        x = _combine(ys, runs, top_idx, rank, gates, x1, mod3, jnp.stack([ln2_g[l], ln2_b[l]]).astype(F32))
    return x
```

```python
import functools
import math

import jax
import jax.numpy as jnp
from jax import lax
from jax.experimental import pallas as pl
from jax.experimental.pallas import tpu as pltpu

F32 = jnp.float32
BF16 = jnp.bfloat16

HEAD_DIM = 64
N_Q_HEADS = 32
N_KV_HEADS = 4
GQ = N_Q_HEADS // N_KV_HEADS
ATTN_WIDTH = N_Q_HEADS * HEAD_DIM
KV_WIDTH = N_KV_HEADS * HEAD_DIM
QKV_WIDTH = ATTN_WIDTH + 2 * KV_WIDTH
BLK = 128
ROT_DIM = HEAD_DIM // 4
ROPE_THETA = 500000.0
GROUP_CH = 16
STATE = 64
N_EXPERTS = 32
TOP_K = 4
SWIGLU_LIMIT = 7.0
SWIGLU_ALPHA = 1.702
EXPERT_BLK = 256
ROUTE_TILE = 256
GATHER_CHUNK = 8
MOVE_TILE = 1024
DOWN_TILE = 4096
S5_STEPS = 512
CAST_ROWS = 128
DEPTH = 1
DN_ALPHA = (2.0 * DEPTH) ** 0.25
EPS = 1e-5

LANES = 128
SUBLANES = 8
N_SUBSEQ = SUBLANES
GROUPS_PER_BLK = 16
SSM_BLK_IN = GROUPS_PER_BLK * GROUP_CH
SSM_BLK_ST = GROUPS_PER_BLK * STATE
VMEM_LIMIT = 56 * 1024 * 1024


def _cparams(sem, vmem=VMEM_LIMIT):
    return pltpu.CompilerParams(dimension_semantics=sem, vmem_limit_bytes=vmem)


def _resident(shape, index_map):
    return pl.BlockSpec(shape, index_map, pipeline_mode=pl.Buffered(1))


def _ada_kernel(c_ref, w_ref, b_ref, o_ref):
    c = c_ref[...]
    ca = c * jax.nn.sigmoid(c)
    o_ref[...] = jnp.dot(ca.astype(BF16), w_ref[...].astype(BF16),
                         preferred_element_type=F32) + b_ref[...]


def _ada_mod(c, w_ada, b_ada, tn=1024):
    bsz, d = c.shape
    n = w_ada.shape[1]
    c8 = jnp.zeros((SUBLANES, d), F32).at[:bsz].set(c)
    out = pl.pallas_call(
        _ada_kernel,
        grid=(n // tn,),
        in_specs=[pl.BlockSpec((SUBLANES, d), lambda j: (0, 0)),
                  pl.BlockSpec((d, tn), lambda j: (0, j)),
                  pl.BlockSpec((1, tn), lambda j: (0, j))],
        out_specs=pl.BlockSpec((SUBLANES, tn), lambda j: (0, j)),
        out_shape=jax.ShapeDtypeStruct((SUBLANES, n), F32),
        compiler_params=_cparams(("arbitrary",)),
        name="ada_mod",
    )(c8, w_ada, b_ada.reshape(1, n))
    return out[:bsz]


def _rope(t, tab):
    c, s_lo, s_hi = tab[:, :LANES], tab[:, LANES:2 * LANES], tab[:, 2 * LANES:]
    half = ROT_DIM // 2
    out = []
    for j in range(t.shape[1] // LANES):
        tj = t[:, j * LANES:(j + 1) * LANES]
        out.append(tj * c + pltpu.roll(tj, LANES - half, 1) * s_lo + pltpu.roll(tj, half, 1) * s_hi)
    return jnp.concatenate(out, axis=1)


def _inproj_kernel(x_ref, mod_ref, tab_ref, w_ref, o_ref, *, nc):
    m = mod_ref[...]
    h = (x_ref[...] * (1.0 + m[1:2, :]) + m[0:1, :]).astype(BF16)
    for n0 in range(0, o_ref.shape[-1], nc):
        p = jnp.dot(h, w_ref[:, n0:n0 + nc], preferred_element_type=F32)
        if n0 < ATTN_WIDTH:
            p = _rope(p * (HEAD_DIM ** -0.5), tab_ref[...])
        elif n0 == ATTN_WIDTH:
            p = jnp.concatenate([_rope(p[:, :KV_WIDTH], tab_ref[...]), p[:, KV_WIDTH:]], axis=1)
        o_ref[:, n0:n0 + nc] = p.astype(BF16)


def _in_proj(x, mod3, rope_tab, w_in_bf, tm=256, nc=512):
    bsz, seqlen, d = x.shape
    n_in = w_in_bf.shape[1]
    assert ATTN_WIDTH % nc == 0 and nc >= 2 * KV_WIDTH
    return pl.pallas_call(
        functools.partial(_inproj_kernel, nc=nc),
        grid=(bsz, seqlen // tm),
        in_specs=[pl.BlockSpec((None, tm, d), lambda b, i: (b, i, 0)),
                  pl.BlockSpec((None, 6, d), lambda b, i: (b, 0, 0)),
                  pl.BlockSpec((None, tm, 3 * LANES), lambda b, i: (b, i, 0)),
                  _resident((d, n_in), lambda b, i: (0, 0))],
        out_specs=pl.BlockSpec((None, tm, n_in), lambda b, i: (b, i, 0)),
        out_shape=jax.ShapeDtypeStruct((bsz, seqlen, n_in), BF16),
        compiler_params=_cparams(("arbitrary", "arbitrary")),
        name="in_proj",
    )(x, mod3, rope_tab, w_in_bf)


def _attn_kernel(sink_ref, q_ref, kc_ref, kp_ref, vc_ref, vp_ref, g_ref, o_ref):
    n = pl.program_id(1)
    low = lax.broadcasted_iota(jnp.int32, (2 * BLK, LANES), 1) < HEAD_DIM
    low_q = lax.broadcasted_iota(jnp.int32, (BLK, LANES), 1) < HEAD_DIM

    k_raw = jnp.concatenate([kp_ref[...], kc_ref[...]], axis=0).astype(F32)
    v_raw = jnp.concatenate([vp_ref[...], vc_ref[...]], axis=0).astype(F32)

    qi = lax.broadcasted_iota(jnp.int32, (BLK, BLK), 0)
    kj = lax.broadcasted_iota(jnp.int32, (BLK, BLK), 1)
    own = kj <= qi
    prev_ok = kj >= jnp.where(n > 0, 0, BLK)

    o_chunks = []
    for hk in range(N_KV_HEADS):
        kc = k_raw[:, (hk // 2) * LANES:(hk // 2 + 1) * LANES]
        vc = v_raw[:, (hk // 2) * LANES:(hk // 2 + 1) * LANES]
        k_sw = pltpu.roll(kc, HEAD_DIM, 1)
        v_sw = pltpu.roll(vc, HEAD_DIM, 1)
        if hk % 2 == 0:
            kk2 = jnp.where(low, kc, k_sw)
            v_lo = jnp.where(low, vc, 0.0)
            v_hi = jnp.where(low, 0.0, v_sw)
        else:
            kk2 = jnp.where(low, k_sw, kc)
            v_lo = jnp.where(low, v_sw, 0.0)
            v_hi = jnp.where(low, 0.0, vc)
        kk2 = kk2.astype(BF16)
        v_lo = v_lo.astype(BF16)
        v_hi = v_hi.astype(BF16)
        lhs = []
        for j in range(GQ // 2):
            c0 = (hk * (GQ // 2) + j) * LANES
            q2 = q_ref[:, c0:c0 + LANES].astype(F32)
            lhs.append(jnp.where(low_q, q2, 0.0).astype(BF16))
            lhs.append(jnp.where(low_q, 0.0, q2).astype(BF16))
        s_all = lax.dot_general(jnp.concatenate(lhs, axis=0), kk2,
                                (((1,), (1,)), ((), ())), preferred_element_type=F32)
        for j in range(GQ // 2):
            acc = None
            for side, vv in ((0, v_lo), (1, v_hi)):
                i = 2 * j + side
                s_prev = jnp.where(prev_ok, s_all[i * BLK:(i + 1) * BLK, :BLK], -1e30)
                s = jnp.where(own, s_all[i * BLK:(i + 1) * BLK, BLK:], s_prev)
                sink = sink_ref[hk * GQ + i]
                m = jnp.maximum(jnp.max(s, axis=-1, keepdims=True), sink)
                p = jnp.exp(s - m)
                denom = jnp.sum(p, axis=-1, keepdims=True) + jnp.exp(sink - m)
                p = p * (1.0 / denom)
                p = jnp.concatenate([jnp.where(own, 0.0, p), jnp.where(own, p, 0.0)], axis=1).astype(BF16)
                o = jnp.dot(p, vv, preferred_element_type=F32)
                acc = o if acc is None else acc + o
            o_chunks.append(acc)

    ssq = None
    for oc in o_chunks:
        t = jnp.sum(oc * oc, axis=-1, keepdims=True)
        ssq = t if ssq is None else ssq + t
    inv = lax.rsqrt(ssq * (1.0 / ATTN_WIDTH) + EPS)
    for j, oc in enumerate(o_chunks):
        o_ref[:, j * LANES:(j + 1) * LANES] = (oc * inv * g_ref[:, j * LANES:(j + 1) * LANES]).astype(BF16)


def _attention(qkv, sinks, g_attn):
    bsz, seqlen, _ = qkv.shape
    nb = seqlen // BLK
    kcol = ATTN_WIDTH // KV_WIDTH
    cur = lambda b, n: (b, n, 0)
    return pl.pallas_call(
        _attn_kernel,
        grid=(bsz, nb),
        in_specs=[pl.BlockSpec(memory_space=pltpu.SMEM),
                  pl.BlockSpec((None, BLK, ATTN_WIDTH), cur),
                  pl.BlockSpec((None, BLK, KV_WIDTH), lambda b, n: (b, n, kcol)),
                  pl.BlockSpec((None, BLK, KV_WIDTH), lambda b, n: (b, jnp.maximum(n - 1, 0), kcol)),
                  pl.BlockSpec((None, BLK, KV_WIDTH), lambda b, n: (b, n, kcol + 1)),
                  pl.BlockSpec((None, BLK, KV_WIDTH), lambda b, n: (b, jnp.maximum(n - 1, 0), kcol + 1)),
                  pl.BlockSpec((1, ATTN_WIDTH), lambda b, n: (0, 0))],
        out_specs=pl.BlockSpec((None, BLK, ATTN_WIDTH), cur),
        out_shape=jax.ShapeDtypeStruct((bsz, seqlen, ATTN_WIDTH), BF16),
        compiler_params=_cparams(("arbitrary", "arbitrary")),
        name="swa_attention",
    )(sinks, qkv, qkv, qkv, qkv, qkv, g_attn.reshape(1, ATTN_WIDTH))


def _rope_tables(positions):
    half = ROT_DIM // 2
    inv_freq = ROPE_THETA ** (-jnp.arange(0, ROT_DIM, 2, dtype=F32) / ROT_DIM)
    ang = positions.astype(F32)[..., None] * inv_freq
    cos, sin = jnp.cos(ang), jnp.sin(ang)
    shp = cos.shape[:-1] + (HEAD_DIM - ROT_DIM,)
    c = jnp.concatenate([cos, cos, jnp.ones(shp, F32)], axis=-1)
    z8 = jnp.zeros_like(sin)
    s_lo = jnp.concatenate([-sin, z8, jnp.zeros(shp, F32)], axis=-1)
    s_hi = jnp.concatenate([z8, sin, jnp.zeros(shp, F32)], axis=-1)
    rep = LANES // HEAD_DIM
    return jnp.concatenate([jnp.tile(c, rep), jnp.tile(s_lo, rep), jnp.tile(s_hi, rep)], axis=-1)


def _s5_scan(buf, ar, ai, hr, hi, ti, store):
    def step(i, carry):
        hr, hi = carry
        r0 = pl.multiple_of(i * SUBLANES, SUBLANES)
        row = buf[pl.ds(r0, SUBLANES), :]
        nhr = ar * hr - ai * hi + row[:, :SSM_BLK_ST]
        nhi = ar * hi + ai * hr + row[:, SSM_BLK_ST:]
        if store:
            buf[pl.ds(r0, SUBLANES), :] = jnp.concatenate([nhr, nhi], axis=1)
        return nhr, nhi
    return lax.fori_loop(0, ti, step, (hr, hi), unroll=4)


def _time_major(u_ref, ti):
    u = pltpu.einshape("jid->ijd", u_ref[...].astype(F32))
    return u.reshape(ti * N_SUBSEQ, u.shape[-1])


def _s5_pass1_kernel(u_ref, bdb_ref, a_ref, f_ref, buf, hst, *, ti):
    ic = pl.program_id(2)

    @pl.when(ic == 0)
    def _():
        hst[...] = jnp.zeros_like(hst)

    u = _time_major(u_ref, ti)
    buf[...] = jnp.dot(u.astype(BF16), bdb_ref[...], preferred_element_type=F32)
    ar = jnp.broadcast_to(a_ref[0:1, :], (SUBLANES, SSM_BLK_ST))
    ai = jnp.broadcast_to(a_ref[1:2, :], (SUBLANES, SSM_BLK_ST))
    hr, hi = _s5_scan(buf, ar, ai, hst[:, :SSM_BLK_ST], hst[:, SSM_BLK_ST:], ti, store=False)
    hst[...] = jnp.concatenate([hr, hi], axis=1)

    @pl.when(ic == pl.num_programs(2) - 1)
    def _():
        f_ref[...] = hst[...]


def _s5_pass2_kernel(u_ref, f_ref, bdb_ref, a_ref, bdc_ref, glu_ref, vec_ref, o_ref, buf, hst, *, ti):
    ic = pl.program_id(2)

    @pl.when(ic == 0)
    def _():
        fr, fi = f_ref[:, :SSM_BLK_ST], f_ref[:, SSM_BLK_ST:]
        pr = jnp.broadcast_to(a_ref[2:3, :], (SUBLANES, SSM_BLK_ST))
        pi = jnp.broadcast_to(a_ref[3:4, :], (SUBLANES, SSM_BLK_ST))
        row = lax.broadcasted_iota(jnp.int32, (SUBLANES, SSM_BLK_ST), 0)
        hr = jnp.zeros((SUBLANES, SSM_BLK_ST), F32)
        hi = jnp.zeros((SUBLANES, SSM_BLK_ST), F32)
        for _ in range(N_SUBSEQ - 1):
            nr = pr * hr - pi * hi + fr
            ni = pr * hi + pi * hr + fi
            hr = jnp.where(row == 0, 0.0, pltpu.roll(nr, 1, 0))
            hi = jnp.where(row == 0, 0.0, pltpu.roll(ni, 1, 0))
        hst[...] = jnp.concatenate([hr, hi], axis=1)

    u = _time_major(u_ref, ti)
    buf[...] = jnp.dot(u.astype(BF16), bdb_ref[...], preferred_element_type=F32)
    ar = jnp.broadcast_to(a_ref[0:1, :], (SUBLANES, SSM_BLK_ST))
    ai = jnp.broadcast_to(a_ref[1:2, :], (SUBLANES, SSM_BLK_ST))
    hr, hi = _s5_scan(buf, ar, ai, hst[:, :SSM_BLK_ST], hst[:, SSM_BLK_ST:], ti, store=True)
    hst[...] = jnp.concatenate([hr, hi], axis=1)

    y = jnp.dot(buf[...].astype(BF16), bdc_ref[...], preferred_element_type=F32)
    y = jax.nn.gelu(y + vec_ref[0:1, :SSM_BLK_IN] * u)
    z = jnp.dot(y.astype(BF16), glu_ref[...], preferred_element_type=F32) + vec_ref[1:2, :]
    out = z[:, :SSM_BLK_IN] * jax.nn.sigmoid(z[:, SSM_BLK_IN:])
    out = pltpu.einshape("ijd->jid", out.reshape(ti, N_SUBSEQ, SSM_BLK_IN))
    o_ref[...] = out.astype(BF16)


def _s5_params(a_re, a_im, b_re, b_im, c_re, c_im, d_skip, log_dt, w_glu, b_glu, lsub):
    g = a_re.shape[0]
    nf = g // GROUPS_PER_BLK
    a = lax.complex(a_re.astype(F32), a_im.astype(F32))
    dt = jnp.exp(log_dt.astype(F32))[:, None]
    a_bar = jnp.exp(a * dt)
    a_pow = jnp.exp(a * dt * lsub)
    b_bar = ((a_bar - 1.0) / a)[..., None] * lax.complex(b_re.astype(F32), b_im.astype(F32))
    def block_diag(m, inner):
        rows = m.shape[1]
        tile = jnp.tile(jnp.eye(inner, dtype=F32), (1, GROUPS_PER_BLK))
        wide = jnp.einsum('frk,kn->frn', m, tile, precision=lax.Precision.HIGHEST)
        rg = jnp.arange(rows, dtype=jnp.int32)[:, None] // (rows // GROUPS_PER_BLK)
        cg = jnp.arange(GROUPS_PER_BLK * inner, dtype=jnp.int32)[None, :] // inner
        return jnp.where(rg == cg, wide, 0.0)

    def bd_in(m):
        return block_diag(jnp.swapaxes(m, 1, 2).reshape(nf, SSM_BLK_IN, STATE), STATE)

    def bd_out(m):
        return block_diag(jnp.swapaxes(m, 1, 2).reshape(nf, SSM_BLK_ST, GROUP_CH), GROUP_CH)

    def bd_glu(m):
        return block_diag(m.reshape(nf, SSM_BLK_IN, GROUP_CH), GROUP_CH)

    bdb = jnp.concatenate([bd_in(jnp.real(b_bar)), bd_in(jnp.imag(b_bar))], axis=2).astype(BF16)
    bdc = jnp.concatenate([bd_out(c_re.astype(F32)), bd_out(-c_im.astype(F32))], axis=1).astype(BF16)
    wg = w_glu.astype(F32)
    glu = jnp.concatenate([bd_glu(wg[..., :GROUP_CH]), bd_glu(wg[..., GROUP_CH:])], axis=2).astype(BF16)
    flat = lambda m: m.reshape(nf, 1, SSM_BLK_ST)
    avec = jnp.concatenate([flat(jnp.real(a_bar)), flat(jnp.imag(a_bar)),
                            flat(jnp.real(a_pow)), flat(jnp.imag(a_pow))], axis=1)
    bg = b_glu.astype(F32).reshape(nf, GROUPS_PER_BLK, 2 * GROUP_CH)
    bvec = jnp.concatenate([bg[..., :GROUP_CH].reshape(nf, 1, SSM_BLK_IN),
                            bg[..., GROUP_CH:].reshape(nf, 1, SSM_BLK_IN)], axis=2)
    dvec = jnp.concatenate([d_skip.astype(F32).reshape(nf, 1, SSM_BLK_IN),
                            jnp.zeros((nf, 1, SSM_BLK_IN), F32)], axis=2)
    vec = jnp.concatenate([dvec, bvec], axis=1)
    return bdb, bdc, glu, avec, vec


def _s5(proj, params, ti=S5_STEPS):
    bdb, bdc, glu, avec, vec = params
    bsz, seqlen, n_in = proj.shape
    width = n_in - QKV_WIDTH
    nf = width // SSM_BLK_IN
    lsub = seqlen // N_SUBSEQ
    ti = min(ti, lsub)
    u_col0 = QKV_WIDTH // SSM_BLK_IN
    p4 = proj.reshape(bsz, N_SUBSEQ, lsub, n_in)
    grid = (bsz, nf, lsub // ti)
    u_spec = pl.BlockSpec((None, N_SUBSEQ, ti, SSM_BLK_IN), lambda b, f, i: (b, 0, i, u_col0 + f))
    o_spec = pl.BlockSpec((None, N_SUBSEQ, ti, SSM_BLK_IN), lambda b, f, i: (b, 0, i, f))
    blk = lambda r, c: pl.BlockSpec((None, r, c), lambda b, f, i: (f, 0, 0))
    f_spec = pl.BlockSpec((None, None, N_SUBSEQ, 2 * SSM_BLK_ST), lambda b, f, i: (b, f, 0, 0))
    scratch = [pltpu.VMEM((ti * SUBLANES, 2 * SSM_BLK_ST), F32),
               pltpu.VMEM((SUBLANES, 2 * SSM_BLK_ST), F32)]
    sem = ("arbitrary", "arbitrary", "arbitrary")
    fin = pl.pallas_call(
        functools.partial(_s5_pass1_kernel, ti=ti),
        grid=grid,
        in_specs=[u_spec, blk(SSM_BLK_IN, 2 * SSM_BLK_ST), blk(4, SSM_BLK_ST)],
        out_specs=f_spec,
        out_shape=jax.ShapeDtypeStruct((bsz, nf, N_SUBSEQ, 2 * SSM_BLK_ST), F32),
        scratch_shapes=scratch,
        compiler_params=_cparams(sem),
        name="s5_pass1",
    )(p4, bdb, avec)
    out = pl.pallas_call(
        functools.partial(_s5_pass2_kernel, ti=ti),
        grid=grid,
        in_specs=[u_spec, f_spec, blk(SSM_BLK_IN, 2 * SSM_BLK_ST), blk(4, SSM_BLK_ST),
                  blk(2 * SSM_BLK_ST, SSM_BLK_IN), blk(SSM_BLK_IN, 2 * SSM_BLK_IN),
                  blk(2, 2 * SSM_BLK_IN)],
        out_specs=o_spec,
        out_shape=jax.ShapeDtypeStruct((bsz, N_SUBSEQ, lsub, width), BF16),
        scratch_shapes=scratch,
        compiler_params=_cparams(sem),
        name="s5_pass2",
    )(p4, fin, bdb, avec, bdc, glu, vec)
    return out.reshape(bsz, seqlen, width)


def _layer_norm_rows(y, g, b):
    mu = jnp.mean(y, axis=-1, keepdims=True)
    yc = y - mu
    var = jnp.mean(yc * yc, axis=-1, keepdims=True)
    return yc * lax.rsqrt(var + EPS) * g + b


def _pack_bf16_pairs(lo, hi):
    lo_bits = lax.bitcast_convert_type(lo.astype(BF16).astype(F32), jnp.uint32)
    hi_bits = lax.bitcast_convert_type(hi.astype(BF16).astype(F32), jnp.uint32)
    return (lo_bits >> 16) | (hi_bits & jnp.uint32(0xFFFF0000))


def _unpack_bf16_pairs(w):
    lo = lax.bitcast_convert_type(w << 16, F32)
    hi = lax.bitcast_convert_type(w & jnp.uint32(0xFFFF0000), F32)
    return lo, hi


def _outproj_kernel(attn_ref, ssm_ref, gs_ref, w_ref, x_ref, mod_ref, ln_ref, wr_ref, br_ref,
                    x1_ref, hp_ref, logit_ref, ybuf, *, nc):
    m = mod_ref[...]
    ssm = ssm_ref[...].astype(F32)
    ms = jnp.mean(ssm * ssm, axis=-1, keepdims=True)
    ssm_n = (ssm * lax.rsqrt(ms + EPS) * gs_ref[...]).astype(BF16)
    attn = attn_ref[...]
    ka = attn.shape[1]
    d = x_ref.shape[1]
    for n0 in range(0, d, nc):
        mix = (jnp.dot(attn, w_ref[:ka, n0:n0 + nc], preferred_element_type=F32)
               + jnp.dot(ssm_n, w_ref[ka:, n0:n0 + nc], preferred_element_type=F32))
        ybuf[:, n0:n0 + nc] = DN_ALPHA * x_ref[:, n0:n0 + nc] + (1.0 + m[2:3, n0:n0 + nc]) * mix
    x1 = _layer_norm_rows(ybuf[...], ln_ref[0:1, :], ln_ref[1:2, :])
    x1_ref[...] = x1
    h2 = x1 * (1.0 + m[4:5, :]) + m[3:4, :]
    hi = h2.astype(BF16)
    hi_f32 = hi.astype(F32)
    bits = lax.bitcast_convert_type(hi_f32, jnp.uint32)
    hp_ref[...] = (bits[:, :d // 2] >> 16) | (bits[:, d // 2:] & jnp.uint32(0xFFFF0000))
    lo = (h2 - hi_f32).astype(BF16)
    tm = h2.shape[0]
    r = jnp.dot(jnp.concatenate([hi, lo], axis=0), wr_ref[...], preferred_element_type=F32)
    logit_ref[...] = r[:tm, :N_EXPERTS] + r[:tm, N_EXPERTS:] + r[tm:, :N_EXPERTS] + br_ref[...]


def _route_kernel(logit_ref, idx_ref, gate_ref, rank_ref, cnt_ref):
    logits = logit_ref[...]
    tm = logits.shape[0]
    lane = lax.broadcasted_iota(jnp.int32, logits.shape, 1)
    vals, idxs = [], []
    for _ in range(TOP_K):
        mx = jnp.max(logits, axis=-1, keepdims=True)
        ix = jnp.min(jnp.where(logits == mx, lane, N_EXPERTS), axis=-1, keepdims=True)
        vals.append(mx)
        idxs.append(ix)
        logits = jnp.where(lane == ix, -jnp.inf, logits)
    tv = jnp.concatenate(vals, axis=1)
    e = jnp.exp(tv - vals[0])
    gate_ref[...] = e / jnp.sum(e, axis=-1, keepdims=True)
    idx_ref[...] = jnp.concatenate(idxs, axis=1)
    tri = (lax.broadcasted_iota(jnp.int32, (tm, tm), 0) > lax.broadcasted_iota(jnp.int32, (tm, tm), 1)).astype(BF16)
    run = jnp.zeros((1, N_EXPERTS), F32)
    ranks = []
    for ix in idxs:
        onehot = (lane == ix).astype(F32)
        before = jnp.dot(tri, onehot.astype(BF16), preferred_element_type=F32) + run
        ranks.append(jnp.sum(onehot * before, axis=-1, keepdims=True))
        run = run + jnp.sum(onehot, axis=0, keepdims=True)
    rank_ref[...] = jnp.concatenate(ranks, axis=1).astype(jnp.int32)
    cnt_ref[...] = jnp.zeros_like(cnt_ref)
    cnt_ref[0:1, 0:N_EXPERTS] = run


def _out_proj(attn_n, ssm, g_ssm, w_out_bf, x, mod3, ln1, wr, br, tm=128):
    bsz, seqlen, d = x.shape
    ka = attn_n.shape[-1]
    ks = w_out_bf.shape[0] - ka
    row = lambda b, i: (b, i, 0)
    const = lambda b, i: (0, 0)
    return pl.pallas_call(
        functools.partial(_outproj_kernel, nc=512),
        grid=(bsz, seqlen // tm),
        in_specs=[pl.BlockSpec((None, tm, ka), row),
                  pl.BlockSpec((None, tm, ks), row),
                  pl.BlockSpec((1, ks), const),
                  _resident((ka + ks, d), const),
                  pl.BlockSpec((None, tm, d), row),
                  pl.BlockSpec((None, 6, d), lambda b, i: (b, 0, 0)),
                  pl.BlockSpec((2, d), const),
                  pl.BlockSpec((d, 2 * N_EXPERTS), const),
                  pl.BlockSpec((1, N_EXPERTS), const)],
        out_specs=[pl.BlockSpec((None, tm, d), row),
                   pl.BlockSpec((None, tm, d // 2), row),
                   pl.BlockSpec((None, tm, N_EXPERTS), row)],
        out_shape=[jax.ShapeDtypeStruct((bsz, seqlen, d), F32),
                   jax.ShapeDtypeStruct((bsz, seqlen, d // 2), jnp.uint32),
                   jax.ShapeDtypeStruct((bsz, seqlen, N_EXPERTS), F32)],
        scratch_shapes=[pltpu.VMEM((tm, d), F32)],
        compiler_params=_cparams(("arbitrary", "arbitrary")),
        name="out_proj_ln_router",
    )(attn_n, ssm, g_ssm.reshape(1, ks), w_out_bf, x, mod3, ln1, wr, br)


def _route(logits, tm=ROUTE_TILE):
    n_tok = logits.shape[0]
    nt = n_tok // tm
    row = lambda i: (i, 0)
    k_shape = lambda dt: jax.ShapeDtypeStruct((n_tok, TOP_K), dt)
    return pl.pallas_call(
        _route_kernel,
        grid=(nt,),
        in_specs=[pl.BlockSpec((tm, N_EXPERTS), row)],
        out_specs=[pl.BlockSpec((tm, TOP_K), row), pl.BlockSpec((tm, TOP_K), row), pl.BlockSpec((tm, TOP_K), row),
                   pl.BlockSpec((None, SUBLANES, LANES), lambda i: (i, 0, 0))],
        out_shape=[k_shape(jnp.int32), k_shape(F32), k_shape(jnp.int32),
                   jax.ShapeDtypeStruct((nt, SUBLANES, LANES), F32)],
        compiler_params=_cparams(("arbitrary",)),
        name="route_topk",
    )(logits)


def _stream_expert_weights(blk_e, first, nxt, last, w_hbm, stage, wbf, sem):
    ct = pl.program_id(0)
    rb = pl.program_id(1)
    width = wbf[0].shape[1]

    def copies(e, col_tile):
        c0 = pl.multiple_of(col_tile * width, width)
        return [pltpu.make_async_copy(w.at[e, :, pl.ds(c0, width)], s, sem.at[j])
                for j, (w, s) in enumerate(zip(w_hbm, stage))]

    @pl.when((ct == 0) & (rb == 0))
    def _():
        for cp in copies(blk_e[0], 0):
            cp.start(priority=1)

    @pl.when(first[rb] == 1)
    def _():
        for cp in copies(blk_e[rb], ct):
            cp.wait()
        def convert(c, carry):
            r0 = pl.multiple_of(c * CAST_ROWS, CAST_ROWS)
            for s, w in zip(stage, wbf):
                w[pl.ds(r0, CAST_ROWS), :] = s[pl.ds(r0, CAST_ROWS), :].astype(BF16)
            return carry
        lax.fori_loop(0, stage[0].shape[0] // CAST_ROWS, convert, 0)

        @pl.when(last[rb] == 0)
        def _():
            for cp in copies(nxt[rb], ct):
                cp.start(priority=1)

        @pl.when((last[rb] == 1) & (ct + 1 < pl.num_programs(0)))
        def _():
            for cp in copies(nxt[rb], ct + 1):
                cp.start(priority=1)


def _for_live_rows(rb, nreal, short, out_ref, compute):
    full = out_ref.shape[0]
    for rows, is_short in ((full, 0), (full // 2, 1)):
        @pl.when((rb < nreal[0]) & (short[rb] == is_short))
        def _(rows=rows):
            compute(rows)
            if rows < full:
                out_ref[rows:, :] = jnp.zeros((full - rows, out_ref.shape[1]), out_ref.dtype)

    @pl.when(rb >= nreal[0])
    def _():
        out_ref[...] = jnp.zeros_like(out_ref)


def _expert_up_kernel(blk_e, first, nreal, nxt, last, short, xs_ref, wg_hbm, wu_hbm, bg_ref, bu_ref, act_ref,
                      stg_g, stg_u, wgb, wub, sem):
    rb = pl.program_id(1)
    _stream_expert_weights(blk_e, first, nxt, last, (wg_hbm, wu_hbm), (stg_g, stg_u), (wgb, wub), sem)

    def compute(rows):
        x = jnp.concatenate(_unpack_bf16_pairs(xs_ref[:rows, :]), axis=1).astype(BF16)
        g = jnp.dot(x, wgb[...], preferred_element_type=F32) + bg_ref[...]
        up = jnp.dot(x, wub[...], preferred_element_type=F32) + bu_ref[...]
        g = jnp.minimum(g, SWIGLU_LIMIT)
        up = jnp.clip(up, -SWIGLU_LIMIT, SWIGLU_LIMIT)
        act_ref[:rows, :] = (g * jax.nn.sigmoid(SWIGLU_ALPHA * g) * (up + 1.0)).astype(BF16)

    _for_live_rows(rb, nreal, short, act_ref, compute)


def _expert_down_kernel(blk_e, first, nreal, nxt, last, short, act_ref, wd_hbm, bd_ref, y_ref, stg, wdb, sem):
    rb = pl.program_id(1)
    _stream_expert_weights(blk_e, first, nxt, last, (wd_hbm,), (stg,), (wdb,), sem)

    def compute(rows):
        y = jnp.dot(act_ref[:rows, :], wdb[...], preferred_element_type=F32) + bd_ref[...]
        half = y.shape[1] // 2
        y_ref[:rows, :] = _pack_bf16_pairs(y[:, :half], y[:, half:])

    _for_live_rows(rb, nreal, short, y_ref, compute)


def _experts(xs, sched, w_gate, b_gate, w_up, b_up, w_down, b_down, tf=512, tn=DOWN_TILE):
    cap = xs.shape[0]
    n_e, d, dff = w_gate.shape
    nblk = cap // EXPERT_BLK
    hbm = pl.BlockSpec(memory_space=pl.ANY)
    act = pl.pallas_call(
        _expert_up_kernel,
        grid_spec=pltpu.PrefetchScalarGridSpec(
            num_scalar_prefetch=6,
            grid=(dff // tf, nblk),
            in_specs=[pl.BlockSpec((EXPERT_BLK, d // 2), lambda f, r, be, *_: (r, 0)),
                      hbm, hbm,
                      pl.BlockSpec((None, 1, tf), lambda f, r, be, *_: (be[r], 0, f)),
                      pl.BlockSpec((None, 1, tf), lambda f, r, be, *_: (be[r], 0, f))],
            out_specs=pl.BlockSpec((EXPERT_BLK, tf), lambda f, r, be, *_: (r, f)),
            scratch_shapes=[pltpu.VMEM((d, tf), F32), pltpu.VMEM((d, tf), F32),
                            pltpu.VMEM((d, tf), BF16), pltpu.VMEM((d, tf), BF16),
                            pltpu.SemaphoreType.DMA((2,))]),
        out_shape=jax.ShapeDtypeStruct((cap, dff), BF16),
        compiler_params=_cparams(("arbitrary", "arbitrary")),
        name="expert_gate_up",
    )(*sched, xs, w_gate, w_up, b_gate.reshape(n_e, 1, dff), b_up.reshape(n_e, 1, dff))
    ys = pl.pallas_call(
        _expert_down_kernel,
        grid_spec=pltpu.PrefetchScalarGridSpec(
            num_scalar_prefetch=6,
            grid=(d // tn, nblk),
            in_specs=[pl.BlockSpec((EXPERT_BLK, dff), lambda n, r, be, *_: (r, 0)),
                      hbm,
                      pl.BlockSpec((None, 1, tn), lambda n, r, be, *_: (be[r], 0, n))],
            out_specs=pl.BlockSpec((EXPERT_BLK, tn // 2), lambda n, r, be, *_: (r, n)),
            scratch_shapes=[pltpu.VMEM((dff, tn), F32), pltpu.VMEM((dff, tn), BF16),
                            pltpu.SemaphoreType.DMA((1,))]),
        out_shape=jax.ShapeDtypeStruct((cap, d // 2), jnp.uint32),
        compiler_params=_cparams(("arbitrary", "arbitrary")),
        name="expert_down",
    )(*sched, act, w_down, b_down.reshape(n_e, 1, d))
    return ys


def _route_tables(cnt, top_idx, rank):
    n_assign = top_idx.size
    counts = cnt[:, 0, :N_EXPERTS].astype(jnp.int32)
    tot = jnp.sum(counts, axis=0)
    padded = (tot + EXPERT_BLK - 1) // EXPERT_BLK * EXPERT_BLK
    pend = jnp.cumsum(padded)
    pstart = pend - padded
    base = pstart[None, :] + jnp.cumsum(counts, axis=0) - counts
    experts = jnp.arange(N_EXPERTS, dtype=jnp.int32)
    idx_t = top_idx.reshape(-1, ROUTE_TILE * TOP_K)
    dest = jnp.sum(jnp.where(idx_t[..., None] == experts, base[:, None, :], 0), axis=-1)
    dest = (dest.reshape(-1) + rank.reshape(-1)).astype(jnp.int32)
    cap = ((n_assign + EXPERT_BLK - 1) // EXPERT_BLK) * EXPERT_BLK + N_EXPERTS * EXPERT_BLK
    nblk = cap // EXPERT_BLK
    blk = jnp.arange(nblk, dtype=jnp.int32)
    blk_e = jnp.sum((pend[None, :] <= blk[:, None] * EXPERT_BLK).astype(jnp.int32), axis=1)
    blk_e = jnp.minimum(blk_e, N_EXPERTS - 1).astype(jnp.int32)
    nreal = (pend[-1:] // EXPERT_BLK).astype(jnp.int32)
    real = blk < nreal[0]
    first = real & jnp.concatenate([jnp.ones((1,), bool), blk_e[1:] != blk_e[:-1]])
    starts = jnp.where(first, blk, nblk)
    nxt_blk = jnp.concatenate([lax.cummin(starts, reverse=True)[1:], jnp.full((1,), nblk, jnp.int32)])
    last = nxt_blk >= nblk
    nxt = jnp.where(last, blk_e[0], blk_e[jnp.minimum(nxt_blk, nblk - 1)]).astype(jnp.int32)
    live = jnp.sum(jnp.where(blk_e[:, None] == experts[None, :], (pstart + tot)[None, :], 0), axis=1) - blk * EXPERT_BLK
    short = real & (live <= EXPERT_BLK // 2)
    sched = (blk_e, first.astype(jnp.int32), nreal, nxt, last.astype(jnp.int32), short.astype(jnp.int32))
    shift = base % GATHER_CHUNK
    span = jnp.where(counts > 0, (counts + shift + GATHER_CHUNK - 1) // GATHER_CHUNK * GATHER_CHUNK, 0)
    off = jnp.cumsum(span, axis=1) - span
    max_pieces = ROUTE_TILE * TOP_K // GATHER_CHUNK + 2 * N_EXPERTS
    piece = jnp.arange(max_pieces, dtype=jnp.int32)[None, :, None] * GATHER_CHUNK
    in_run = (piece >= off[:, None, :]) & (piece < (off + span)[:, None, :])
    piece_src = jnp.sum(jnp.where(in_run, (base - shift - off)[:, None, :] + piece, 0), axis=-1)
    n_piece = jnp.sum(span, axis=1) // GATHER_CHUNK
    runs = (n_piece.astype(jnp.int32), piece_src.reshape(-1).astype(jnp.int32),
            (off + shift).astype(jnp.int32)[:, None, :])
    return dest, cap, sched, (pstart + tot).astype(jnp.int32), (padded - tot).astype(jnp.int32), runs


def _row_copy(src, src_row, dst, dst_row, sem):
    return pltpu.make_async_copy(src.at[pl.ds(src_row, 1)], dst.at[pl.ds(dst_row, 1)], sem)


def _dispatch_kernel(pad0_ref, padn_ref, nreal_ref, hp_ref, dest_ref, xs_ref, zblk, sem, zsem, csem, bsem):
    i = pl.program_id(0)
    tm = hp_ref.shape[0]
    nblk = xs_ref.shape[0] // EXPERT_BLK

    def for_each_pad_row(fn):
        def per_expert(e, c):
            p0 = pad0_ref[e]
            n = padn_ref[e]
            head = jnp.minimum(n, (-p0) & (SUBLANES - 1))

            def per_row(r, c2):
                fn(_row_copy(zblk, 0, xs_ref, p0 + r, zsem))
                return c2
            lax.fori_loop(0, head, per_row, 0)

            def per_tile(j, c2):
                r0 = pl.multiple_of(p0 + head + j * SUBLANES, SUBLANES)
                fn(pltpu.make_async_copy(zblk.at[pl.ds(0, SUBLANES)], xs_ref.at[pl.ds(r0, SUBLANES)], csem))
                return c2
            lax.fori_loop(0, lax.shift_right_logical(n - head, SUBLANES.bit_length() - 1), per_tile, 0)
            return c
        lax.fori_loop(0, N_EXPERTS, per_expert, 0)

    def for_each_unused_block(fn):
        def per_block(b, c):
            r0 = pl.multiple_of(b * EXPERT_BLK, EXPERT_BLK)
            fn(pltpu.make_async_copy(zblk, xs_ref.at[pl.ds(r0, EXPERT_BLK)], bsem))
            return c
        lax.fori_loop(nreal_ref[0], nblk, per_block, 0)

    @pl.when(i == 0)
    def _():
        zblk[...] = jnp.zeros_like(zblk)
        for_each_pad_row(lambda cp: cp.start())
        for_each_unused_block(lambda cp: cp.start())

    def issue(r, c):
        for k in range(TOP_K):
            _row_copy(hp_ref, r, xs_ref, dest_ref[r * TOP_K + k], sem).start(priority=k % 2)
        return c
    lax.fori_loop(0, tm, issue, 0, unroll=2)

    def drain(r, c):
        for k in range(TOP_K):
            _row_copy(hp_ref, 0, xs_ref, 0, sem).wait()
        return c
    lax.fori_loop(0, tm, drain, 0)

    @pl.when(i == 0)
    def _():
        for_each_pad_row(lambda cp: cp.wait())
        for_each_unused_block(lambda cp: cp.wait())


def _dispatch(hp, dest, pad0, padn, nreal, cap, tm=MOVE_TILE):
    n_tok, half = hp.shape
    return pl.pallas_call(
        _dispatch_kernel,
        grid_spec=pltpu.PrefetchScalarGridSpec(
            num_scalar_prefetch=3,
            grid=(n_tok // tm,),
            in_specs=[pl.BlockSpec((tm, half), lambda i, *_: (i, 0)),
                      pl.BlockSpec((tm * TOP_K,), lambda i, *_: (i,), memory_space=pltpu.SMEM)],
            out_specs=pl.BlockSpec(memory_space=pl.ANY),
            scratch_shapes=[pltpu.VMEM((EXPERT_BLK, half), jnp.uint32)] + [pltpu.SemaphoreType.DMA] * 4),
        out_shape=jax.ShapeDtypeStruct((cap, half), jnp.uint32),
        compiler_params=_cparams(("arbitrary",)),
        name="moe_dispatch",
    )(pad0, padn, nreal, hp, dest)


def _combine_kernel(npiece_ref, src_ref, idx_ref, rank_ref, gate_ref, runrow_ref, x1_ref, mod_ref, ln_ref,
                    ys_ref, o_ref, buf, sem, *, tn):
    i = pl.program_id(0)
    tm = x1_ref.shape[0]
    rows = buf.shape[1]

    max_pieces = rows // GATHER_CHUNK

    def for_each_piece(tile, fn):
        slot = tile % 2

        def per_piece(p, c):
            src = pl.multiple_of(src_ref[tile * max_pieces + p], GATHER_CHUNK)
            dst = pl.multiple_of(p * GATHER_CHUNK, GATHER_CHUNK)
            fn(pltpu.make_async_copy(ys_ref.at[pl.ds(src, GATHER_CHUNK)],
                                     buf.at[slot, pl.ds(dst, GATHER_CHUNK)], sem.at[slot]))
            return c
        lax.fori_loop(0, npiece_ref[tile], per_piece, 0)

    @pl.when(i == 0)
    def _():
        buf[...] = jnp.zeros_like(buf)
        for_each_piece(0, lambda cp: cp.start())

    @pl.when(i + 1 < pl.num_programs(0))
    def _():
        for_each_piece(i + 1, lambda cp: cp.start())

    for_each_piece(i, lambda cp: cp.wait())

    lane = lax.broadcasted_iota(jnp.int32, (tm, N_EXPERTS), 1)
    col = lax.broadcasted_iota(jnp.int32, (tm, rows), 1)
    sel = jnp.zeros((tm, rows), F32)
    for k in range(TOP_K):
        run_row = jnp.sum(jnp.where(lane == idx_ref[:, k:k + 1], runrow_ref[...], 0), axis=-1, keepdims=True)
        sel = sel + jnp.where(col == run_row + rank_ref[:, k:k + 1], gate_ref[:, k:k + 1], 0.0)
    sel = sel.astype(BF16)

    cur = buf.at[i % 2]
    m = mod_ref[...]
    d = x1_ref.shape[1]
    hw = tn // 2
    cw = 512
    for w0 in range(0, d // 2, cw):
        wl, wh = _unpack_bf16_pairs(cur[:, w0:w0 + cw])
        lo = jnp.dot(sel, wl.astype(BF16), preferred_element_type=F32)
        hi = jnp.dot(sel, wh.astype(BF16), preferred_element_type=F32)
        for half, moe in ((0, lo), (1, hi)):
            c0 = (w0 // hw) * tn + half * hw + w0 % hw
            o_ref[:, c0:c0 + cw] = DN_ALPHA * x1_ref[:, c0:c0 + cw] + (1.0 + m[5:6, c0:c0 + cw]) * moe
    o_ref[...] = _layer_norm_rows(o_ref[...], ln_ref[0:1, :], ln_ref[1:2, :])


def _combine(ys, runs, top_idx, rank, gates, x1, mod3, ln2, tm=ROUTE_TILE, tn=DOWN_TILE):
    bsz, seqlen, d = x1.shape
    n_tok = bsz * seqlen
    per_b = seqlen // tm
    n_piece, piece_src, run_row = runs
    buf_rows = tm * TOP_K + N_EXPERTS * 2 * GATHER_CHUNK
    assert piece_src.shape[0] * GATHER_CHUNK == (n_tok // tm) * buf_rows
    tok = lambda i, *_: (i, 0)
    out = pl.pallas_call(
        functools.partial(_combine_kernel, tn=tn),
        grid_spec=pltpu.PrefetchScalarGridSpec(
            num_scalar_prefetch=2,
            grid=(n_tok // tm,),
            in_specs=[pl.BlockSpec((tm, TOP_K), tok), pl.BlockSpec((tm, TOP_K), tok), pl.BlockSpec((tm, TOP_K), tok),
                      pl.BlockSpec((None, 1, N_EXPERTS), lambda i, *_: (i, 0, 0)),
                      pl.BlockSpec((tm, d), tok),
                      pl.BlockSpec((None, 6, d), lambda i, *_: (i // per_b, 0, 0)),
                      pl.BlockSpec((2, d), lambda i, *_: (0, 0)),
                      pl.BlockSpec(memory_space=pl.ANY)],
            out_specs=pl.BlockSpec((tm, d), tok),
            scratch_shapes=[pltpu.VMEM((2, buf_rows, d // 2), jnp.uint32), pltpu.SemaphoreType.DMA((2,))]),
        out_shape=jax.ShapeDtypeStruct((n_tok, d), F32),
        compiler_params=_cparams(("arbitrary",)),
        name="moe_combine_ln",
    )(n_piece, piece_src, top_idx, rank, gates, run_row, x1.reshape(n_tok, d), mod3, ln2, ys)
    return out.reshape(bsz, seqlen, d)


def kernel(x, c, positions, w_ada, b_ada, w_in, attn_sinks, ssm_a_re, ssm_a_im, ssm_b_re, ssm_b_im,
           ssm_c_re, ssm_c_im, ssm_d, ssm_log_dt, ssm_w_glu, ssm_b_glu, g_attn_out, g_ssm_out, w_out,
           ln1_g, ln1_b, w_router, b_router, w_gate, b_gate, w_up, b_up, w_down, b_down, ln2_g, ln2_b):
    bsz, seqlen, d = x.shape
    lsub = seqlen // N_SUBSEQ
    n_tok = bsz * seqlen
    rope_tab = _rope_tables(positions)
    for l in range(w_ada.shape[0]):
        mod3 = _ada_mod(c, w_ada[l], b_ada[l]).reshape(bsz, 6, d)
        proj = _in_proj(x, mod3, rope_tab, w_in[l].astype(BF16))
        attn_n = _attention(proj, attn_sinks[l].astype(F32), g_attn_out[l].astype(F32))
        s5p = _s5_params(ssm_a_re[l], ssm_a_im[l], ssm_b_re[l], ssm_b_im[l], ssm_c_re[l], ssm_c_im[l],
                         ssm_d[l], ssm_log_dt[l], ssm_w_glu[l], ssm_b_glu[l], lsub)
        ssm = _s5(proj, s5p)
        wr_hi = w_router[l].astype(BF16)
        wr_lo = (w_router[l] - wr_hi.astype(F32)).astype(BF16)
        x1, hp, logits = _out_proj(
            attn_n, ssm, g_ssm_out[l].astype(F32), w_out[l].astype(BF16), x, mod3,
            jnp.stack([ln1_g[l], ln1_b[l]]).astype(F32),
            jnp.concatenate([wr_hi, wr_lo], axis=1), b_router[l].reshape(1, N_EXPERTS).astype(F32))
        top_idx, gates, rank, cnt = _route(logits.reshape(n_tok, N_EXPERTS))
        dest, cap, sched, pad0, padn, runs = _route_tables(cnt, top_idx, rank)
        xs = _dispatch(hp.reshape(n_tok, d // 2), dest, pad0, padn, sched[2], cap)
        ys = _experts(xs, sched, w_gate[l], b_gate[l], w_up[l], b_up[l], w_down[l], b_down[l])
        x = _combine(ys, runs, top_idx, rank, gates, x1, mod3, jnp.stack([ln2_g[l], ln2_b[l]]).astype(F32))
    return x
```

```python
import functools
import math

import jax
import jax.numpy as jnp
from jax import lax
from jax.experimental import pallas as pl
from jax.experimental.pallas import tpu as pltpu

F32 = jnp.float32
BF16 = jnp.bfloat16

HEAD_DIM = 64
N_Q_HEADS = 32
N_KV_HEADS = 4
GQ = N_Q_HEADS // N_KV_HEADS
ATTN_WIDTH = N_Q_HEADS * HEAD_DIM
KV_WIDTH = N_KV_HEADS * HEAD_DIM
QKV_WIDTH = ATTN_WIDTH + 2 * KV_WIDTH
BLK = 128
ROT_DIM = HEAD_DIM // 4
ROPE_THETA = 500000.0
GROUP_CH = 16
STATE = 64
N_EXPERTS = 32
TOP_K = 4
SWIGLU_LIMIT = 7.0
SWIGLU_ALPHA = 1.702
EXPERT_BLK = 256
ROUTE_TILE = 256
GATHER_CHUNK = 8
MOVE_TILE = 1024
DOWN_TILE = 4096
S5_STEPS = 512
CAST_ROWS = 128
DEPTH = 1
DN_ALPHA = (2.0 * DEPTH) ** 0.25
EPS = 1e-5

LANES = 128
SUBLANES = 8
N_SUBSEQ = SUBLANES
GROUPS_PER_BLK = 16
SSM_BLK_IN = GROUPS_PER_BLK * GROUP_CH
SSM_BLK_ST = GROUPS_PER_BLK * STATE
VMEM_LIMIT = 56 * 1024 * 1024


def _cparams(sem, vmem=VMEM_LIMIT):
    return pltpu.CompilerParams(dimension_semantics=sem, vmem_limit_bytes=vmem)


def _resident(shape, index_map):
    return pl.BlockSpec(shape, index_map, pipeline_mode=pl.Buffered(1))


def _ada_kernel(c_ref, w_ref, b_ref, o_ref):
    c = c_ref[...]
    ca = c * jax.nn.sigmoid(c)
    o_ref[...] = jnp.dot(ca.astype(BF16), w_ref[...].astype(BF16),
                         preferred_element_type=F32) + b_ref[...]


def _ada_mod(c, w_ada, b_ada, tn=1024):
    bsz, d = c.shape
    n = w_ada.shape[1]
    c8 = jnp.zeros((SUBLANES, d), F32).at[:bsz].set(c)
    out = pl.pallas_call(
        _ada_kernel,
        grid=(n // tn,),
        in_specs=[pl.BlockSpec((SUBLANES, d), lambda j: (0, 0)),
                  pl.BlockSpec((d, tn), lambda j: (0, j)),
                  pl.BlockSpec((1, tn), lambda j: (0, j))],
        out_specs=pl.BlockSpec((SUBLANES, tn), lambda j: (0, j)),
        out_shape=jax.ShapeDtypeStruct((SUBLANES, n), F32),
        compiler_params=_cparams(("arbitrary",)),
        name="ada_mod",
    )(c8, w_ada, b_ada.reshape(1, n))
    return out[:bsz]


def _rope(t, tab):
    c, s_lo, s_hi = tab[:, :LANES], tab[:, LANES:2 * LANES], tab[:, 2 * LANES:]
    half = ROT_DIM // 2
    out = []
    for j in range(t.shape[1] // LANES):
        tj = t[:, j * LANES:(j + 1) * LANES]
        out.append(tj * c + pltpu.roll(tj, LANES - half, 1) * s_lo + pltpu.roll(tj, half, 1) * s_hi)
    return jnp.concatenate(out, axis=1)


def _inproj_kernel(x_ref, mod_ref, tab_ref, w_ref, o_ref, *, nc):
    m = mod_ref[...]
    h = (x_ref[...] * (1.0 + m[1:2, :]) + m[0:1, :]).astype(BF16)
    for n0 in range(0, o_ref.shape[-1], nc):
        p = jnp.dot(h, w_ref[:, n0:n0 + nc], preferred_element_type=F32)
        if n0 < ATTN_WIDTH:
            p = _rope(p * (HEAD_DIM ** -0.5), tab_ref[...])
        elif n0 == ATTN_WIDTH:
            p = jnp.concatenate([_rope(p[:, :KV_WIDTH], tab_ref[...]), p[:, KV_WIDTH:]], axis=1)
        o_ref[:, n0:n0 + nc] = p.astype(BF16)


def _in_proj(x, mod3, rope_tab, w_in_bf, tm=256, nc=512):
    bsz, seqlen, d = x.shape
    n_in = w_in_bf.shape[1]
    assert ATTN_WIDTH % nc == 0 and nc >= 2 * KV_WIDTH
    return pl.pallas_call(
        functools.partial(_inproj_kernel, nc=nc),
        grid=(bsz, seqlen // tm),
        in_specs=[pl.BlockSpec((None, tm, d), lambda b, i: (b, i, 0)),
                  pl.BlockSpec((None, 6, d), lambda b, i: (b, 0, 0)),
                  pl.BlockSpec((None, tm, 3 * LANES), lambda b, i: (b, i, 0)),
                  _resident((d, n_in), lambda b, i: (0, 0))],
        out_specs=pl.BlockSpec((None, tm, n_in), lambda b, i: (b, i, 0)),
        out_shape=jax.ShapeDtypeStruct((bsz, seqlen, n_in), BF16),
        compiler_params=_cparams(("arbitrary", "arbitrary")),
        name="in_proj",
    )(x, mod3, rope_tab, w_in_bf)


def _attn_kernel(sink_ref, q_ref, kc_ref, kp_ref, vc_ref, vp_ref, g_ref, o_ref):
    n = pl.program_id(1)
    low = lax.broadcasted_iota(jnp.int32, (2 * BLK, LANES), 1) < HEAD_DIM
    low_q = lax.broadcasted_iota(jnp.int32, (BLK, LANES), 1) < HEAD_DIM

    k_raw = jnp.concatenate([kp_ref[...], kc_ref[...]], axis=0).astype(F32)
    v_raw = jnp.concatenate([vp_ref[...], vc_ref[...]], axis=0).astype(F32)

    qi = lax.broadcasted_iota(jnp.int32, (BLK, BLK), 0)
    kj = lax.broadcasted_iota(jnp.int32, (BLK, BLK), 1)
    own = kj <= qi
    prev_ok = kj >= jnp.where(n > 0, 0, BLK)

    o_chunks = []
    for hk in range(N_KV_HEADS):
        kc = k_raw[:, (hk // 2) * LANES:(hk // 2 + 1) * LANES]
        vc = v_raw[:, (hk // 2) * LANES:(hk // 2 + 1) * LANES]
        k_sw = pltpu.roll(kc, HEAD_DIM, 1)
        v_sw = pltpu.roll(vc, HEAD_DIM, 1)
        if hk % 2 == 0:
            kk2 = jnp.where(low, kc, k_sw)
            v_lo = jnp.where(low, vc, 0.0)
            v_hi = jnp.where(low, 0.0, v_sw)
        else:
            kk2 = jnp.where(low, k_sw, kc)
            v_lo = jnp.where(low, v_sw, 0.0)
            v_hi = jnp.where(low, 0.0, vc)
        kk2 = kk2.astype(BF16)
        v_lo = v_lo.astype(BF16)
        v_hi = v_hi.astype(BF16)
        lhs = []
        for j in range(GQ // 2):
            c0 = (hk * (GQ // 2) + j) * LANES
            q2 = q_ref[:, c0:c0 + LANES].astype(F32)
            lhs.append(jnp.where(low_q, q2, 0.0).astype(BF16))
            lhs.append(jnp.where(low_q, 0.0, q2).astype(BF16))
        s_all = lax.dot_general(jnp.concatenate(lhs, axis=0), kk2,
                                (((1,), (1,)), ((), ())), preferred_element_type=F32)
        for j in range(GQ // 2):
            acc = None
            for side, vv in ((0, v_lo), (1, v_hi)):
                i = 2 * j + side
                s_prev = jnp.where(prev_ok, s_all[i * BLK:(i + 1) * BLK, :BLK], -1e30)
                s = jnp.where(own, s_all[i * BLK:(i + 1) * BLK, BLK:], s_prev)
                sink = sink_ref[hk * GQ + i]
                m = jnp.maximum(jnp.max(s, axis=-1, keepdims=True), sink)
                p = jnp.exp(s - m)
                denom = jnp.sum(p, axis=-1, keepdims=True) + jnp.exp(sink - m)
                p = p * (1.0 / denom)
                p = jnp.concatenate([jnp.where(own, 0.0, p), jnp.where(own, p, 0.0)], axis=1).astype(BF16)
                o = jnp.dot(p, vv, preferred_element_type=F32)
                acc = o if acc is None else acc + o
            o_chunks.append(acc)

    ssq = None
    for oc in o_chunks:
        t = jnp.sum(oc * oc, axis=-1, keepdims=True)
        ssq = t if ssq is None else ssq + t
    inv = lax.rsqrt(ssq * (1.0 / ATTN_WIDTH) + EPS)
    for j, oc in enumerate(o_chunks):
        o_ref[:, j * LANES:(j + 1) * LANES] = (oc * inv * g_ref[:, j * LANES:(j + 1) * LANES]).astype(BF16)


def _attention(qkv, sinks, g_attn):
    bsz, seqlen, _ = qkv.shape
    nb = seqlen // BLK
    kcol = ATTN_WIDTH // KV_WIDTH
    cur = lambda b, n: (b, n, 0)
    return pl.pallas_call(
        _attn_kernel,
        grid=(bsz, nb),
        in_specs=[pl.BlockSpec(memory_space=pltpu.SMEM),
                  pl.BlockSpec((None, BLK, ATTN_WIDTH), cur),
                  pl.BlockSpec((None, BLK, KV_WIDTH), lambda b, n: (b, n, kcol)),
                  pl.BlockSpec((None, BLK, KV_WIDTH), lambda b, n: (b, jnp.maximum(n - 1, 0), kcol)),
                  pl.BlockSpec((None, BLK, KV_WIDTH), lambda b, n: (b, n, kcol + 1)),
                  pl.BlockSpec((None, BLK, KV_WIDTH), lambda b, n: (b, jnp.maximum(n - 1, 0), kcol + 1)),
                  pl.BlockSpec((1, ATTN_WIDTH), lambda b, n: (0, 0))],
        out_specs=pl.BlockSpec((None, BLK, ATTN_WIDTH), cur),
        out_shape=jax.ShapeDtypeStruct((bsz, seqlen, ATTN_WIDTH), BF16),
        compiler_params=_cparams(("arbitrary", "arbitrary")),
        name="swa_attention",
    )(sinks, qkv, qkv, qkv, qkv, qkv, g_attn.reshape(1, ATTN_WIDTH))


def _rope_tables(positions):
    half = ROT_DIM // 2
    inv_freq = ROPE_THETA ** (-jnp.arange(0, ROT_DIM, 2, dtype=F32) / ROT_DIM)
    ang = positions.astype(F32)[..., None] * inv_freq
    cos, sin = jnp.cos(ang), jnp.sin(ang)
    shp = cos.shape[:-1] + (HEAD_DIM - ROT_DIM,)
    c = jnp.concatenate([cos, cos, jnp.ones(shp, F32)], axis=-1)
    z8 = jnp.zeros_like(sin)
    s_lo = jnp.concatenate([-sin, z8, jnp.zeros(shp, F32)], axis=-1)
    s_hi = jnp.concatenate([z8, sin, jnp.zeros(shp, F32)], axis=-1)
    rep = LANES // HEAD_DIM
    return jnp.concatenate([jnp.tile(c, rep), jnp.tile(s_lo, rep), jnp.tile(s_hi, rep)], axis=-1)


def _s5_scan(buf, ar, ai, hr, hi, ti, store):
    def step(i, carry):
        hr, hi = carry
        r0 = pl.multiple_of(i * SUBLANES, SUBLANES)
        row = buf[pl.ds(r0, SUBLANES), :]
        nhr = ar * hr - ai * hi + row[:, :SSM_BLK_ST]
        nhi = ar * hi + ai * hr + row[:, SSM_BLK_ST:]
        if store:
            buf[pl.ds(r0, SUBLANES), :] = jnp.concatenate([nhr, nhi], axis=1)
        return nhr, nhi
    return lax.fori_loop(0, ti, step, (hr, hi), unroll=4)


def _time_major(u_ref, ti):
    u = pltpu.einshape("jid->ijd", u_ref[...].astype(F32))
    return u.reshape(ti * N_SUBSEQ, u.shape[-1])


def _s5_pass1_kernel(u_ref, bdb_ref, a_ref, f_ref, buf, hst, *, ti):
    ic = pl.program_id(2)

    @pl.when(ic == 0)
    def _():
        hst[...] = jnp.zeros_like(hst)

    u = _time_major(u_ref, ti)
    buf[...] = jnp.dot(u.astype(BF16), bdb_ref[...], preferred_element_type=F32)
    ar = jnp.broadcast_to(a_ref[0:1, :], (SUBLANES, SSM_BLK_ST))
    ai = jnp.broadcast_to(a_ref[1:2, :], (SUBLANES, SSM_BLK_ST))
    hr, hi = _s5_scan(buf, ar, ai, hst[:, :SSM_BLK_ST], hst[:, SSM_BLK_ST:], ti, store=False)
    hst[...] = jnp.concatenate([hr, hi], axis=1)

    @pl.when(ic == pl.num_programs(2) - 1)
    def _():
        f_ref[...] = hst[...]


def _s5_pass2_kernel(u_ref, f_ref, bdb_ref, a_ref, bdc_ref, glu_ref, vec_ref, o_ref, buf, hst, *, ti):
    ic = pl.program_id(2)

    @pl.when(ic == 0)
    def _():
        fr, fi = f_ref[:, :SSM_BLK_ST], f_ref[:, SSM_BLK_ST:]
        pr = jnp.broadcast_to(a_ref[2:3, :], (SUBLANES, SSM_BLK_ST))
        pi = jnp.broadcast_to(a_ref[3:4, :], (SUBLANES, SSM_BLK_ST))
        row = lax.broadcasted_iota(jnp.int32, (SUBLANES, SSM_BLK_ST), 0)
        hr = jnp.zeros((SUBLANES, SSM_BLK_ST), F32)
        hi = jnp.zeros((SUBLANES, SSM_BLK_ST), F32)
        for _ in range(N_SUBSEQ - 1):
            nr = pr * hr - pi * hi + fr
            ni = pr * hi + pi * hr + fi
            hr = jnp.where(row == 0, 0.0, pltpu.roll(nr, 1, 0))
            hi = jnp.where(row == 0, 0.0, pltpu.roll(ni, 1, 0))
        hst[...] = jnp.concatenate([hr, hi], axis=1)

    u = _time_major(u_ref, ti)
    buf[...] = jnp.dot(u.astype(BF16), bdb_ref[...], preferred_element_type=F32)
    ar = jnp.broadcast_to(a_ref[0:1, :], (SUBLANES, SSM_BLK_ST))
    ai = jnp.broadcast_to(a_ref[1:2, :], (SUBLANES, SSM_BLK_ST))
    hr, hi = _s5_scan(buf, ar, ai, hst[:, :SSM_BLK_ST], hst[:, SSM_BLK_ST:], ti, store=True)
    hst[...] = jnp.concatenate([hr, hi], axis=1)

    y = jnp.dot(buf[...].astype(BF16), bdc_ref[...], preferred_element_type=F32)
    y = jax.nn.gelu(y + vec_ref[0:1, :SSM_BLK_IN] * u)
    z = jnp.dot(y.astype(BF16), glu_ref[...], preferred_element_type=F32) + vec_ref[1:2, :]
    out = z[:, :SSM_BLK_IN] * jax.nn.sigmoid(z[:, SSM_BLK_IN:])
    out = pltpu.einshape("ijd->jid", out.reshape(ti, N_SUBSEQ, SSM_BLK_IN))
    o_ref[...] = out.astype(BF16)


def _s5_params(a_re, a_im, b_re, b_im, c_re, c_im, d_skip, log_dt, w_glu, b_glu, lsub):
    g = a_re.shape[0]
    nf = g // GROUPS_PER_BLK
    a = lax.complex(a_re.astype(F32), a_im.astype(F32))
    dt = jnp.exp(log_dt.astype(F32))[:, None]
    a_bar = jnp.exp(a * dt)
    a_pow = jnp.exp(a * dt * lsub)
    b_bar = ((a_bar - 1.0) / a)[..., None] * lax.complex(b_re.astype(F32), b_im.astype(F32))
    def block_diag(m, inner):
        rows = m.shape[1]
        tile = jnp.tile(jnp.eye(inner, dtype=F32), (1, GROUPS_PER_BLK))
        wide = jnp.einsum('frk,kn->frn', m, tile, precision=lax.Precision.HIGHEST)
        rg = jnp.arange(rows, dtype=jnp.int32)[:, None] // (rows // GROUPS_PER_BLK)
        cg = jnp.arange(GROUPS_PER_BLK * inner, dtype=jnp.int32)[None, :] // inner
        return jnp.where(rg == cg, wide, 0.0)

    def bd_in(m):
        return block_diag(jnp.swapaxes(m, 1, 2).reshape(nf, SSM_BLK_IN, STATE), STATE)

    def bd_out(m):
        return block_diag(jnp.swapaxes(m, 1, 2).reshape(nf, SSM_BLK_ST, GROUP_CH), GROUP_CH)

    def bd_glu(m):
        return block_diag(m.reshape(nf, SSM_BLK_IN, GROUP_CH), GROUP_CH)

    bdb = jnp.concatenate([bd_in(jnp.real(b_bar)), bd_in(jnp.imag(b_bar))], axis=2).astype(BF16)
    bdc = jnp.concatenate([bd_out(c_re.astype(F32)), bd_out(-c_im.astype(F32))], axis=1).astype(BF16)
    wg = w_glu.astype(F32)
    glu = jnp.concatenate([bd_glu(wg[..., :GROUP_CH]), bd_glu(wg[..., GROUP_CH:])], axis=2).astype(BF16)
    flat = lambda m: m.reshape(nf, 1, SSM_BLK_ST)
    avec = jnp.concatenate([flat(jnp.real(a_bar)), flat(jnp.imag(a_bar)),
                            flat(jnp.real(a_pow)), flat(jnp.imag(a_pow))], axis=1)
    bg = b_glu.astype(F32).reshape(nf, GROUPS_PER_BLK, 2 * GROUP_CH)
    bvec = jnp.concatenate([bg[..., :GROUP_CH].reshape(nf, 1, SSM_BLK_IN),
                            bg[..., GROUP_CH:].reshape(nf, 1, SSM_BLK_IN)], axis=2)
    dvec = jnp.concatenate([d_skip.astype(F32).reshape(nf, 1, SSM_BLK_IN),
                            jnp.zeros((nf, 1, SSM_BLK_IN), F32)], axis=2)
    vec = jnp.concatenate([dvec, bvec], axis=1)
    return bdb, bdc, glu, avec, vec


def _s5(proj, params, ti=S5_STEPS):
    bdb, bdc, glu, avec, vec = params
    bsz, seqlen, n_in = proj.shape
    width = n_in - QKV_WIDTH
    nf = width // SSM_BLK_IN
    lsub = seqlen // N_SUBSEQ
    ti = min(ti, lsub)
    u_col0 = QKV_WIDTH // SSM_BLK_IN
    p4 = proj.reshape(bsz, N_SUBSEQ, lsub, n_in)
    grid = (bsz, nf, lsub // ti)
    u_spec = pl.BlockSpec((None, N_SUBSEQ, ti, SSM_BLK_IN), lambda b, f, i: (b, 0, i, u_col0 + f))
    o_spec = pl.BlockSpec((None, N_SUBSEQ, ti, SSM_BLK_IN), lambda b, f, i: (b, 0, i, f))
    blk = lambda r, c: pl.BlockSpec((None, r, c), lambda b, f, i: (f, 0, 0))
    f_spec = pl.BlockSpec((None, None, N_SUBSEQ, 2 * SSM_BLK_ST), lambda b, f, i: (b, f, 0, 0))
    scratch = [pltpu.VMEM((ti * SUBLANES, 2 * SSM_BLK_ST), F32),
               pltpu.VMEM((SUBLANES, 2 * SSM_BLK_ST), F32)]
    sem = ("arbitrary", "arbitrary", "arbitrary")
    fin = pl.pallas_call(
        functools.partial(_s5_pass1_kernel, ti=ti),
        grid=grid,
        in_specs=[u_spec, blk(SSM_BLK_IN, 2 * SSM_BLK_ST), blk(4, SSM_BLK_ST)],
        out_specs=f_spec,
        out_shape=jax.ShapeDtypeStruct((bsz, nf, N_SUBSEQ, 2 * SSM_BLK_ST), F32),
        scratch_shapes=scratch,
        compiler_params=_cparams(sem),
        name="s5_pass1",
    )(p4, bdb, avec)
    out = pl.pallas_call(
        functools.partial(_s5_pass2_kernel, ti=ti),
        grid=grid,
        in_specs=[u_spec, f_spec, blk(SSM_BLK_IN, 2 * SSM_BLK_ST), blk(4, SSM_BLK_ST),
                  blk(2 * SSM_BLK_ST, SSM_BLK_IN), blk(SSM_BLK_IN, 2 * SSM_BLK_IN),
                  blk(2, 2 * SSM_BLK_IN)],
        out_specs=o_spec,
        out_shape=jax.ShapeDtypeStruct((bsz, N_SUBSEQ, lsub, width), BF16),
        scratch_shapes=scratch,
        compiler_params=_cparams(sem),
        name="s5_pass2",
    )(p4, fin, bdb, avec, bdc, glu, vec)
    return out.reshape(bsz, seqlen, width)


def _layer_norm_rows(y, g, b):
    mu = jnp.mean(y, axis=-1, keepdims=True)
    yc = y - mu
    var = jnp.mean(yc * yc, axis=-1, keepdims=True)
    return yc * lax.rsqrt(var + EPS) * g + b


def _pack_bf16_pairs(lo, hi):
    lo_bits = lax.bitcast_convert_type(lo.astype(BF16).astype(F32), jnp.uint32)
    hi_bits = lax.bitcast_convert_type(hi.astype(BF16).astype(F32), jnp.uint32)
    return (lo_bits >> 16) | (hi_bits & jnp.uint32(0xFFFF0000))


def _unpack_bf16_pairs(w):
    lo = lax.bitcast_convert_type(w << 16, F32)
    hi = lax.bitcast_convert_type(w & jnp.uint32(0xFFFF0000), F32)
    return lo, hi


def _outproj_kernel(attn_ref, ssm_ref, gs_ref, w_ref, x_ref, mod_ref, ln_ref, wr_ref, br_ref,
                    x1_ref, hp_ref, logit_ref, ybuf, *, nc):
    m = mod_ref[...]
    ssm = ssm_ref[...].astype(F32)
    ms = jnp.mean(ssm * ssm, axis=-1, keepdims=True)
    ssm_n = (ssm * lax.rsqrt(ms + EPS) * gs_ref[...]).astype(BF16)
    attn = attn_ref[...]
    ka = attn.shape[1]
    d = x_ref.shape[1]
    for n0 in range(0, d, nc):
        mix = (jnp.dot(attn, w_ref[:ka, n0:n0 + nc], preferred_element_type=F32)
               + jnp.dot(ssm_n, w_ref[ka:, n0:n0 + nc], preferred_element_type=F32))
        ybuf[:, n0:n0 + nc] = DN_ALPHA * x_ref[:, n0:n0 + nc] + (1.0 + m[2:3, n0:n0 + nc]) * mix
    x1 = _layer_norm_rows(ybuf[...], ln_ref[0:1, :], ln_ref[1:2, :])
    x1_ref[...] = x1
    h2 = x1 * (1.0 + m[4:5, :]) + m[3:4, :]
    hi = h2.astype(BF16)
    hi_f32 = hi.astype(F32)
    bits = lax.bitcast_convert_type(hi_f32, jnp.uint32)
    hp_ref[...] = (bits[:, :d // 2] >> 16) | (bits[:, d // 2:] & jnp.uint32(0xFFFF0000))
    lo = (h2 - hi_f32).astype(BF16)
    tm = h2.shape[0]
    r = jnp.dot(jnp.concatenate([hi, lo], axis=0), wr_ref[...], preferred_element_type=F32)
    logit_ref[...] = r[:tm, :N_EXPERTS] + r[:tm, N_EXPERTS:] + r[tm:, :N_EXPERTS] + br_ref[...]


def _route_kernel(logit_ref, idx_ref, gate_ref, rank_ref, cnt_ref):
    logits = logit_ref[...]
    tm = logits.shape[0]
    lane = lax.broadcasted_iota(jnp.int32, logits.shape, 1)
    vals, idxs = [], []
    for _ in range(TOP_K):
        mx = jnp.max(logits, axis=-1, keepdims=True)
        ix = jnp.min(jnp.where(logits == mx, lane, N_EXPERTS), axis=-1, keepdims=True)
        vals.append(mx)
        idxs.append(ix)
        logits = jnp.where(lane == ix, -jnp.inf, logits)
    tv = jnp.concatenate(vals, axis=1)
    e = jnp.exp(tv - vals[0])
    gate_ref[...] = e / jnp.sum(e, axis=-1, keepdims=True)
    idx_ref[...] = jnp.concatenate(idxs, axis=1)
    tri = (lax.broadcasted_iota(jnp.int32, (tm, tm), 0) > lax.broadcasted_iota(jnp.int32, (tm, tm), 1)).astype(BF16)
    run = jnp.zeros((1, N_EXPERTS), F32)
    ranks = []
    for ix in idxs:
        onehot = (lane == ix).astype(F32)
        before = jnp.dot(tri, onehot.astype(BF16), preferred_element_type=F32) + run
        ranks.append(jnp.sum(onehot * before, axis=-1, keepdims=True))
        run = run + jnp.sum(onehot, axis=0, keepdims=True)
    rank_ref[...] = jnp.concatenate(ranks, axis=1).astype(jnp.int32)
    cnt_ref[...] = jnp.zeros_like(cnt_ref)
    cnt_ref[0:1, 0:N_EXPERTS] = run


def _out_proj(attn_n, ssm, g_ssm, w_out_bf, x, mod3, ln1, wr, br, tm=128):
    bsz, seqlen, d = x.shape
    ka = attn_n.shape[-1]
    ks = w_out_bf.shape[0] - ka
    row = lambda b, i: (b, i, 0)
    const = lambda b, i: (0, 0)
    return pl.pallas_call(
        functools.partial(_outproj_kernel, nc=512),
        grid=(bsz, seqlen // tm),
        in_specs=[pl.BlockSpec((None, tm, ka), row),
                  pl.BlockSpec((None, tm, ks), row),
                  pl.BlockSpec((1, ks), const),
                  _resident((ka + ks, d), const),
                  pl.BlockSpec((None, tm, d), row),
                  pl.BlockSpec((None, 6, d), lambda b, i: (b, 0, 0)),
                  pl.BlockSpec((2, d), const),
                  pl.BlockSpec((d, 2 * N_EXPERTS), const),
                  pl.BlockSpec((1, N_EXPERTS), const)],
        out_specs=[pl.BlockSpec((None, tm, d), row),
                   pl.BlockSpec((None, tm, d // 2), row),
                   pl.BlockSpec((None, tm, N_EXPERTS), row)],
        out_shape=[jax.ShapeDtypeStruct((bsz, seqlen, d), F32),
                   jax.ShapeDtypeStruct((bsz, seqlen, d // 2), jnp.uint32),
                   jax.ShapeDtypeStruct((bsz, seqlen, N_EXPERTS), F32)],
        scratch_shapes=[pltpu.VMEM((tm, d), F32)],
        compiler_params=_cparams(("arbitrary", "arbitrary")),
        name="out_proj_ln_router",
    )(attn_n, ssm, g_ssm.reshape(1, ks), w_out_bf, x, mod3, ln1, wr, br)


def _route(logits, tm=ROUTE_TILE):
    n_tok = logits.shape[0]
    nt = n_tok // tm
    row = lambda i: (i, 0)
    k_shape = lambda dt: jax.ShapeDtypeStruct((n_tok, TOP_K), dt)
    return pl.pallas_call(
        _route_kernel,
        grid=(nt,),
        in_specs=[pl.BlockSpec((tm, N_EXPERTS), row)],
        out_specs=[pl.BlockSpec((tm, TOP_K), row), pl.BlockSpec((tm, TOP_K), row), pl.BlockSpec((tm, TOP_K), row),
                   pl.BlockSpec((None, SUBLANES, LANES), lambda i: (i, 0, 0))],
        out_shape=[k_shape(jnp.int32), k_shape(F32), k_shape(jnp.int32),
                   jax.ShapeDtypeStruct((nt, SUBLANES, LANES), F32)],
        compiler_params=_cparams(("arbitrary",)),
        name="route_topk",
    )(logits)


def _stream_expert_weights(blk_e, first, nxt, last, w_hbm, stage, wbf, sem):
    ct = pl.program_id(0)
    rb = pl.program_id(1)
    width = wbf[0].shape[1]

    def copies(e, col_tile):
        c0 = pl.multiple_of(col_tile * width, width)
        return [pltpu.make_async_copy(w.at[e, :, pl.ds(c0, width)], s, sem.at[j])
                for j, (w, s) in enumerate(zip(w_hbm, stage))]

    @pl.when((ct == 0) & (rb == 0))
    def _():
        for cp in copies(blk_e[0], 0):
            cp.start(priority=1)

    @pl.when(first[rb] == 1)
    def _():
        for cp in copies(blk_e[rb], ct):
            cp.wait()
        def convert(c, carry):
            r0 = pl.multiple_of(c * CAST_ROWS, CAST_ROWS)
            for s, w in zip(stage, wbf):
                w[pl.ds(r0, CAST_ROWS), :] = s[pl.ds(r0, CAST_ROWS), :].astype(BF16)
            return carry
        lax.fori_loop(0, stage[0].shape[0] // CAST_ROWS, convert, 0)

        @pl.when(last[rb] == 0)
        def _():
            for cp in copies(nxt[rb], ct):
                cp.start(priority=1)

        @pl.when((last[rb] == 1) & (ct + 1 < pl.num_programs(0)))
        def _():
            for cp in copies(nxt[rb], ct + 1):
                cp.start(priority=1)


def _for_live_rows(rb, nreal, short, out_ref, compute):
    full = out_ref.shape[0]
    for rows, is_short in ((full, 0), (full // 2, 1)):
        @pl.when((rb < nreal[0]) & (short[rb] == is_short))
        def _(rows=rows):
            compute(rows)
            if rows < full:
                out_ref[rows:, :] = jnp.zeros((full - rows, out_ref.shape[1]), out_ref.dtype)

    @pl.when(rb >= nreal[0])
    def _():
        out_ref[...] = jnp.zeros_like(out_ref)


def _expert_up_kernel(blk_e, first, nreal, nxt, last, short, xs_ref, wg_hbm, wu_hbm, bg_ref, bu_ref, act_ref,
                      stg_g, stg_u, wgb, wub, sem):
    rb = pl.program_id(1)
    _stream_expert_weights(blk_e, first, nxt, last, (wg_hbm, wu_hbm), (stg_g, stg_u), (wgb, wub), sem)

    def compute(rows):
        x = jnp.concatenate(_unpack_bf16_pairs(xs_ref[:rows, :]), axis=1).astype(BF16)
        g = jnp.dot(x, wgb[...], preferred_element_type=F32) + bg_ref[...]
        up = jnp.dot(x, wub[...], preferred_element_type=F32) + bu_ref[...]
        g = jnp.minimum(g, SWIGLU_LIMIT)
        up = jnp.clip(up, -SWIGLU_LIMIT, SWIGLU_LIMIT)
        act_ref[:rows, :] = (g * jax.nn.sigmoid(SWIGLU_ALPHA * g) * (up + 1.0)).astype(BF16)

    _for_live_rows(rb, nreal, short, act_ref, compute)


def _expert_down_kernel(blk_e, first, nreal, nxt, last, short, act_ref, wd_hbm, bd_ref, y_ref, stg, wdb, sem):
    rb = pl.program_id(1)
    _stream_expert_weights(blk_e, first, nxt, last, (wd_hbm,), (stg,), (wdb,), sem)

    def compute(rows):
        y = jnp.dot(act_ref[:rows, :], wdb[...], preferred_element_type=F32) + bd_ref[...]
        half = y.shape[1] // 2
        y_ref[:rows, :] = _pack_bf16_pairs(y[:, :half], y[:, half:])

    _for_live_rows(rb, nreal, short, y_ref, compute)


def _experts(xs, sched, w_gate, b_gate, w_up, b_up, w_down, b_down, tf=512, tn=DOWN_TILE):
    cap = xs.shape[0]
    n_e, d, dff = w_gate.shape
    nblk = cap // EXPERT_BLK
    hbm = pl.BlockSpec(memory_space=pl.ANY)
    act = pl.pallas_call(
        _expert_up_kernel,
        grid_spec=pltpu.PrefetchScalarGridSpec(
            num_scalar_prefetch=6,
            grid=(dff // tf, nblk),
            in_specs=[pl.BlockSpec((EXPERT_BLK, d // 2), lambda f, r, be, *_: (r, 0)),
                      hbm, hbm,
                      pl.BlockSpec((None, 1, tf), lambda f, r, be, *_: (be[r], 0, f)),
                      pl.BlockSpec((None, 1, tf), lambda f, r, be, *_: (be[r], 0, f))],
            out_specs=pl.BlockSpec((EXPERT_BLK, tf), lambda f, r, be, *_: (r, f)),
            scratch_shapes=[pltpu.VMEM((d, tf), F32), pltpu.VMEM((d, tf), F32),
                            pltpu.VMEM((d, tf), BF16), pltpu.VMEM((d, tf), BF16),
                            pltpu.SemaphoreType.DMA((2,))]),
        out_shape=jax.ShapeDtypeStruct((cap, dff), BF16),
        compiler_params=_cparams(("arbitrary", "arbitrary")),
        name="expert_gate_up",
    )(*sched, xs, w_gate, w_up, b_gate.reshape(n_e, 1, dff), b_up.reshape(n_e, 1, dff))
    ys = pl.pallas_call(
        _expert_down_kernel,
        grid_spec=pltpu.PrefetchScalarGridSpec(
            num_scalar_prefetch=6,
            grid=(d // tn, nblk),
            in_specs=[pl.BlockSpec((EXPERT_BLK, dff), lambda n, r, be, *_: (r, 0)),
                      hbm,
                      pl.BlockSpec((None, 1, tn), lambda n, r, be, *_: (be[r], 0, n))],
            out_specs=pl.BlockSpec((EXPERT_BLK, tn // 2), lambda n, r, be, *_: (r, n)),
            scratch_shapes=[pltpu.VMEM((dff, tn), F32), pltpu.VMEM((dff, tn), BF16),
                            pltpu.SemaphoreType.DMA((1,))]),
        out_shape=jax.ShapeDtypeStruct((cap, d // 2), jnp.uint32),
        compiler_params=_cparams(("arbitrary", "arbitrary")),
        name="expert_down",
    )(*sched, act, w_down, b_down.reshape(n_e, 1, d))
    return ys


def _route_tables(cnt, top_idx, rank):
    n_assign = top_idx.size
    counts = cnt[:, 0, :N_EXPERTS].astype(jnp.int32)
    tot = jnp.sum(counts, axis=0)
    padded = (tot + EXPERT_BLK - 1) // EXPERT_BLK * EXPERT_BLK
    pend = jnp.cumsum(padded)
    pstart = pend - padded
    base = pstart[None, :] + jnp.cumsum(counts, axis=0) - counts
    experts = jnp.arange(N_EXPERTS, dtype=jnp.int32)
    idx_t = top_idx.reshape(-1, ROUTE_TILE * TOP_K)
    dest = jnp.sum(jnp.where(idx_t[..., None] == experts, base[:, None, :], 0), axis=-1)
    dest = (dest.reshape(-1) + rank.reshape(-1)).astype(jnp.int32)
    cap = ((n_assign + EXPERT_BLK - 1) // EXPERT_BLK) * EXPERT_BLK + N_EXPERTS * EXPERT_BLK
    nblk = cap // EXPERT_BLK
    blk = jnp.arange(nblk, dtype=jnp.int32)
    blk_e = jnp.sum((pend[None, :] <= blk[:, None] * EXPERT_BLK).astype(jnp.int32), axis=1)
    blk_e = jnp.minimum(blk_e, N_EXPERTS - 1).astype(jnp.int32)
    nreal = (pend[-1:] // EXPERT_BLK).astype(jnp.int32)
    real = blk < nreal[0]
    first = real & jnp.concatenate([jnp.ones((1,), bool), blk_e[1:] != blk_e[:-1]])
    starts = jnp.where(first, blk, nblk)
    nxt_blk = jnp.concatenate([lax.cummin(starts, reverse=True)[1:], jnp.full((1,), nblk, jnp.int32)])
    last = nxt_blk >= nblk
    nxt = jnp.where(last, blk_e[0], blk_e[jnp.minimum(nxt_blk, nblk - 1)]).astype(jnp.int32)
    live = jnp.sum(jnp.where(blk_e[:, None] == experts[None, :], (pstart + tot)[None, :], 0), axis=1) - blk * EXPERT_BLK
    short = real & (live <= EXPERT_BLK // 2)
    sched = (blk_e, first.astype(jnp.int32), nreal, nxt, last.astype(jnp.int32), short.astype(jnp.int32))
    shift = base % GATHER_CHUNK
    span = jnp.where(counts > 0, (counts + shift + GATHER_CHUNK - 1) // GATHER_CHUNK * GATHER_CHUNK, 0)
    off = jnp.cumsum(span, axis=1) - span
    max_pieces = ROUTE_TILE * TOP_K // GATHER_CHUNK + 2 * N_EXPERTS
    piece = jnp.arange(max_pieces, dtype=jnp.int32)[None, :, None] * GATHER_CHUNK
    in_run = (piece >= off[:, None, :]) & (piece < (off + span)[:, None, :])
    piece_src = jnp.sum(jnp.where(in_run, (base - shift - off)[:, None, :] + piece, 0), axis=-1)
    n_piece = jnp.sum(span, axis=1) // GATHER_CHUNK
    runs = (n_piece.astype(jnp.int32), piece_src.reshape(-1).astype(jnp.int32),
            (off + shift).astype(jnp.int32)[:, None, :])
    return dest, cap, sched, (pstart + tot).astype(jnp.int32), (padded - tot).astype(jnp.int32), runs


def _row_copy(src, src_row, dst, dst_row, sem):
    return pltpu.make_async_copy(src.at[pl.ds(src_row, 1)], dst.at[pl.ds(dst_row, 1)], sem)


def _dispatch_kernel(pad0_ref, padn_ref, nreal_ref, hp_ref, dest_ref, xs_ref, zblk, sem, zsem, csem, bsem):
    i = pl.program_id(0)
    tm = hp_ref.shape[0]
    nblk = xs_ref.shape[0] // EXPERT_BLK

    def for_each_pad_row(fn):
        def per_expert(e, c):
            p0 = pad0_ref[e]
            n = padn_ref[e]
            head = jnp.minimum(n, (-p0) & (SUBLANES - 1))

            def per_row(r, c2):
                fn(_row_copy(zblk, 0, xs_ref, p0 + r, zsem))
                return c2
            lax.fori_loop(0, head, per_row, 0)

            def per_tile(j, c2):
                r0 = pl.multiple_of(p0 + head + j * SUBLANES, SUBLANES)
                fn(pltpu.make_async_copy(zblk.at[pl.ds(0, SUBLANES)], xs_ref.at[pl.ds(r0, SUBLANES)], csem))
                return c2
            lax.fori_loop(0, lax.shift_right_logical(n - head, SUBLANES.bit_length() - 1), per_tile, 0)
            return c
        lax.fori_loop(0, N_EXPERTS, per_expert, 0)

    def for_each_unused_block(fn):
        def per_block(b, c):
            r0 = pl.multiple_of(b * EXPERT_BLK, EXPERT_BLK)
            fn(pltpu.make_async_copy(zblk, xs_ref.at[pl.ds(r0, EXPERT_BLK)], bsem))
            return c
        lax.fori_loop(nreal_ref[0], nblk, per_block, 0)

    @pl.when(i == 0)
    def _():
        zblk[...] = jnp.zeros_like(zblk)
        for_each_pad_row(lambda cp: cp.start())
        for_each_unused_block(lambda cp: cp.start())

    def issue(r, c):
        for k in range(TOP_K):
            _row_copy(hp_ref, r, xs_ref, dest_ref[r * TOP_K + k], sem).start(priority=k % 2)
        return c
    lax.fori_loop(0, tm, issue, 0, unroll=2)

    def drain(r, c):
        for k in range(TOP_K):
            _row_copy(hp_ref, 0, xs_ref, 0, sem).wait()
        return c
    lax.fori_loop(0, tm, drain, 0)

    @pl.when(i == 0)
    def _():
        for_each_pad_row(lambda cp: cp.wait())
        for_each_unused_block(lambda cp: cp.wait())


def _dispatch(hp, dest, pad0, padn, nreal, cap, tm=MOVE_TILE):
    n_tok, half = hp.shape
    return pl.pallas_call(
        _dispatch_kernel,
        grid_spec=pltpu.PrefetchScalarGridSpec(
            num_scalar_prefetch=3,
            grid=(n_tok // tm,),
            in_specs=[pl.BlockSpec((tm, half), lambda i, *_: (i, 0)),
                      pl.BlockSpec((tm * TOP_K,), lambda i, *_: (i,), memory_space=pltpu.SMEM)],
            out_specs=pl.BlockSpec(memory_space=pl.ANY),
            scratch_shapes=[pltpu.VMEM((EXPERT_BLK, half), jnp.uint32)] + [pltpu.SemaphoreType.DMA] * 4),
        out_shape=jax.ShapeDtypeStruct((cap, half), jnp.uint32),
        compiler_params=_cparams(("arbitrary",)),
        name="moe_dispatch",
    )(pad0, padn, nreal, hp, dest)


def _combine_kernel(npiece_ref, src_ref, idx_ref, rank_ref, gate_ref, runrow_ref, x1_ref, mod_ref, ln_ref,
                    ys_ref, o_ref, buf, sel_ref, sem, *, tn):
    i = pl.program_id(0)
    tm = x1_ref.shape[0]
    rows = buf.shape[1]

    max_pieces = rows // GATHER_CHUNK

    def for_each_piece(tile, fn):
        slot = tile % 2

        def per_piece(p, c):
            src = pl.multiple_of(src_ref[tile * max_pieces + p], GATHER_CHUNK)
            dst = pl.multiple_of(p * GATHER_CHUNK, GATHER_CHUNK)
            fn(pltpu.make_async_copy(ys_ref.at[pl.ds(src, GATHER_CHUNK)],
                                     buf.at[slot, pl.ds(dst, GATHER_CHUNK)], sem.at[slot]))
            return c
        lax.fori_loop(0, npiece_ref[tile], per_piece, 0)

    @pl.when(i == 0)
    def _():
        buf[...] = jnp.zeros_like(buf)
        for_each_piece(0, lambda cp: cp.start())

    @pl.when(i + 1 < pl.num_programs(0))
    def _():
        for_each_piece(i + 1, lambda cp: cp.start())

    for_each_piece(i, lambda cp: cp.wait())

    lane = lax.broadcasted_iota(jnp.int32, (tm, N_EXPERTS), 1)
    pos = [jnp.sum(jnp.where(lane == idx_ref[:, k:k + 1], runrow_ref[...], 0), axis=-1, keepdims=True)
           + rank_ref[:, k:k + 1] for k in range(TOP_K)]
    sel_w = 512
    for c0 in range(0, rows, sel_w):
        col = lax.broadcasted_iota(jnp.int32, (tm, sel_w), 1) + c0
        chunk = jnp.zeros((tm, sel_w), F32)
        for k in range(TOP_K):
            chunk = jnp.where(col == pos[k], gate_ref[:, k:k + 1], chunk)
        sel_ref[:, c0:c0 + sel_w] = chunk.astype(BF16)
    sel = sel_ref[...]

    cur = buf.at[i % 2]
    m = mod_ref[...]
    d = x1_ref.shape[1]
    hw = tn // 2
    cw = 512
    for w0 in range(0, d // 2, cw):
        wl, wh = _unpack_bf16_pairs(cur[:, w0:w0 + cw])
        lo = jnp.dot(sel, wl.astype(BF16), preferred_element_type=F32)
        hi = jnp.dot(sel, wh.astype(BF16), preferred_element_type=F32)
        for half, moe in ((0, lo), (1, hi)):
            c0 = (w0 // hw) * tn + half * hw + w0 % hw
            o_ref[:, c0:c0 + cw] = DN_ALPHA * x1_ref[:, c0:c0 + cw] + (1.0 + m[5:6, c0:c0 + cw]) * moe
    o_ref[...] = _layer_norm_rows(o_ref[...], ln_ref[0:1, :], ln_ref[1:2, :])


def _combine(ys, runs, top_idx, rank, gates, x1, mod3, ln2, tm=ROUTE_TILE, tn=DOWN_TILE):
    bsz, seqlen, d = x1.shape
    n_tok = bsz * seqlen
    per_b = seqlen // tm
    n_piece, piece_src, run_row = runs
    buf_rows = tm * TOP_K + N_EXPERTS * 2 * GATHER_CHUNK
    assert piece_src.shape[0] * GATHER_CHUNK == (n_tok // tm) * buf_rows
    tok = lambda i, *_: (i, 0)
    out = pl.pallas_call(
        functools.partial(_combine_kernel, tn=tn),
        grid_spec=pltpu.PrefetchScalarGridSpec(
            num_scalar_prefetch=2,
            grid=(n_tok // tm,),
            in_specs=[pl.BlockSpec((tm, TOP_K), tok), pl.BlockSpec((tm, TOP_K), tok), pl.BlockSpec((tm, TOP_K), tok),
                      pl.BlockSpec((None, 1, N_EXPERTS), lambda i, *_: (i, 0, 0)),
                      pl.BlockSpec((tm, d), tok),
                      pl.BlockSpec((None, 6, d), lambda i, *_: (i // per_b, 0, 0)),
                      pl.BlockSpec((2, d), lambda i, *_: (0, 0)),
                      pl.BlockSpec(memory_space=pl.ANY)],
            out_specs=pl.BlockSpec((tm, d), tok),
            scratch_shapes=[pltpu.VMEM((2, buf_rows, d // 2), jnp.uint32), pltpu.VMEM((tm, buf_rows), BF16),
                            pltpu.SemaphoreType.DMA((2,))]),
        out_shape=jax.ShapeDtypeStruct((n_tok, d), F32),
        compiler_params=_cparams(("arbitrary",)),
        name="moe_combine_ln",
    )(n_piece, piece_src, top_idx, rank, gates, run_row, x1.reshape(n_tok, d), mod3, ln2, ys)
    return out.reshape(bsz, seqlen, d)


def kernel(x, c, positions, w_ada, b_ada, w_in, attn_sinks, ssm_a_re, ssm_a_im, ssm_b_re, ssm_b_im,
           ssm_c_re, ssm_c_im, ssm_d, ssm_log_dt, ssm_w_glu, ssm_b_glu, g_attn_out, g_ssm_out, w_out,
           ln1_g, ln1_b, w_router, b_router, w_gate, b_gate, w_up, b_up, w_down, b_down, ln2_g, ln2_b):
    bsz, seqlen, d = x.shape
    lsub = seqlen // N_SUBSEQ
    n_tok = bsz * seqlen
    rope_tab = _rope_tables(positions)
    for l in range(w_ada.shape[0]):
        mod3 = _ada_mod(c, w_ada[l], b_ada[l]).reshape(bsz, 6, d)
        proj = _in_proj(x, mod3, rope_tab, w_in[l].astype(BF16))
        attn_n = _attention(proj, attn_sinks[l].astype(F32), g_attn_out[l].astype(F32))
        s5p = _s5_params(ssm_a_re[l], ssm_a_im[l], ssm_b_re[l], ssm_b_im[l], ssm_c_re[l], ssm_c_im[l],
                         ssm_d[l], ssm_log_dt[l], ssm_w_glu[l], ssm_b_glu[l], lsub)
        ssm = _s5(proj, s5p)
        wr_hi = w_router[l].astype(BF16)
        wr_lo = (w_router[l] - wr_hi.astype(F32)).astype(BF16)
        x1, hp, logits = _out_proj(
            attn_n, ssm, g_ssm_out[l].astype(F32), w_out[l].astype(BF16), x, mod3,
            jnp.stack([ln1_g[l], ln1_b[l]]).astype(F32),
            jnp.concatenate([wr_hi, wr_lo], axis=1), b_router[l].reshape(1, N_EXPERTS).astype(F32))
        top_idx, gates, rank, cnt = _route(logits.reshape(n_tok, N_EXPERTS))
        dest, cap, sched, pad0, padn, runs = _route_tables(cnt, top_idx, rank)
        xs = _dispatch(hp.reshape(n_tok, d // 2), dest, pad0, padn, sched[2], cap)
        ys = _experts(xs, sched, w_gate[l], b_gate[l], w_up[l], b_up[l], w_down[l], b_down[l])
        x = _combine(ys, runs, top_idx, rank, gates, x1, mod3, jnp.stack([ln2_g[l], ln2_b[l]]).astype(F32))
    return x
```
